```python
import jax, jax.numpy as jnp
from jax import lax
import numpy as np

D_MODEL = 1024
BATCH = 16
SEQ = 2048
DEPTH = 1
DEC_BATCH = 8
DEC_SEQ = 32
PAST_LEN = 4096

CHUNK = 64
D_HEAD = 64
H_A = 6
H_B = 6
H_M = 4
D_A = H_A * D_HEAD
D_B = H_B * D_HEAD
D_M = H_M * D_HEAD
D_MIX = D_A + D_B + D_M
A_LEFT_CHUNKS = 8
A_WINDOW = A_LEFT_CHUNKS * CHUNK
REL_CLIP = 256
H_IDX = 8
D_IDX = 32
TOPK_MAX = 256
N_MEM = 256
Q_BLOCK = 128
ROPE_THETA = 10000.0
EPS = 1e-6
ATTN_SCALE = D_HEAD ** -0.5
COL_SIZES = (D_A, D_A, D_A, D_A, D_B, D_B, D_B, D_B, D_M, D_M, H_IDX * D_IDX, D_IDX, H_IDX)
IN_COLS = sum(COL_SIZES)

kernel_name = 'hymba_chunk_band_dsa_memory_step'


def _rmsnorm(x, g):
    xf = x.astype(jnp.float32)
    y = xf * lax.rsqrt(jnp.mean(xf * xf, axis=-1, keepdims=True) + EPS)
    return (y * g.astype(jnp.float32)).astype(x.dtype)


def _rope(x, pos):
    d = x.shape[-1]
    half = d // 2
    inv_freq = ROPE_THETA ** (-jnp.arange(half, dtype=jnp.float32) * 2.0 / d)
    ang = pos.astype(jnp.float32)[:, None] * inv_freq[None, :]
    cos = jnp.cos(ang)[:, None, :]
    sin = jnp.sin(ang)[:, None, :]
    xf = x.astype(jnp.float32)
    x1, x2 = xf[..., :half], xf[..., half:]
    return jnp.concatenate([x1 * cos - x2 * sin, x2 * cos + x1 * sin], axis=-1).astype(x.dtype)


def _split_cols(z):
    offs = [int(o) for o in np.cumsum(COL_SIZES)[:-1]]
    return jnp.split(z, offs, axis=-1)


def _mixer_inputs(x, g, w_in, pos):
    B, T, _ = x.shape
    z = _rmsnorm(x, g) @ w_in
    qa, ka, va, ga, qb, kb, vb, gb, qm, gm, qi, ki, wi = _split_cols(z)
    hd = lambda t, n: t.reshape(B, T, n, D_HEAD)
    qa, ka, va = hd(qa, H_A), hd(ka, H_A), hd(va, H_A)
    qb, kb, vb = _rope(hd(qb, H_B), pos), _rope(hd(kb, H_B), pos), hd(vb, H_B)
    qm = hd(qm, H_M)
    qi = _rope(qi.reshape(B, T, H_IDX, D_IDX), pos)
    ki = _rope(ki.reshape(B, T, 1, D_IDX), pos)[:, :, 0]
    return qa, ka, va, ga, qb, kb, vb, gb, qm, gm, qi, ki, wi


def _memory_kv(mem, g, w_mem_kv):
    B, N, _ = mem.shape
    mk, mv = jnp.split(_rmsnorm(mem, g) @ w_mem_kv, 2, axis=-1)
    return mk.reshape(B, N, H_M, D_HEAD), mv.reshape(B, N, H_M, D_HEAD)


def _attend(q, k, v, bias):
    s = jnp.einsum('bqhd,bkhd->bhqk', q, k).astype(jnp.float32) * ATTN_SCALE
    if bias is not None:
        s = s + bias
    p = jax.nn.softmax(s, axis=-1).astype(v.dtype)
    return jnp.einsum('bhqk,bkhd->bqhd', p, v)


def _rel_bias(table, q_pos, k_pos):
    d = jnp.clip(q_pos[:, None] - k_pos[None, :], -REL_CLIP, REL_CLIP) + REL_CLIP
    return table[:, d].astype(jnp.float32)


def _band_prompt(q, k, v, table):
    B, S, H, dh = q.shape
    nc = S // CHUNK
    band = A_WINDOW + CHUNK
    pad = ((0, 0), (A_WINDOW, 0), (0, 0), (0, 0))
    k_pad, v_pad = jnp.pad(k, pad), jnp.pad(v, pad)
    bias = _rel_bias(table, A_WINDOW + jnp.arange(CHUNK), jnp.arange(band))[None]
    q_chunks = jnp.moveaxis(q.reshape(B, nc, CHUNK, H, dh), 1, 0)

    def one_chunk(args):
        q_c, c = args
        start = c * CHUNK
        k_c = lax.dynamic_slice_in_dim(k_pad, start, band, axis=1)
        v_c = lax.dynamic_slice_in_dim(v_pad, start, band, axis=1)
        valid = (start + jnp.arange(band)) >= A_WINDOW
        return _attend(q_c, k_c, v_c, jnp.where(valid[None, None, None, :], bias, -jnp.inf))

    o = lax.map(one_chunk, (q_chunks, jnp.arange(nc)))
    return jnp.moveaxis(o, 0, 1).reshape(B, S, H * dh)


def _band_sample(q, k_new, v_new, k_cache, v_cache, table):
    B, T, H, dh = q.shape
    P = k_cache.shape[1]
    k = jnp.concatenate([k_cache, k_new], axis=1)
    v = jnp.concatenate([v_cache, v_new], axis=1)
    bias = _rel_bias(table, P + jnp.arange(T), jnp.arange(P + T))[None]
    return _attend(q, k, v, bias).reshape(B, T, H * dh)


def _indexer_scores(qi, ki, wi):
    dots = jnp.einsum('bqhd,bsd->bqhs', qi, ki).astype(jnp.float32) * (D_IDX ** -0.5)
    w = wi.astype(jnp.float32) * (H_IDX ** -0.5)
    return jnp.einsum('bqh,bqhs->bqs', w, jax.nn.relu(dots))


def _gather_attend(q, k, v, scores, topk):
    top_val, top_idx = lax.top_k(scores, topk)
    gather = jax.vmap(lambda kb, ib: kb[ib])
    k_sel = gather(k, top_idx)
    v_sel = gather(v, top_idx)
    s = jnp.einsum('bqhd,bqkhd->bhqk', q, k_sel).astype(jnp.float32) * ATTN_SCALE
    s = jnp.where(jnp.isfinite(top_val)[:, None], s, -jnp.inf)
    p = jax.nn.softmax(s, axis=-1).astype(v.dtype)
    return jnp.einsum('bhqk,bqkhd->bqhd', p, v_sel)


def _dsa_prompt(q, k, v, qi, ki, wi):
    B, S, H, dh = q.shape
    topk = min(TOPK_MAX, S // 4)
    nb = S // Q_BLOCK
    k_pos = jnp.arange(S)
    to_blocks = lambda t: jnp.moveaxis(t.reshape((B, nb, Q_BLOCK) + t.shape[2:]), 1, 0)

    def one_block(args):
        q_b, qi_b, wi_b, start = args
        q_pos = start + jnp.arange(Q_BLOCK)
        admissible = k_pos[None, :] < (q_pos[:, None] // CHUNK + 1) * CHUNK
        sc = jnp.where(admissible[None], _indexer_scores(qi_b, ki, wi_b), -jnp.inf)
        return _gather_attend(q_b, k, v, sc, topk)

    o = lax.map(one_block, (to_blocks(q), to_blocks(qi), to_blocks(wi), jnp.arange(nb) * Q_BLOCK))
    return jnp.moveaxis(o, 0, 1).reshape(B, S, H * dh)


def _dsa_sample(q, k_new, v_new, qi, ki_new, wi, k_cache, v_cache, ki_cache):
    B, T, H, dh = q.shape
    k = jnp.concatenate([k_cache, k_new], axis=1)
    v = jnp.concatenate([v_cache, v_new], axis=1)
    ki = jnp.concatenate([ki_cache, ki_new], axis=1)
    topk = min(TOPK_MAX, k.shape[1] // 4)
    return _gather_attend(q, k, v, _indexer_scores(qi, ki, wi), topk).reshape(B, T, H * dh)


def _merge(x, oa, ob, om, ga, gb, gm, w_out):
    o = jnp.concatenate([oa * jax.nn.silu(ga), ob * jax.nn.silu(gb), om * jax.nn.silu(gm)], axis=-1)
    return x + o @ w_out


def setup_inputs(seed: int = 0) -> dict:
    key = jax.random.key(seed)
    ks = jax.random.split(key, 18)
    nrm = lambda k, shape, scale=1.0: jax.random.normal(k, shape, jnp.float32) * scale
    a_rows = min(A_WINDOW, PAST_LEN)
    return {
        'x_prompt': nrm(ks[0], (BATCH, SEQ, D_MODEL)),
        'x_sample': nrm(ks[1], (DEC_BATCH, DEC_SEQ, D_MODEL)),
        'mem_prompt': nrm(ks[2], (BATCH, N_MEM, D_MODEL)),
        'cache_a_k': nrm(ks[3], (DEPTH, DEC_BATCH, a_rows, H_A, D_HEAD)),
        'cache_a_v': nrm(ks[4], (DEPTH, DEC_BATCH, a_rows, H_A, D_HEAD)),
        'cache_b_k': nrm(ks[5], (DEPTH, DEC_BATCH, PAST_LEN, H_B, D_HEAD)),
        'cache_b_v': nrm(ks[6], (DEPTH, DEC_BATCH, PAST_LEN, H_B, D_HEAD)),
        'cache_b_kidx': nrm(ks[7], (DEPTH, DEC_BATCH, PAST_LEN, D_IDX)),
        'cache_mem_k': nrm(ks[8], (DEPTH, DEC_BATCH, N_MEM, H_M, D_HEAD)),
        'cache_mem_v': nrm(ks[9], (DEPTH, DEC_BATCH, N_MEM, H_M, D_HEAD)),
        'norm_mix_g': 1.0 + nrm(ks[10], (DEPTH, D_MODEL), 0.02),
        'w_in': nrm(ks[11], (DEPTH, D_MODEL, IN_COLS), D_MODEL ** -0.5),
        'rel_bias_a': nrm(ks[12], (DEPTH, H_A, 2 * REL_CLIP + 1), 0.1),
        'norm_mem_g': 1.0 + nrm(ks[13], (DEPTH, D_MODEL), 0.02),
        'w_mem_kv': nrm(ks[14], (DEPTH, D_MODEL, 2 * D_M), D_MODEL ** -0.5),
        'w_out': nrm(ks[15], (DEPTH, D_MIX, D_MODEL), D_MIX ** -0.5),
        'norm_final_g': 1.0 + nrm(ks[16], (D_MODEL,), 0.02),
    }


def reference(x_prompt, x_sample, mem_prompt, cache_a_k, cache_a_v, cache_b_k, cache_b_v, cache_b_kidx,
              cache_mem_k, cache_mem_v, norm_mix_g, w_in, rel_bias_a, norm_mem_g, w_mem_kv, w_out, norm_final_g):
    S = x_prompt.shape[1]
    T = x_sample.shape[1]
    P = cache_b_k.shape[2]
    pos_p = jnp.arange(S)
    pos_s = P + jnp.arange(T)
    keep = min(A_WINDOW, S)
    xp, xs = x_prompt, x_sample
    akp, avp, bkp, bvp, bip, mkp, mvp = [], [], [], [], [], [], []
    aks, avs, bks, bvs, bis = [], [], [], [], []
    for l in range(DEPTH):
        qa, ka, va, ga, qb, kb, vb, gb, qm, gm, qi, ki, wi = _mixer_inputs(xp, norm_mix_g[l], w_in[l], pos_p)
        mk, mv = _memory_kv(mem_prompt, norm_mem_g[l], w_mem_kv[l])
        oa = _band_prompt(qa, ka, va, rel_bias_a[l])
        ob = _dsa_prompt(qb, kb, vb, qi, ki, wi)
        om = _attend(qm, mk, mv, None).reshape(xp.shape[0], S, D_M)
        xp = _merge(xp, oa, ob, om, ga, gb, gm, w_out[l])
        akp.append(ka[:, S - keep:])
        avp.append(va[:, S - keep:])
        bkp.append(kb)
        bvp.append(vb)
        bip.append(ki)
        mkp.append(mk)
        mvp.append(mv)
        qa, ka, va, ga, qb, kb, vb, gb, qm, gm, qi, ki, wi = _mixer_inputs(xs, norm_mix_g[l], w_in[l], pos_s)
        oa = _band_sample(qa, ka, va, cache_a_k[l], cache_a_v[l], rel_bias_a[l])
        ob = _dsa_sample(qb, kb, vb, qi, ki, wi, cache_b_k[l], cache_b_v[l], cache_b_kidx[l])
        om = _attend(qm, cache_mem_k[l], cache_mem_v[l], None).reshape(xs.shape[0], T, D_M)
        xs = _merge(xs, oa, ob, om, ga, gb, gm, w_out[l])
        aks.append(ka)
        avs.append(va)
        bks.append(kb)
        bvs.append(vb)
        bis.append(ki)
    y_prompt = _rmsnorm(xp, norm_final_g)
    y_sample = _rmsnorm(xs, norm_final_g)
    st = lambda t: jnp.stack(t, axis=0)
    return (y_prompt, y_sample, st(akp), st(avp), st(bkp), st(bvp), st(bip), st(mkp), st(mvp),
            st(aks), st(avs), st(bks), st(bvs), st(bis))
```

```python
import functools

import jax
import jax.numpy as jnp
from jax import lax
from jax.experimental import pallas as pl
from jax.experimental.pallas import tpu as pltpu

CHUNK = 64
D_HEAD = 64
H_A = 6
H_B = 6
H_M = 4
D_A = H_A * D_HEAD
D_B = H_B * D_HEAD
D_M = H_M * D_HEAD
A_LEFT_CHUNKS = 8
A_WINDOW = A_LEFT_CHUNKS * CHUNK
REL_CLIP = 256
H_IDX = 8
D_IDX = 32
TOPK_MAX = 256
ROPE_THETA = 10000.0
EPS = 1e-6

LANES = 128
VMEM_LIMIT_BYTES = 56 * 1024 * 1024

C_QA, C_KA, C_VA, C_GA = 0, D_A, 2 * D_A, 3 * D_A
C_QB = 4 * D_A
C_KB, C_VB, C_GB = C_QB + D_B, C_QB + 2 * D_B, C_QB + 3 * D_B
C_QM = C_QB + 4 * D_B
C_GM = C_QM + D_M
C_QI = C_GM + D_M
C_KIWI = C_QI + H_IDX * D_IDX
C_KI4 = C_KIWI + LANES
W_COLS = C_KI4 + LANES

PROJ_TM = 512
BAND_TQ = 128
BAND_WIN = A_WINDOW + BAND_TQ
DSA_TQ = 256
DSA_BUCKETS = 4

_NT = (((1,), (1,)), ((), ()))
_INT_MIN = -2 ** 31
_NEG_INF = float("-inf")


def _cparams(n_axes):
    return pltpu.CompilerParams(
        dimension_semantics=("arbitrary",) * n_axes,
        vmem_limit_bytes=VMEM_LIMIT_BYTES)


def _silu(g):
    return g * (1.0 / (1.0 + jnp.exp(-g)))


def _rms_scale(x, g):
    ms = jnp.mean(x * x, axis=-1, keepdims=True)
    return (x * lax.rsqrt(ms + EPS)) * g


def _rope_slab(z, cos, sin, half):
    lane = lax.broadcasted_iota(jnp.int32, z.shape, 1)
    first = (lane & (2 * half - 1)) < half
    partner = jnp.where(first, pltpu.roll(z, LANES - half, 1), pltpu.roll(z, half, 1))
    return z * cos + partner * sin


def _proj_kernel(x_ref, g_ref, w_ref, rope_ref,
                 qa_ref, ka_ref, va_ref, ga_ref, qb_ref, kb_ref, vb_ref, gb_ref,
                 qm_ref, gm_ref, qi_ref, kiwi_ref, ki4_ref, ki_ref, ak_ref, av_ref,
                 *, n_tiles, keep_tiles):
    i = pl.program_id(1)
    xn = _rms_scale(x_ref[0], g_ref[...]).astype(jnp.bfloat16)

    def proj(c0, n):
        return jnp.dot(xn, w_ref[:, c0:c0 + n], preferred_element_type=jnp.float32)

    scale = D_HEAD ** -0.5
    z = proj(C_QA, 2 * D_A)
    qa_ref[0] = (z[:, :D_A] * scale).astype(jnp.bfloat16)
    ka = z[:, D_A:]
    ka_ref[0] = ka.astype(jnp.bfloat16)
    z = proj(C_VA, 2 * D_A)
    va = z[:, :D_A]
    va_ref[0] = va.astype(jnp.bfloat16)
    ga_ref[0] = z[:, D_A:]

    @pl.when(i >= n_tiles - keep_tiles)
    def _():
        ak_ref[0] = ka
        av_ref[0] = va

    cos64, sin64 = rope_ref[0], rope_ref[1]
    cos32, sin32 = rope_ref[2], rope_ref[3]
    cos_kw, sin_kw = rope_ref[4], rope_ref[5]
    z = proj(C_QB, 2 * D_B)
    for s in range(D_B // LANES):
        sl = slice(s * LANES, (s + 1) * LANES)
        qb = _rope_slab(z[:, sl], cos64, sin64, D_HEAD // 2)
        qb_ref[0, :, sl] = (qb * scale).astype(jnp.bfloat16)
        kb_ref[0, :, sl] = _rope_slab(z[:, D_B + s * LANES:D_B + (s + 1) * LANES],
                                      cos64, sin64, D_HEAD // 2)
    z = proj(C_VB, 2 * D_B)
    vb_ref[0] = z[:, :D_B]
    gb_ref[0] = z[:, D_B:]
    qm_ref[0] = (proj(C_QM, D_M) * scale).astype(jnp.bfloat16)
    gm_ref[0] = proj(C_GM, D_M)
    z = proj(C_QI, H_IDX * D_IDX)
    for s in range(H_IDX * D_IDX // LANES):
        sl = slice(s * LANES, (s + 1) * LANES)
        qi_ref[0, :, sl] = _rope_slab(z[:, sl], cos32, sin32, D_IDX // 2).astype(jnp.bfloat16)
    z = proj(C_KIWI, 2 * LANES)
    kiwi = _rope_slab(z[:, :LANES], cos_kw, sin_kw, D_IDX // 2)
    kiwi_ref[0] = kiwi
    ki_ref[0] = kiwi[:, :D_IDX]
    ki4_ref[0] = _rope_slab(z[:, LANES:], cos32, sin32, D_IDX // 2).astype(jnp.bfloat16)


def _proj_call(x, g, w, rope, keep_rows):
    B, S, D = x.shape
    tm = min(PROJ_TM, S)
    n_tiles = S // tm
    keep_tiles = keep_rows // tm
    assert n_tiles * tm == S and keep_tiles * tm == keep_rows

    def tile(n, dtype):
        return (jax.ShapeDtypeStruct((B, S, n), dtype),
                pl.BlockSpec((1, tm, n), lambda b, i: (b, i, 0)))

    f32, bf16 = jnp.float32, jnp.bfloat16
    outs = [tile(D_A, bf16), tile(D_A, bf16), tile(D_A, bf16), tile(D_A, f32),
            tile(D_B, bf16), tile(D_B, f32), tile(D_B, f32), tile(D_B, f32),
            tile(D_M, bf16), tile(D_M, f32), tile(H_IDX * D_IDX, bf16),
            tile(LANES, f32), tile(LANES, bf16), tile(D_IDX, f32)]
    keep_spec = pl.BlockSpec(
        (1, tm, D_A), lambda b, i: (b, jnp.maximum(i - (n_tiles - keep_tiles), 0), 0))
    outs += [(jax.ShapeDtypeStruct((B, keep_rows, D_A), f32), keep_spec)] * 2
    return pl.pallas_call(
        functools.partial(_proj_kernel, n_tiles=n_tiles, keep_tiles=keep_tiles),
        grid=(B, n_tiles),
        in_specs=[pl.BlockSpec((1, tm, D), lambda b, i: (b, i, 0)),
                  pl.BlockSpec((1, D), lambda b, i: (0, 0)),
                  pl.BlockSpec((D, W_COLS), lambda b, i: (0, 0)),
                  pl.BlockSpec((6, tm, LANES), lambda b, i: (0, i, 0))],
        out_specs=[o[1] for o in outs],
        out_shape=[o[0] for o in outs],
        compiler_params=_cparams(2),
        name="proj",
    )(x, g, w, rope)


def _memkv_kernel(m_ref, g_ref, w_ref, mk_ref, mv_ref):
    xn = _rms_scale(m_ref[...], g_ref[...]).astype(jnp.bfloat16)
    mk_ref[...] = jnp.dot(xn, w_ref[:, :D_M], preferred_element_type=jnp.float32)
    mv_ref[...] = jnp.dot(xn, w_ref[:, D_M:], preferred_element_type=jnp.float32)


def _memkv_call(mem2d, g, w):
    R, D = mem2d.shape
    tm = min(PROJ_TM, R)
    assert R % tm == 0
    row = lambda n: pl.BlockSpec((tm, n), lambda i: (i, 0))
    return pl.pallas_call(
        _memkv_kernel,
        grid=(R // tm,),
        in_specs=[row(D), pl.BlockSpec((1, D), lambda i: (0, 0)),
                  pl.BlockSpec((D, 2 * D_M), lambda i: (0, 0))],
        out_specs=[row(D_M), row(D_M)],
        out_shape=[jax.ShapeDtypeStruct((R, D_M), jnp.float32)] * 2,
        compiler_params=_cparams(1),
        name="memkv",
    )(mem2d, g, w)


def _softmax_rows_pv(s, v):
    m = jnp.max(s, axis=1, keepdims=True)
    p = jnp.exp(s - m)
    l = jnp.sum(p, axis=1, keepdims=True)
    o = jnp.dot(p.astype(jnp.bfloat16), v, preferred_element_type=jnp.float32)
    return o * (1.0 / l)


def _keep_lanes(slab, lo, hi):
    x = slab.astype(jnp.float32)
    lane = lax.broadcasted_iota(jnp.int32, x.shape, 1)
    return jnp.where(lane >= lo, jnp.where(lane < hi, x, 0.0), 0.0).astype(jnp.bfloat16)


def _pair_attend(q_slab, k_slab, v_slab, bias_fn):
    lane = lax.broadcasted_iota(jnp.int32, q_slab.shape, 1)
    out = None
    for hh in range(2):
        s = lax.dot_general(_keep_lanes(q_slab, hh * D_HEAD, (hh + 1) * D_HEAD), k_slab, _NT,
                            preferred_element_type=jnp.float32)
        o = _softmax_rows_pv(bias_fn(s, hh), v_slab)
        out = o if hh == 0 else jnp.where(lane < D_HEAD, out, o)
    return out


def _mem_attend(qm_ref, gm_ref, mk_ref, mv_ref, om_ref):
    for p in range(D_M // LANES):
        sl = slice(p * LANES, (p + 1) * LANES)
        o = _pair_attend(qm_ref[0, :, sl], mk_ref[0, :, sl].astype(jnp.bfloat16),
                         mv_ref[0, :, sl].astype(jnp.bfloat16), lambda s, hh: s)
        om_ref[0, :, sl] = (o * _silu(gm_ref[0, :, sl])).astype(jnp.bfloat16)


def _band_prompt_kernel(qa_ref, ka_ref, va_ref, ga_ref, qm_ref, gm_ref, mk_ref, mv_ref, bias_ref,
                        oa_ref, om_ref, kpad, vpad):
    j = pl.program_id(1)

    @pl.when(j == 0)
    def _():
        zeros = jnp.zeros((A_WINDOW, D_A), jnp.bfloat16)
        kpad[:A_WINDOW, :] = zeros
        vpad[:A_WINDOW, :] = zeros
        kpad[A_WINDOW:, :] = ka_ref[0]
        vpad[A_WINDOW:, :] = va_ref[0]

    start = pl.multiple_of(j * BAND_TQ, BAND_TQ)

    def run(mask_front):
        col = lax.broadcasted_iota(jnp.int32, (BAND_TQ, BAND_WIN), 1) + j * BAND_TQ
        for p in range(D_A // LANES):
            sl = slice(p * LANES, (p + 1) * LANES)

            def bias_fn(s, hh, p=p):
                s = s + bias_ref[2 * p + hh]
                if mask_front:
                    s = jnp.where(col >= A_WINDOW, s, _NEG_INF)
                return s

            o = _pair_attend(qa_ref[0, :, sl], kpad[pl.ds(start, BAND_WIN), sl],
                             vpad[pl.ds(start, BAND_WIN), sl], bias_fn)
            oa_ref[0, :, sl] = (o * _silu(ga_ref[0, :, sl])).astype(jnp.bfloat16)

    front_steps = A_WINDOW // BAND_TQ
    pl.when(j < front_steps)(lambda: run(True))
    pl.when(j >= front_steps)(lambda: run(False))
    _mem_attend(qm_ref, gm_ref, mk_ref, mv_ref, om_ref)


def _band_prompt_call(qa, ka, va, ga, qm, gm, mk, mv, bias):
    B, S, _ = qa.shape
    nq = S // BAND_TQ
    n_mem = mk.shape[1]
    qblk = lambda n: pl.BlockSpec((1, BAND_TQ, n), lambda b, j: (b, j, 0))
    full = lambda r, n: pl.BlockSpec((1, r, n), lambda b, j: (b, 0, 0))
    return pl.pallas_call(
        _band_prompt_kernel,
        grid=(B, nq),
        in_specs=[qblk(D_A), full(S, D_A), full(S, D_A), qblk(D_A), qblk(D_M), qblk(D_M),
                  full(n_mem, D_M), full(n_mem, D_M),
                  pl.BlockSpec((H_A, BAND_TQ, BAND_WIN), lambda b, j: (0, 0, 0))],
        out_specs=[qblk(D_A), qblk(D_M)],
        out_shape=[jax.ShapeDtypeStruct((B, S, D_A), jnp.bfloat16),
                   jax.ShapeDtypeStruct((B, S, D_M), jnp.bfloat16)],
        scratch_shapes=[pltpu.VMEM((S + A_WINDOW, D_A), jnp.bfloat16)] * 2,
        compiler_params=_cparams(2),
        name="band_prompt",
    )(qa, ka, va, ga, qm, gm, mk, mv, bias)


def _band_sample_kernel(qa_ref, kc_ref, vc_ref, kn_ref, vn_ref, ga_ref, qm_ref, gm_ref,
                        mk_ref, mv_ref, bias_ref, oa_ref, om_ref, kcat, vcat):
    P = kc_ref.shape[1]
    T = kn_ref.shape[1]
    pad = kcat.shape[0] - P - T
    kcat[:P, :] = kc_ref[0].astype(jnp.bfloat16)
    vcat[:P, :] = vc_ref[0].astype(jnp.bfloat16)
    kcat[P:P + T, :] = kn_ref[0]
    vcat[P:P + T, :] = vn_ref[0]
    zeros = jnp.zeros((pad, D_A), jnp.bfloat16)
    kcat[P + T:, :] = zeros
    vcat[P + T:, :] = zeros
    for p in range(D_A // LANES):
        sl = slice(p * LANES, (p + 1) * LANES)
        o = _pair_attend(qa_ref[0, :, sl], kcat[:, sl], vcat[:, sl],
                         lambda s, hh, p=p: s + bias_ref[2 * p + hh])
        oa_ref[0, :, sl] = (o * _silu(ga_ref[0, :, sl])).astype(jnp.bfloat16)
    _mem_attend(qm_ref, gm_ref, mk_ref, mv_ref, om_ref)


def _band_sample_call(qa, kc, vc, kn, vn, ga, qm, gm, mk, mv, bias):
    B, T, _ = qa.shape
    P = kc.shape[1]
    n_keys = bias.shape[2]
    n_mem = mk.shape[1]
    blk = lambda r, n: pl.BlockSpec((1, r, n), lambda b: (b, 0, 0))
    return pl.pallas_call(
        _band_sample_kernel,
        grid=(B,),
        in_specs=[blk(T, D_A), blk(P, D_A), blk(P, D_A), blk(T, D_A), blk(T, D_A), blk(T, D_A),
                  blk(T, D_M), blk(T, D_M), blk(n_mem, D_M), blk(n_mem, D_M),
                  pl.BlockSpec((H_A, T, n_keys), lambda b: (0, 0, 0))],
        out_specs=[blk(T, D_A), blk(T, D_M)],
        out_shape=[jax.ShapeDtypeStruct((B, T, D_A), jnp.bfloat16),
                   jax.ShapeDtypeStruct((B, T, D_M), jnp.bfloat16)],
        scratch_shapes=[pltpu.VMEM((n_keys, D_A), jnp.bfloat16)] * 2,
        compiler_params=_cparams(1),
        name="band_sample",
    )(qa, kc, vc, kn, vn, ga, qm, gm, mk, mv, bias)


def _key_to_f32(k):
    bits = k ^ ((k >> 31) & jnp.int32(0x7FFFFFFF))
    return lax.bitcast_convert_type(bits, jnp.float32)


def _count(pred_f32):
    return jnp.sum(pred_f32, axis=0, keepdims=True)


def _dsa_select(sc_ref, bias_ref, tk, tq, topk):
    one, zero = jnp.float32(1.0), jnp.float32(0.0)

    def thr_step(i, t):
        cand = t + lax.shift_left(jnp.int32(1), 31 - i)
        c = _count(jnp.where(sc_ref[:tk, :] >= _key_to_f32(cand), one, zero))
        return jnp.where(c >= topk, cand, t)

    t_key = lax.fori_loop(0, 32, thr_step, jnp.full((1, tq), _INT_MIN, jnp.int32))
    thr = jnp.where(t_key == _INT_MIN, _NEG_INF, _key_to_f32(t_key))

    sc = sc_ref[:tk, :]
    c_gt = _count(jnp.where(sc > thr, one, zero))
    c_ge = _count(jnp.where(sc >= thr, one, zero))
    need = topk - c_gt
    kidx = lax.broadcasted_iota(jnp.int32, (tk, tq), 0)

    def tie_search():
        nbits = tk.bit_length()

        def tie_step(i, jmax):
            cand = jmax + lax.shift_left(jnp.int32(1), nbits - 1 - i)
            ki = lax.broadcasted_iota(jnp.int32, (tk, tq), 0)
            f = _count(jnp.where(sc_ref[:tk, :] == thr, jnp.where(ki < cand, one, zero), zero))
            return jnp.where(f < need, cand, jmax)

        return lax.fori_loop(0, nbits, tie_step, jnp.zeros((1, tq), jnp.int32))

    has_excess = jnp.max(jnp.where(c_ge > topk, one, zero)) > 0
    jmax = lax.cond(has_excess, tie_search, lambda: jnp.full((1, tq), tk, jnp.int32))
    sel = jnp.where(sc > thr, one,
                    jnp.where(sc == thr, jnp.where(kidx <= jmax, one, zero), zero))
    finite = jnp.abs(sc) < jnp.float32(jnp.inf)
    bias_ref[:tk, :] = jnp.where(sel > 0, jnp.where(finite, zero, _NEG_INF), _NEG_INF)


def _dsa_body(tk, topk, limit, q_ref, qi_ref, kiwi_ref, g_ref, o_ref,
              kbf, vt, ki4s, sc_ref, bias_ref):
    tq = q_ref.shape[1]

    kiwi_t = kiwi_ref[0].T
    ki = ki4s[:tk, :]
    acc = jnp.zeros((tk, tq), jnp.float32)
    heads_per_slab = LANES // D_IDX
    for h in range(H_IDX):
        slab = qi_ref[0, :, (h // heads_per_slab) * LANES:(h // heads_per_slab + 1) * LANES]
        lo = (h % heads_per_slab) * D_IDX
        d = lax.dot_general(ki, _keep_lanes(slab, lo, lo + D_IDX), _NT,
                            preferred_element_type=jnp.float32)
        w = kiwi_t[D_IDX + h:D_IDX + h + 1, :] * ((D_IDX ** -0.5) * (H_IDX ** -0.5))
        acc = acc + w * jnp.maximum(d, 0.0)
    if limit is not None:
        kidx = lax.broadcasted_iota(jnp.int32, (tk, tq), 0)
        acc = jnp.where(kidx < limit, acc, _NEG_INF)
    sc_ref[:tk, :] = acc

    _dsa_select(sc_ref, bias_ref, tk, tq, topk)

    row = lax.broadcasted_iota(jnp.int32, (LANES, tq), 0)
    for p in range(D_B // LANES):
        sl = slice(p * LANES, (p + 1) * LANES)
        q_slab = q_ref[0, :, sl]
        k_slab = kbf[:tk, sl]
        v_slab = vt[sl, :tk]
        ot = None
        for hh in range(2):
            s = lax.dot_general(k_slab, _keep_lanes(q_slab, hh * D_HEAD, (hh + 1) * D_HEAD), _NT,
                                preferred_element_type=jnp.float32)
            s = s + bias_ref[:tk, :]
            m = jnp.max(s, axis=0, keepdims=True)
            pr = jnp.exp(s - m)
            l = jnp.sum(pr, axis=0, keepdims=True)
            o = jnp.dot(v_slab, pr.astype(jnp.bfloat16), preferred_element_type=jnp.float32)
            o = o * (1.0 / l)
            ot = o if hh == 0 else jnp.where(row < D_HEAD, ot, o)
        o_ref[0, :, sl] = (ot.T * _silu(g_ref[0, :, sl])).astype(jnp.bfloat16)


def _dsa_prompt_kernel(q_ref, qi_ref, kiwi_ref, g_ref, k_ref, v_ref, ki4_ref, o_ref,
                       kbf, vt, ki4s, sc_ref, bias_ref, *, topk):
    j = pl.program_id(1)
    S = k_ref.shape[1]
    tq = q_ref.shape[1]

    @pl.when(j == 0)
    def _():
        kbf[...] = k_ref[0].astype(jnp.bfloat16)
        vt[...] = v_ref[0].T.astype(jnp.bfloat16)
        ki4s[...] = ki4_ref[0]

    qpos = j * tq + lax.broadcasted_iota(jnp.int32, (1, tq), 1)
    limit = ((qpos >> 6) + 1) << 6
    nq = S // tq
    per_bucket = nq // DSA_BUCKETS
    for bi in range(DSA_BUCKETS):
        tk = (bi + 1) * per_bucket * tq

        @pl.when(j // per_bucket == bi)
        def _(tk=tk):
            _dsa_body(tk, topk, limit, q_ref, qi_ref, kiwi_ref, g_ref, o_ref,
                      kbf, vt, ki4s, sc_ref, bias_ref)


def _dsa_prompt_call(qb, qi, kiwi, gb, kb, vb, ki4):
    B, S, _ = qb.shape
    tq = DSA_TQ
    nq = S // tq
    assert CHUNK == 64 and nq % DSA_BUCKETS == 0
    topk = min(TOPK_MAX, S // 4)
    qblk = lambda n: pl.BlockSpec((1, tq, n), lambda b, j: (b, j, 0))
    full = lambda n: pl.BlockSpec((1, S, n), lambda b, j: (b, 0, 0))
    return pl.pallas_call(
        functools.partial(_dsa_prompt_kernel, topk=topk),
        grid=(B, nq),
        in_specs=[qblk(D_B), qblk(H_IDX * D_IDX), qblk(LANES), qblk(D_B),
                  full(D_B), full(D_B), full(LANES)],
        out_specs=qblk(D_B),
        out_shape=jax.ShapeDtypeStruct((B, S, D_B), jnp.bfloat16),
        scratch_shapes=[pltpu.VMEM((S, D_B), jnp.bfloat16),
                        pltpu.VMEM((D_B, S), jnp.bfloat16),
                        pltpu.VMEM((S, LANES), jnp.bfloat16),
                        pltpu.VMEM((S, tq), jnp.float32),
                        pltpu.VMEM((S, tq), jnp.float32)],
        compiler_params=_cparams(2),
        name="dsa_prompt",
    )(qb, qi, kiwi, gb, kb, vb, ki4)


def _dsa_sample_kernel(q_ref, qi_ref, kiwi_ref, g_ref, kc_ref, vc_ref, kic_ref,
                       kn_ref, vn_ref, kin_ref, o_ref, kbf, vt, ki4s, sc_ref, bias_ref, *, topk):
    P = kc_ref.shape[1]
    T = kn_ref.shape[1]
    kbf[:P, :] = kc_ref[0].astype(jnp.bfloat16)
    kbf[P:, :] = kn_ref[0].astype(jnp.bfloat16)
    vt[:, :P] = vc_ref[0].T.astype(jnp.bfloat16)
    vn_tile = jnp.concatenate([vn_ref[0], jnp.zeros((LANES - T, D_B), jnp.float32)], axis=0)
    vt[:, P:] = vn_tile.T[:, :T].astype(jnp.bfloat16)
    ki4s[:P, :] = kic_ref[0]
    ki4s[P:, :] = kin_ref[0]
    _dsa_body(P + T, topk, None, q_ref, qi_ref, kiwi_ref, g_ref, o_ref,
              kbf, vt, ki4s, sc_ref, bias_ref)


def _dsa_sample_call(qb, qi, kiwi, gb, kc, vc, kic, kn, vn, kin):
    B, tq, _ = qb.shape
    P, T = kc.shape[1], kn.shape[1]
    tk = P + T
    topk = min(TOPK_MAX, tk // 4)
    blk = lambda r, n: pl.BlockSpec((1, r, n), lambda b: (b, 0, 0))
    return pl.pallas_call(
        functools.partial(_dsa_sample_kernel, topk=topk),
        grid=(B,),
        in_specs=[blk(tq, D_B), blk(tq, H_IDX * D_IDX), blk(tq, LANES), blk(tq, D_B),
                  blk(P, D_B), blk(P, D_B), blk(P, LANES),
                  blk(T, D_B), blk(T, D_B), blk(T, LANES)],
        out_specs=blk(tq, D_B),
        out_shape=jax.ShapeDtypeStruct((B, tq, D_B), jnp.bfloat16),
        scratch_shapes=[pltpu.VMEM((tk, D_B), jnp.bfloat16),
                        pltpu.VMEM((D_B, tk), jnp.bfloat16),
                        pltpu.VMEM((tk, LANES), jnp.bfloat16),
                        pltpu.VMEM((tk, tq), jnp.float32),
                        pltpu.VMEM((tk, tq), jnp.float32)],
        compiler_params=_cparams(1),
        name="dsa_sample",
    )(qb, qi, kiwi, gb, kc, vc, kic, kn, vn, kin)


def _merge_kernel(x_ref, oa_ref, ob_ref, om_ref, w_ref, g_ref, y_ref, *, last):
    acc = x_ref[...]
    acc = acc + jnp.dot(oa_ref[...], w_ref[:D_A, :], preferred_element_type=jnp.float32)
    acc = acc + jnp.dot(ob_ref[...], w_ref[D_A:D_A + D_B, :], preferred_element_type=jnp.float32)
    acc = acc + jnp.dot(om_ref[...], w_ref[D_A + D_B:, :], preferred_element_type=jnp.float32)
    y_ref[...] = _rms_scale(acc, g_ref[...]) if last else acc


def _merge_call(x2d, oa, ob, om, w, g, last):
    R, D = x2d.shape
    tm = min(PROJ_TM, R)
    assert R % tm == 0
    row = lambda n: pl.BlockSpec((tm, n), lambda i: (i, 0))
    return pl.pallas_call(
        functools.partial(_merge_kernel, last=last),
        grid=(R // tm,),
        in_specs=[row(D), row(D_A), row(D_B), row(D_M),
                  pl.BlockSpec((D_A + D_B + D_M, D), lambda i: (0, 0)),
                  pl.BlockSpec((1, D), lambda i: (0, 0))],
        out_specs=row(D),
        out_shape=jax.ShapeDtypeStruct((R, D), jnp.float32),
        compiler_params=_cparams(1),
        name="merge",
    )(x2d, oa, ob, om, w, g)


def _pack_w_in(w):
    D = w.shape[0]
    ki = w[:, C_KIWI:C_KIWI + D_IDX]
    wi = w[:, C_KIWI + D_IDX:C_KIWI + D_IDX + H_IDX]
    pad = jnp.zeros((D, LANES - D_IDX - H_IDX), w.dtype)
    packed = jnp.concatenate([w[:, :C_KIWI], ki, wi, pad, jnp.tile(ki, (1, LANES // D_IDX))], axis=1)
    return packed.astype(jnp.bfloat16)


def _rope_tables(pos):
    posf = pos.astype(jnp.float32)

    def tables(d):
        half = d // 2
        inv_freq = ROPE_THETA ** (-jnp.arange(half, dtype=jnp.float32) * 2.0 / d)
        ang = posf[:, None] * inv_freq[None, :]
        cos, sin = jnp.cos(ang), jnp.sin(ang)
        return jnp.concatenate([cos, cos], axis=1), jnp.concatenate([-sin, sin], axis=1)

    cos64, sin64 = tables(D_HEAD)
    cos32, sin32 = tables(D_IDX)
    n = pos.shape[0]
    rest = LANES - D_IDX
    cos_kw = jnp.concatenate([cos32, jnp.ones((n, rest), jnp.float32)], axis=1)
    sin_kw = jnp.concatenate([sin32, jnp.zeros((n, rest), jnp.float32)], axis=1)
    rep = lambda t, d: jnp.tile(t, (1, LANES // d))
    return jnp.stack([rep(cos64, D_HEAD), rep(sin64, D_HEAD), rep(cos32, D_IDX), rep(sin32, D_IDX),
                      cos_kw, sin_kw])


def _band_bias(table, n_q, n_keys, n_valid, chunked):
    i = jnp.arange(n_q)[:, None]
    r = jnp.arange(n_keys)[None, :]
    d = jnp.clip(A_WINDOW + i - r, -REL_CLIP, REL_CLIP) + REL_CLIP
    bias = table[:, d].astype(jnp.float32)
    if chunked:
        lo = (i // CHUNK) * CHUNK
        ok = (r >= lo) & (r < lo + A_WINDOW + CHUNK)
    else:
        ok = r < n_valid
    return jnp.where(ok[None], bias, _NEG_INF)


def kernel(x_prompt, x_sample, mem_prompt, cache_a_k, cache_a_v, cache_b_k, cache_b_v, cache_b_kidx,
           cache_mem_k, cache_mem_v, norm_mix_g, w_in, rel_bias_a, norm_mem_g, w_mem_kv, w_out,
           norm_final_g):
    B, S, D = x_prompt.shape
    Bs, T, _ = x_sample.shape
    depth = w_in.shape[0]
    P = cache_b_k.shape[2]
    Pa = cache_a_k.shape[2]
    n_mem = mem_prompt.shape[1]
    keep = min(A_WINDOW, S)
    assert Pa == A_WINDOW and T <= CHUNK and LANES % T == 0

    rope_p = _rope_tables(jnp.arange(S))
    rope_s = jnp.tile(_rope_tables(P + jnp.arange(T)), (1, Bs, 1))
    g_final = norm_final_g.reshape(1, D)

    xp, xs = x_prompt, x_sample
    outs_p = [[] for _ in range(7)]
    outs_s = [[] for _ in range(5)]
    for l in range(depth):
        last = l == depth - 1
        w = _pack_w_in(w_in[l])
        g_mix = norm_mix_g[l].reshape(1, D)
        w_o = w_out[l].astype(jnp.bfloat16)
        bias_p = _band_bias(rel_bias_a[l], BAND_TQ, BAND_WIN, BAND_WIN, True)
        bias_s = _band_bias(rel_bias_a[l], T, BAND_WIN, Pa + T, False)

        (qa, ka, va, ga, qb, kb, vb, gb, qm, gm, qi, kiwi, ki4, ki, ak, av) = _proj_call(
            xp, g_mix, w, rope_p, keep)
        mk, mv = _memkv_call(mem_prompt.reshape(B * n_mem, D), norm_mem_g[l].reshape(1, D),
                             w_mem_kv[l].astype(jnp.bfloat16))
        mk = mk.reshape(B, n_mem, D_M)
        mv = mv.reshape(B, n_mem, D_M)
        oa, om = _band_prompt_call(qa, ka, va, ga, qm, gm, mk, mv, bias_p)
        ob = _dsa_prompt_call(qb, qi, kiwi, gb, kb, vb, ki4)
        xp = _merge_call(xp.reshape(B * S, D), oa.reshape(B * S, D_A), ob.reshape(B * S, D_B),
                         om.reshape(B * S, D_M), w_o, g_final, last).reshape(B, S, D)
        for lst, t in zip(outs_p, (ak.reshape(B, keep, H_A, D_HEAD), av.reshape(B, keep, H_A, D_HEAD),
                                   kb.reshape(B, S, H_B, D_HEAD), vb.reshape(B, S, H_B, D_HEAD), ki,
                                   mk.reshape(B, n_mem, H_M, D_HEAD), mv.reshape(B, n_mem, H_M, D_HEAD))):
            lst.append(t)

        (qa, ka, va, ga, qb, kb, vb, gb, qm, gm, qi, kiwi, ki4, ki, ak, av) = _proj_call(
            xs.reshape(1, Bs * T, D), g_mix, w, rope_s, Bs * T)
        per_b = lambda t: t.reshape(Bs, T, t.shape[-1])
        oa, om = _band_sample_call(
            per_b(qa), cache_a_k[l].reshape(Bs, Pa, D_A), cache_a_v[l].reshape(Bs, Pa, D_A),
            per_b(ka), per_b(va), per_b(ga), per_b(qm), per_b(gm),
            cache_mem_k[l].reshape(Bs, n_mem, D_M), cache_mem_v[l].reshape(Bs, n_mem, D_M), bias_s)
        rep = lambda t: jnp.tile(per_b(t), (1, LANES // T, 1))
        kic = jnp.tile(cache_b_kidx[l], (1, 1, LANES // D_IDX)).astype(jnp.bfloat16)
        ob = _dsa_sample_call(rep(qb), rep(qi), rep(kiwi), rep(gb),
                              cache_b_k[l].reshape(Bs, P, D_B), cache_b_v[l].reshape(Bs, P, D_B), kic,
                              per_b(kb), per_b(vb), per_b(ki4))[:, :T]
        xs = _merge_call(xs.reshape(Bs * T, D), oa.reshape(Bs * T, D_A), ob.reshape(Bs * T, D_B),
                         om.reshape(Bs * T, D_M), w_o, g_final, last).reshape(Bs, T, D)
        for lst, t in zip(outs_s, (ak.reshape(Bs, T, H_A, D_HEAD), av.reshape(Bs, T, H_A, D_HEAD),
                                   kb.reshape(Bs, T, H_B, D_HEAD), vb.reshape(Bs, T, H_B, D_HEAD),
                                   ki.reshape(Bs, T, D_IDX))):
            lst.append(t)

    st = lambda ts: jnp.stack(ts, axis=0)
    return (xp, xs) + tuple(st(t) for t in outs_p) + tuple(st(t) for t in outs_s)
```

```python
import functools
import math

import jax
import jax.numpy as jnp
from jax import lax
from jax.experimental import pallas as pl
from jax.experimental.pallas import tpu as pltpu

CHUNK = 64
D_HEAD = 64
H_A = 6
H_B = 6
H_M = 4
D_A = H_A * D_HEAD
D_B = H_B * D_HEAD
D_M = H_M * D_HEAD
A_LEFT_CHUNKS = 8
A_WINDOW = A_LEFT_CHUNKS * CHUNK
REL_CLIP = 256
H_IDX = 8
D_IDX = 32
TOPK_MAX = 256
ROPE_THETA = 10000.0
EPS = 1e-6

LANES = 128
VMEM_LIMIT_BYTES = 56 * 1024 * 1024

C_QA, C_KA, C_VA, C_GA = 0, D_A, 2 * D_A, 3 * D_A
C_QB = 4 * D_A
C_KB, C_VB, C_GB = C_QB + D_B, C_QB + 2 * D_B, C_QB + 3 * D_B
C_QM = C_QB + 4 * D_B
C_GM = C_QM + D_M
C_QI = C_GM + D_M
C_KIWI = C_QI + H_IDX * D_IDX
C_KI4 = C_KIWI + LANES
W_COLS = C_KI4 + LANES

PROJ_TM = 512
BAND_TQ = 128
BAND_WIN = A_WINDOW + BAND_TQ
DSA_TQ = 256

LOG2E = math.log2(math.e)
QK_SCALE = (D_HEAD ** -0.5) * LOG2E
BIAS_VEC = 6 * LANES

_NT = (((1,), (1,)), ((), ()))
_INT_MIN = -2 ** 31
_NEG_INF = float("-inf")


def _cparams(n_axes):
    return pltpu.CompilerParams(
        dimension_semantics=("arbitrary",) * n_axes,
        vmem_limit_bytes=VMEM_LIMIT_BYTES)


def _silu(g):
    return g * (1.0 / (1.0 + jnp.exp(-g)))


def _rms_scale(x, g):
    ms = jnp.mean(x * x, axis=-1, keepdims=True)
    return (x * lax.rsqrt(ms + EPS)) * g


def _rope_slab(z, cos, sin, half):
    lane = lax.broadcasted_iota(jnp.int32, z.shape, 1)
    first = (lane & (2 * half - 1)) < half
    partner = jnp.where(first, pltpu.roll(z, LANES - half, 1), pltpu.roll(z, half, 1))
    return z * cos + partner * sin


def _proj_kernel(x_ref, g_ref, w_ref, rope_ref,
                 qa_ref, ka_ref, va_ref, ga_ref, qb_ref, kb_ref, vb_ref, gb_ref,
                 qm_ref, gm_ref, qi_ref, kiwi_ref, ki4_ref, ki_ref, kbb_ref, vbb_ref, ak_ref, av_ref,
                 *, n_tiles, keep_tiles):
    i = pl.program_id(1)
    xn = _rms_scale(x_ref[0], g_ref[...]).astype(jnp.bfloat16)

    def proj(c0, n):
        return jnp.dot(xn, w_ref[:, c0:c0 + n], preferred_element_type=jnp.float32)

    scale = QK_SCALE
    z = proj(C_QA, 2 * D_A)
    qa_ref[0] = (z[:, :D_A] * scale).astype(jnp.bfloat16)
    ka = z[:, D_A:]
    ka_ref[0] = ka.astype(jnp.bfloat16)
    z = proj(C_VA, 2 * D_A)
    va = z[:, :D_A]
    va_ref[0] = va.astype(jnp.bfloat16)
    ga_ref[0] = z[:, D_A:]

    @pl.when(i >= n_tiles - keep_tiles)
    def _():
        ak_ref[0] = ka
        av_ref[0] = va

    cos64, sin64 = rope_ref[0], rope_ref[1]
    cos32, sin32 = rope_ref[2], rope_ref[3]
    cos_kw, sin_kw = rope_ref[4], rope_ref[5]
    z = proj(C_QB, 2 * D_B)
    for s in range(D_B // LANES):
        sl = slice(s * LANES, (s + 1) * LANES)
        qb = _rope_slab(z[:, sl], cos64, sin64, D_HEAD // 2)
        qb_ref[0, :, sl] = (qb * scale).astype(jnp.bfloat16)
        kb = _rope_slab(z[:, D_B + s * LANES:D_B + (s + 1) * LANES], cos64, sin64, D_HEAD // 2)
        kb_ref[0, :, sl] = kb
        kbb_ref[0, :, sl] = kb.astype(jnp.bfloat16)
    z = proj(C_VB, 2 * D_B)
    vb_ref[0] = z[:, :D_B]
    vbb_ref[0] = z[:, :D_B].astype(jnp.bfloat16)
    gb_ref[0] = z[:, D_B:]
    qm_ref[0] = (proj(C_QM, D_M) * scale).astype(jnp.bfloat16)
    gm_ref[0] = proj(C_GM, D_M)
    z = proj(C_QI, H_IDX * D_IDX)
    for s in range(H_IDX * D_IDX // LANES):
        sl = slice(s * LANES, (s + 1) * LANES)
        qi_ref[0, :, sl] = _rope_slab(z[:, sl], cos32, sin32, D_IDX // 2).astype(jnp.bfloat16)
    z = proj(C_KIWI, 2 * LANES)
    kiwi = _rope_slab(z[:, :LANES], cos_kw, sin_kw, D_IDX // 2)
    kiwi_ref[0] = kiwi
    ki_ref[0] = kiwi[:, :D_IDX]
    ki4_ref[0] = _rope_slab(z[:, LANES:], cos32, sin32, D_IDX // 2).astype(jnp.bfloat16)


def _proj_call(x, g, w, rope, keep_rows):
    B, S, D = x.shape
    tm = min(PROJ_TM, S)
    n_tiles = S // tm
    keep_tiles = keep_rows // tm
    assert n_tiles * tm == S and keep_tiles * tm == keep_rows

    def tile(n, dtype):
        return (jax.ShapeDtypeStruct((B, S, n), dtype),
                pl.BlockSpec((1, tm, n), lambda b, i: (b, i, 0)))

    f32, bf16 = jnp.float32, jnp.bfloat16
    outs = [tile(D_A, bf16), tile(D_A, bf16), tile(D_A, bf16), tile(D_A, f32),
            tile(D_B, bf16), tile(D_B, f32), tile(D_B, f32), tile(D_B, f32),
            tile(D_M, bf16), tile(D_M, f32), tile(H_IDX * D_IDX, bf16),
            tile(LANES, f32), tile(LANES, bf16), tile(D_IDX, f32), tile(D_B, bf16), tile(D_B, bf16)]
    keep_spec = pl.BlockSpec(
        (1, tm, D_A), lambda b, i: (b, jnp.maximum(i - (n_tiles - keep_tiles), 0), 0))
    outs += [(jax.ShapeDtypeStruct((B, keep_rows, D_A), f32), keep_spec)] * 2
    return pl.pallas_call(
        functools.partial(_proj_kernel, n_tiles=n_tiles, keep_tiles=keep_tiles),
        grid=(B, n_tiles),
        in_specs=[pl.BlockSpec((1, tm, D), lambda b, i: (b, i, 0)),
                  pl.BlockSpec((1, D), lambda b, i: (0, 0)),
                  pl.BlockSpec((D, W_COLS), lambda b, i: (0, 0)),
                  pl.BlockSpec((6, tm, LANES), lambda b, i: (0, i, 0))],
        out_specs=[o[1] for o in outs],
        out_shape=[o[0] for o in outs],
        compiler_params=_cparams(2),
        name="proj",
    )(x, g, w, rope)


def _memkv_kernel(m_ref, g_ref, w_ref, mk_ref, mv_ref):
    xn = _rms_scale(m_ref[...], g_ref[...]).astype(jnp.bfloat16)
    mk_ref[...] = jnp.dot(xn, w_ref[:, :D_M], preferred_element_type=jnp.float32)
    mv_ref[...] = jnp.dot(xn, w_ref[:, D_M:], preferred_element_type=jnp.float32)


def _memkv_call(mem2d, g, w):
    R, D = mem2d.shape
    tm = min(PROJ_TM, R)
    assert R % tm == 0
    row = lambda n: pl.BlockSpec((tm, n), lambda i: (i, 0))
    return pl.pallas_call(
        _memkv_kernel,
        grid=(R // tm,),
        in_specs=[row(D), pl.BlockSpec((1, D), lambda i: (0, 0)),
                  pl.BlockSpec((D, 2 * D_M), lambda i: (0, 0))],
        out_specs=[row(D_M), row(D_M)],
        out_shape=[jax.ShapeDtypeStruct((R, D_M), jnp.float32)] * 2,
        compiler_params=_cparams(1),
        name="memkv",
    )(mem2d, g, w)


def _softmax_rows_pv(s, v):
    m = jnp.max(s, axis=1, keepdims=True)
    p = jnp.exp2(s - m)
    l = jnp.sum(p, axis=1, keepdims=True)
    o = jnp.dot(p.astype(jnp.bfloat16), v, preferred_element_type=jnp.float32)
    return o * (1.0 / l)


def _keep_lanes(slab, lo, hi):
    x = slab.astype(jnp.float32)
    lane = lax.broadcasted_iota(jnp.int32, x.shape, 1)
    return jnp.where(lane >= lo, jnp.where(lane < hi, x, 0.0), 0.0).astype(jnp.bfloat16)


def _pair_attend(q_slab, k_slab, v_slab, bias_fn):
    lane = lax.broadcasted_iota(jnp.int32, q_slab.shape, 1)
    out = None
    for hh in range(2):
        s = lax.dot_general(_keep_lanes(q_slab, hh * D_HEAD, (hh + 1) * D_HEAD), k_slab, _NT,
                            preferred_element_type=jnp.float32)
        o = _softmax_rows_pv(bias_fn(s, hh), v_slab)
        out = o if hh == 0 else jnp.where(lane < D_HEAD, out, o)
    return out


def _mem_attend(qm_ref, gm_ref, mk_ref, mv_ref, om_ref):
    for p in range(D_M // LANES):
        sl = slice(p * LANES, (p + 1) * LANES)
        o = _pair_attend(qm_ref[0, :, sl], mk_ref[0, :, sl].astype(jnp.bfloat16),
                         mv_ref[0, :, sl].astype(jnp.bfloat16), lambda s, hh: s)
        om_ref[0, :, sl] = (o * _silu(gm_ref[0, :, sl])).astype(jnp.bfloat16)


def _build_band_bias(vec_ref, bias_ref, n_q, n_valid, chunked):
    n_keys = bias_ref.shape[2]
    i = lax.broadcasted_iota(jnp.int32, (n_q, n_keys), 0)
    r = lax.broadcasted_iota(jnp.int32, (n_q, n_keys), 1)
    lo = ((i >> 6) << 6) if chunked else jnp.zeros_like(i)
    for h in range(bias_ref.shape[0]):
        base = jnp.broadcast_to(vec_ref[h:h + 1, :], (n_q, BIAS_VEC))
        rolled = pltpu.roll(base, 0, 1, stride=1, stride_axis=0)
        b = rolled[:, :n_keys] * LOG2E
        bias_ref[h] = jnp.where(r >= lo, jnp.where(r < lo + n_valid, b, _NEG_INF), _NEG_INF)


def _band_prompt_kernel(qa_ref, ka_ref, va_ref, ga_ref, qm_ref, gm_ref, mk_ref, mv_ref, vec_ref,
                        oa_ref, om_ref, kpad, vpad, bias_ref):
    j = pl.program_id(1)

    @pl.when(jnp.logical_and(pl.program_id(0) == 0, j == 0))
    def _():
        _build_band_bias(vec_ref, bias_ref, BAND_TQ, A_WINDOW + CHUNK, True)

    @pl.when(j == 0)
    def _():
        zeros = jnp.zeros((A_WINDOW, D_A), jnp.bfloat16)
        kpad[:A_WINDOW, :] = zeros
        vpad[:A_WINDOW, :] = zeros
        kpad[A_WINDOW:, :] = ka_ref[0]
        vpad[A_WINDOW:, :] = va_ref[0]

    start = pl.multiple_of(j * BAND_TQ, BAND_TQ)

    def run(mask_front):
        col = lax.broadcasted_iota(jnp.int32, (BAND_TQ, BAND_WIN), 1) + j * BAND_TQ
        for p in range(D_A // LANES):
            sl = slice(p * LANES, (p + 1) * LANES)

            def bias_fn(s, hh, p=p):
                s = s + bias_ref[2 * p + hh]
                if mask_front:
                    s = jnp.where(col >= A_WINDOW, s, _NEG_INF)
                return s

            o = _pair_attend(qa_ref[0, :, sl], kpad[pl.ds(start, BAND_WIN), sl],
                             vpad[pl.ds(start, BAND_WIN), sl], bias_fn)
            oa_ref[0, :, sl] = (o * _silu(ga_ref[0, :, sl])).astype(jnp.bfloat16)

    front_steps = A_WINDOW // BAND_TQ
    pl.when(j < front_steps)(lambda: run(True))
    pl.when(j >= front_steps)(lambda: run(False))
    _mem_attend(qm_ref, gm_ref, mk_ref, mv_ref, om_ref)


def _band_prompt_call(qa, ka, va, ga, qm, gm, mk, mv, bias_vec):
    B, S, _ = qa.shape
    nq = S // BAND_TQ
    n_mem = mk.shape[1]
    qblk = lambda n: pl.BlockSpec((1, BAND_TQ, n), lambda b, j: (b, j, 0))
    full = lambda r, n: pl.BlockSpec((1, r, n), lambda b, j: (b, 0, 0))
    return pl.pallas_call(
        _band_prompt_kernel,
        grid=(B, nq),
        in_specs=[qblk(D_A), full(S, D_A), full(S, D_A), qblk(D_A), qblk(D_M), qblk(D_M),
                  full(n_mem, D_M), full(n_mem, D_M),
                  pl.BlockSpec((H_A, BIAS_VEC), lambda b, j: (0, 0))],
        out_specs=[qblk(D_A), qblk(D_M)],
        out_shape=[jax.ShapeDtypeStruct((B, S, D_A), jnp.bfloat16),
                   jax.ShapeDtypeStruct((B, S, D_M), jnp.bfloat16)],
        scratch_shapes=[pltpu.VMEM((S + A_WINDOW, D_A), jnp.bfloat16)] * 2
        + [pltpu.VMEM((H_A, BAND_TQ, BAND_WIN), jnp.float32)],
        compiler_params=_cparams(2),
        name="band_prompt",
    )(qa, ka, va, ga, qm, gm, mk, mv, bias_vec)


def _band_sample_kernel(qa_ref, kc_ref, vc_ref, kn_ref, vn_ref, ga_ref, qm_ref, gm_ref,
                        mk_ref, mv_ref, vec_ref, oa_ref, om_ref, kcat, vcat, bias_ref):
    P = kc_ref.shape[1]
    T = kn_ref.shape[1]
    pad = kcat.shape[0] - P - T

    @pl.when(pl.program_id(0) == 0)
    def _():
        _build_band_bias(vec_ref, bias_ref, T, P + T, False)

    kcat[:P, :] = kc_ref[0].astype(jnp.bfloat16)
    vcat[:P, :] = vc_ref[0].astype(jnp.bfloat16)
    kcat[P:P + T, :] = kn_ref[0]
    vcat[P:P + T, :] = vn_ref[0]
    zeros = jnp.zeros((pad, D_A), jnp.bfloat16)
    kcat[P + T:, :] = zeros
    vcat[P + T:, :] = zeros
    for p in range(D_A // LANES):
        sl = slice(p * LANES, (p + 1) * LANES)
        o = _pair_attend(qa_ref[0, :, sl], kcat[:, sl], vcat[:, sl],
                         lambda s, hh, p=p: s + bias_ref[2 * p + hh])
        oa_ref[0, :, sl] = (o * _silu(ga_ref[0, :, sl])).astype(jnp.bfloat16)
    _mem_attend(qm_ref, gm_ref, mk_ref, mv_ref, om_ref)


def _band_sample_call(qa, kc, vc, kn, vn, ga, qm, gm, mk, mv, bias_vec):
    B, T, _ = qa.shape
    P = kc.shape[1]
    n_keys = -(-(P + T) // LANES) * LANES
    n_mem = mk.shape[1]
    blk = lambda r, n: pl.BlockSpec((1, r, n), lambda b: (b, 0, 0))
    return pl.pallas_call(
        _band_sample_kernel,
        grid=(B,),
        in_specs=[blk(T, D_A), blk(P, D_A), blk(P, D_A), blk(T, D_A), blk(T, D_A), blk(T, D_A),
                  blk(T, D_M), blk(T, D_M), blk(n_mem, D_M), blk(n_mem, D_M),
                  pl.BlockSpec((H_A, BIAS_VEC), lambda b: (0, 0))],
        out_specs=[blk(T, D_A), blk(T, D_M)],
        out_shape=[jax.ShapeDtypeStruct((B, T, D_A), jnp.bfloat16),
                   jax.ShapeDtypeStruct((B, T, D_M), jnp.bfloat16)],
        scratch_shapes=[pltpu.VMEM((n_keys, D_A), jnp.bfloat16)] * 2
        + [pltpu.VMEM((H_A, T, n_keys), jnp.float32)],
        compiler_params=_cparams(1),
        name="band_sample",
    )(qa, kc, vc, kn, vn, ga, qm, gm, mk, mv, bias_vec)


def _key_to_f32(k):
    bits = k ^ ((k >> 31) & jnp.int32(0x7FFFFFFF))
    return lax.bitcast_convert_type(bits, jnp.float32)


def _count(pred_f32):
    ones = jnp.ones((8, pred_f32.shape[0]), jnp.bfloat16)
    c = jnp.dot(ones, pred_f32.astype(jnp.bfloat16), preferred_element_type=jnp.float32)
    return c[0:1, :]


def _dsa_select(sc_ref, probs, tq, topk):
    one, zero = jnp.float32(1.0), jnp.float32(0.0)
    live = [(off, tk) for off, tk in probs if tk > topk]
    n_live = len(live)

    def sc_of(n):
        off, tk = live[n]
        return sc_ref[off:off + tk, :]

    if live:
        def thr_step(i, ts):
            bit = lax.shift_left(jnp.int32(1), 31 - i)
            out = []
            for n in range(n_live):
                cand = ts[n] + bit
                c = _count(jnp.where(sc_of(n) >= _key_to_f32(cand), one, zero))
                out.append(jnp.where(c >= topk, cand, ts[n]))
            return tuple(out)

        start = tuple(jnp.full((1, tq), _INT_MIN, jnp.int32) for _ in range(n_live))
        t_keys = lax.fori_loop(0, 32, thr_step, start)
        thrs = [jnp.where(t == _INT_MIN, _NEG_INF, _key_to_f32(t)) for t in t_keys]

        jmaxs = []
        for n in range(n_live):
            tk = live[n][1]
            sc = sc_of(n)
            c_gt = _count(jnp.where(sc > thrs[n], one, zero))
            c_ge = _count(jnp.where(sc >= thrs[n], one, zero))
            need = topk - c_gt
            nbits = tk.bit_length()

            def tie_search(n=n, tk=tk, need=need, nbits=nbits):
                def tie_step(i, jmax):
                    cand = jmax + lax.shift_left(jnp.int32(1), nbits - 1 - i)
                    kidx = lax.broadcasted_iota(jnp.int32, (tk, tq), 0)
                    f = _count(jnp.where(sc_of(n) == thrs[n], jnp.where(kidx < cand, one, zero), zero))
                    return jnp.where(f < need, cand, jmax)

                return lax.fori_loop(0, nbits, tie_step, jnp.zeros((1, tq), jnp.int32))

            has_excess = jnp.max(jnp.where(c_ge > topk, one, zero)) > 0
            jmaxs.append(lax.cond(has_excess, tie_search,
                                  lambda tk=tk: jnp.full((1, tq), tk, jnp.int32)))

    for off, tk in probs:
        sc = sc_ref[off:off + tk, :]
        keep = jnp.where(jnp.abs(sc) < jnp.float32(jnp.inf), zero, _NEG_INF)
        if tk > topk:
            n = live.index((off, tk))
            kidx = lax.broadcasted_iota(jnp.int32, (tk, tq), 0)
            keep = jnp.where(sc > thrs[n], keep,
                             jnp.where(sc == thrs[n], jnp.where(kidx <= jmaxs[n], keep, _NEG_INF),
                                       _NEG_INF))
        sc_ref[off:off + tk, :] = keep


def _dsa_scores(limit, qi, kiwi, ki4):
    tk, tq = ki4.shape[0], qi.shape[0]
    kiwi_t = kiwi.T
    acc = jnp.zeros((tk, tq), jnp.float32)
    heads_per_slab = LANES // D_IDX
    for h in range(H_IDX):
        slab = qi[:, (h // heads_per_slab) * LANES:(h // heads_per_slab + 1) * LANES]
        lo = (h % heads_per_slab) * D_IDX
        d = lax.dot_general(ki4, _keep_lanes(slab, lo, lo + D_IDX), _NT,
                            preferred_element_type=jnp.float32)
        w = kiwi_t[D_IDX + h:D_IDX + h + 1, :] * ((D_IDX ** -0.5) * (H_IDX ** -0.5))
        acc = acc + w * jnp.maximum(d, 0.0)
    if limit is not None:
        kidx = lax.broadcasted_iota(jnp.int32, (tk, tq), 0)
        acc = jnp.where(kidx < limit, acc, _NEG_INF)
    return acc


def _dsa_attend(bias, q, g, k_slab_of, vt_slab_of, store):
    tq = q.shape[0]
    row = lax.broadcasted_iota(jnp.int32, (LANES, tq), 0)
    for p in range(D_B // LANES):
        sl = slice(p * LANES, (p + 1) * LANES)
        q_slab = q[:, sl]
        k_slab = k_slab_of(sl)
        v_slab = vt_slab_of(sl)
        ot = None
        for hh in range(2):
            s = lax.dot_general(k_slab, _keep_lanes(q_slab, hh * D_HEAD, (hh + 1) * D_HEAD), _NT,
                                preferred_element_type=jnp.float32)
            s = s + bias
            m = jnp.max(s, axis=0, keepdims=True)
            pr = jnp.exp2(s - m)
            l = jnp.sum(pr, axis=0, keepdims=True)
            o = jnp.dot(v_slab, pr.astype(jnp.bfloat16), preferred_element_type=jnp.float32)
            o = o * (1.0 / l)
            ot = o if hh == 0 else jnp.where(row < D_HEAD, ot, o)
        store(sl, (ot.T * _silu(g[:, sl])).astype(jnp.bfloat16))


def _dsa_prompt_kernel(q_ref, qi_ref, kiwi_ref, g_ref, k_ref, v_ref, ki4_ref, o_ref, vt, sc_ref,
                       *, topk, tq):
    S = k_ref.shape[1]
    vt[...] = v_ref[0].astype(jnp.float32).T.astype(jnp.bfloat16)
    own_region = pl.when(pl.program_id(0) >= 0)
    probs, off = [], 0
    for j in range(S // tq):
        tk = (j + 1) * tq
        rows = slice(j * tq, (j + 1) * tq)

        @own_region
        def _(j=j, tk=tk, off=off, rows=rows):
            qpos = j * tq + lax.broadcasted_iota(jnp.int32, (1, tq), 1)
            limit = ((qpos >> 6) + 1) << 6
            sc_ref[off:off + tk, :] = _dsa_scores(limit, qi_ref[0, rows, :], kiwi_ref[0, rows, :],
                                                  ki4_ref[0, :tk, :])

        probs.append((off, tk))
        off += tk
    _dsa_select(sc_ref, probs, tq, topk)
    for j, (off, tk) in enumerate(probs):
        rows = slice(j * tq, (j + 1) * tq)

        @own_region
        def _(tk=tk, off=off, rows=rows):
            def store(sl, o):
                o_ref[0, rows, sl] = o

            _dsa_attend(sc_ref[off:off + tk, :], q_ref[0, rows, :], g_ref[0, rows, :],
                        lambda sl: k_ref[0, :tk, sl], lambda sl: vt[sl, :tk], store)


def _dsa_prompt_call(qb, qi, kiwi, gb, kb, vb, ki4):
    B, S, _ = qb.shape
    tq = DSA_TQ
    nq = S // tq
    assert CHUNK == 64 and nq * tq == S
    topk = min(TOPK_MAX, S // 4)
    full = lambda n: pl.BlockSpec((1, S, n), lambda b: (b, 0, 0))
    return pl.pallas_call(
        functools.partial(_dsa_prompt_kernel, topk=topk, tq=tq),
        grid=(B,),
        in_specs=[full(D_B), full(H_IDX * D_IDX), full(LANES), full(D_B),
                  full(D_B), full(D_B), full(LANES)],
        out_specs=full(D_B),
        out_shape=jax.ShapeDtypeStruct((B, S, D_B), jnp.bfloat16),
        scratch_shapes=[pltpu.VMEM((D_B, S), jnp.bfloat16),
                        pltpu.VMEM((tq * nq * (nq + 1) // 2, tq), jnp.float32)],
        compiler_params=_cparams(1),
        name="dsa_prompt",
    )(qb, qi, kiwi, gb, kb, vb, ki4)


def _dsa_sample_kernel(q_ref, qi_ref, kiwi_ref, g_ref, kc_ref, vc_ref, kic_ref,
                       kn_ref, vn_ref, kin_ref, o_ref, kbf, vt, ki4s, sc_ref, *, topk):
    P = kc_ref.shape[1]
    T = kn_ref.shape[1]
    tq = q_ref.shape[1]
    kbf[:P, :] = kc_ref[0].astype(jnp.bfloat16)
    kbf[P:, :] = kn_ref[0].astype(jnp.bfloat16)
    vt[:, :P] = vc_ref[0].T.astype(jnp.bfloat16)
    vn_tile = jnp.concatenate([vn_ref[0], jnp.zeros((LANES - T, D_B), jnp.float32)], axis=0)
    vt[:, P:] = vn_tile.T[:, :T].astype(jnp.bfloat16)
    ki4s[:P, :] = kic_ref[0]
    ki4s[P:, :] = kin_ref[0]
    sc_ref[...] = _dsa_scores(None, qi_ref[0], kiwi_ref[0], ki4s[...])
    _dsa_select(sc_ref, [(0, P + T)], tq, topk)

    def store(sl, o):
        o_ref[0, :, sl] = o

    _dsa_attend(sc_ref[...], q_ref[0], g_ref[0], lambda sl: kbf[:, sl], lambda sl: vt[sl, :], store)


def _dsa_sample_call(qb, qi, kiwi, gb, kc, vc, kic, kn, vn, kin):
    B, tq, _ = qb.shape
    P, T = kc.shape[1], kn.shape[1]
    tk = P + T
    topk = min(TOPK_MAX, tk // 4)
    blk = lambda r, n: pl.BlockSpec((1, r, n), lambda b: (b, 0, 0))
    return pl.pallas_call(
        functools.partial(_dsa_sample_kernel, topk=topk),
        grid=(B,),
        in_specs=[blk(tq, D_B), blk(tq, H_IDX * D_IDX), blk(tq, LANES), blk(tq, D_B),
                  blk(P, D_B), blk(P, D_B), blk(P, LANES),
                  blk(T, D_B), blk(T, D_B), blk(T, LANES)],
        out_specs=blk(tq, D_B),
        out_shape=jax.ShapeDtypeStruct((B, tq, D_B), jnp.bfloat16),
        scratch_shapes=[pltpu.VMEM((tk, D_B), jnp.bfloat16),
                        pltpu.VMEM((D_B, tk), jnp.bfloat16),
                        pltpu.VMEM((tk, LANES), jnp.bfloat16),
                        pltpu.VMEM((tk, tq), jnp.float32)],
        compiler_params=_cparams(1),
        name="dsa_sample",
    )(qb, qi, kiwi, gb, kc, vc, kic, kn, vn, kin)


def _merge_kernel(x_ref, oa_ref, ob_ref, om_ref, w_ref, g_ref, y_ref, *, last):
    acc = x_ref[...]
    acc = acc + jnp.dot(oa_ref[...], w_ref[:D_A, :], preferred_element_type=jnp.float32)
    acc = acc + jnp.dot(ob_ref[...], w_ref[D_A:D_A + D_B, :], preferred_element_type=jnp.float32)
    acc = acc + jnp.dot(om_ref[...], w_ref[D_A + D_B:, :], preferred_element_type=jnp.float32)
    y_ref[...] = _rms_scale(acc, g_ref[...]) if last else acc


def _merge_call(x2d, oa, ob, om, w, g, last):
    R, D = x2d.shape
    tm = min(PROJ_TM, R)
    assert R % tm == 0
    row = lambda n: pl.BlockSpec((tm, n), lambda i: (i, 0))
    return pl.pallas_call(
        functools.partial(_merge_kernel, last=last),
        grid=(R // tm,),
        in_specs=[row(D), row(D_A), row(D_B), row(D_M),
                  pl.BlockSpec((D_A + D_B + D_M, D), lambda i: (0, 0)),
                  pl.BlockSpec((1, D), lambda i: (0, 0))],
        out_specs=row(D),
        out_shape=jax.ShapeDtypeStruct((R, D), jnp.float32),
        compiler_params=_cparams(1),
        name="merge",
    )(x2d, oa, ob, om, w, g)


def _pack_w_in(w):
    D = w.shape[0]
    ki = w[:, C_KIWI:C_KIWI + D_IDX]
    wi = w[:, C_KIWI + D_IDX:C_KIWI + D_IDX + H_IDX]
    pad = jnp.zeros((D, LANES - D_IDX - H_IDX), w.dtype)
    packed = jnp.concatenate([w[:, :C_KIWI], ki, wi, pad, jnp.tile(ki, (1, LANES // D_IDX))], axis=1)
    return packed.astype(jnp.bfloat16)


def _rope_tables(pos):
    posf = pos.astype(jnp.float32)

    def tables(d):
        half = d // 2
        inv_freq = ROPE_THETA ** (-jnp.arange(half, dtype=jnp.float32) * 2.0 / d)
        ang = posf[:, None] * inv_freq[None, :]
        cos, sin = jnp.cos(ang), jnp.sin(ang)
        return jnp.concatenate([cos, cos], axis=1), jnp.concatenate([-sin, sin], axis=1)

    cos64, sin64 = tables(D_HEAD)
    cos32, sin32 = tables(D_IDX)
    n = pos.shape[0]
    rest = LANES - D_IDX
    cos_kw = jnp.concatenate([cos32, jnp.ones((n, rest), jnp.float32)], axis=1)
    sin_kw = jnp.concatenate([sin32, jnp.zeros((n, rest), jnp.float32)], axis=1)
    rep = lambda t, d: jnp.tile(t, (1, LANES // d))
    return jnp.stack([rep(cos64, D_HEAD), rep(sin64, D_HEAD), rep(cos32, D_IDX), rep(sin32, D_IDX),
                      cos_kw, sin_kw])


def _band_bias_vec(table):
    n_far = A_WINDOW - REL_CLIP + 1
    lo_idx = A_WINDOW + REL_CLIP - (BAND_WIN - 1)
    assert lo_idx >= 0 and BIAS_VEC >= BAND_WIN + BAND_TQ - 1
    far = table[:, 2 * REL_CLIP:2 * REL_CLIP + 1].astype(jnp.float32)
    mid = table[:, lo_idx:2 * REL_CLIP][:, ::-1].astype(jnp.float32)
    return jnp.concatenate([jnp.tile(far, (1, n_far)), mid, jnp.tile(far, (1, BIAS_VEC - BAND_WIN))], axis=1)


def kernel(x_prompt, x_sample, mem_prompt, cache_a_k, cache_a_v, cache_b_k, cache_b_v, cache_b_kidx,
           cache_mem_k, cache_mem_v, norm_mix_g, w_in, rel_bias_a, norm_mem_g, w_mem_kv, w_out,
           norm_final_g):
    B, S, D = x_prompt.shape
    Bs, T, _ = x_sample.shape
    depth = w_in.shape[0]
    P = cache_b_k.shape[2]
    Pa = cache_a_k.shape[2]
    n_mem = mem_prompt.shape[1]
    keep = min(A_WINDOW, S)
    assert Pa == A_WINDOW and T <= CHUNK and LANES % T == 0

    rope_p = _rope_tables(jnp.arange(S))
    rope_s = jnp.tile(_rope_tables(P + jnp.arange(T)), (1, Bs, 1))
    g_final = norm_final_g.reshape(1, D)

    xp, xs = x_prompt, x_sample
    outs_p = [[] for _ in range(7)]
    outs_s = [[] for _ in range(5)]
    for l in range(depth):
        last = l == depth - 1
        w = _pack_w_in(w_in[l])
        g_mix = norm_mix_g[l].reshape(1, D)
        w_o = w_out[l].astype(jnp.bfloat16)
        bias_vec = _band_bias_vec(rel_bias_a[l])

        (qa, ka, va, ga, qb, kb, vb, gb, qm, gm, qi, kiwi, ki4, ki, kbb, vbb, ak, av) = _proj_call(
            xp, g_mix, w, rope_p, keep)
        mk, mv = _memkv_call(mem_prompt.reshape(B * n_mem, D), norm_mem_g[l].reshape(1, D),
                             w_mem_kv[l].astype(jnp.bfloat16))
        mk = mk.reshape(B, n_mem, D_M)
        mv = mv.reshape(B, n_mem, D_M)
        oa, om = _band_prompt_call(qa, ka, va, ga, qm, gm, mk, mv, bias_vec)
        ob = _dsa_prompt_call(qb, qi, kiwi, gb, kbb, vbb, ki4)
        xp = _merge_call(xp.reshape(B * S, D), oa.reshape(B * S, D_A), ob.reshape(B * S, D_B),
                         om.reshape(B * S, D_M), w_o, g_final, last).reshape(B, S, D)
        for lst, t in zip(outs_p, (ak.reshape(B, keep, H_A, D_HEAD), av.reshape(B, keep, H_A, D_HEAD),
                                   kb.reshape(B, S, H_B, D_HEAD), vb.reshape(B, S, H_B, D_HEAD), ki,
                                   mk.reshape(B, n_mem, H_M, D_HEAD), mv.reshape(B, n_mem, H_M, D_HEAD))):
            lst.append(t)

        (qa, ka, va, ga, qb, kb, vb, gb, qm, gm, qi, kiwi, ki4, ki, kbb, vbb, ak, av) = _proj_call(
            xs.reshape(1, Bs * T, D), g_mix, w, rope_s, Bs * T)
        per_b = lambda t: t.reshape(Bs, T, t.shape[-1])
        oa, om = _band_sample_call(
            per_b(qa), cache_a_k[l].reshape(Bs, Pa, D_A), cache_a_v[l].reshape(Bs, Pa, D_A),
            per_b(ka), per_b(va), per_b(ga), per_b(qm), per_b(gm),
            cache_mem_k[l].reshape(Bs, n_mem, D_M), cache_mem_v[l].reshape(Bs, n_mem, D_M), bias_vec)
        rep = lambda t: jnp.tile(per_b(t), (1, LANES // T, 1))
        kic = jnp.tile(cache_b_kidx[l], (1, 1, LANES // D_IDX)).astype(jnp.bfloat16)
        ob = _dsa_sample_call(rep(qb), rep(qi), rep(kiwi), rep(gb),
                              cache_b_k[l].reshape(Bs, P, D_B), cache_b_v[l].reshape(Bs, P, D_B), kic,
                              per_b(kb), per_b(vb), per_b(ki4))[:, :T]
        xs = _merge_call(xs.reshape(Bs * T, D), oa.reshape(Bs * T, D_A), ob.reshape(Bs * T, D_B),
                         om.reshape(Bs * T, D_M), w_o, g_final, last).reshape(Bs, T, D)
        for lst, t in zip(outs_s, (ak.reshape(Bs, T, H_A, D_HEAD), av.reshape(Bs, T, H_A, D_HEAD),
                                   kb.reshape(Bs, T, H_B, D_HEAD), vb.reshape(Bs, T, H_B, D_HEAD),
                                   ki.reshape(Bs, T, D_IDX))):
            lst.append(t)

    st = lambda ts: jnp.stack(ts, axis=0)
    return (xp, xs) + tuple(st(t) for t in outs_p) + tuple(st(t) for t in outs_s)
```

```python
import functools
import math

import jax
import jax.numpy as jnp
from jax import lax
from jax.experimental import pallas as pl
from jax.experimental.pallas import tpu as pltpu

CHUNK = 64
D_HEAD = 64
H_A = 6
H_B = 6
H_M = 4
D_A = H_A * D_HEAD
D_B = H_B * D_HEAD
D_M = H_M * D_HEAD
A_LEFT_CHUNKS = 8
A_WINDOW = A_LEFT_CHUNKS * CHUNK
REL_CLIP = 256
H_IDX = 8
D_IDX = 32
TOPK_MAX = 256
ROPE_THETA = 10000.0
EPS = 1e-6

LANES = 128
VMEM_LIMIT_BYTES = 56 * 1024 * 1024

C_QA, C_KA, C_VA, C_GA = 0, D_A, 2 * D_A, 3 * D_A
C_QB = 4 * D_A
C_KB, C_VB, C_GB = C_QB + D_B, C_QB + 2 * D_B, C_QB + 3 * D_B
C_QM = C_QB + 4 * D_B
C_GM = C_QM + D_M
C_QI = C_GM + D_M
C_KIWI = C_QI + H_IDX * D_IDX
C_KI4 = C_KIWI + LANES
W_COLS = C_KI4 + LANES

PROJ_TM = 512
BAND_TQ = 256
BAND_WIN = A_WINDOW + BAND_TQ
DSA_TQ = 256

LOG2E = math.log2(math.e)
QK_SCALE = (D_HEAD ** -0.5) * LOG2E
BIAS_VEC = 8 * LANES

_NT = (((1,), (1,)), ((), ()))
_INT_MIN = -2 ** 31
_NEG_INF = float("-inf")


def _cparams(n_axes):
    return pltpu.CompilerParams(
        dimension_semantics=("arbitrary",) * n_axes,
        vmem_limit_bytes=VMEM_LIMIT_BYTES)


def _silu(g):
    return g * (1.0 / (1.0 + jnp.exp(-g)))


def _rms_scale(x, g):
    ms = jnp.mean(x * x, axis=-1, keepdims=True)
    return (x * lax.rsqrt(ms + EPS)) * g


def _rope_slab(z, cos, sin, half):
    lane = lax.broadcasted_iota(jnp.int32, z.shape, 1)
    first = (lane & (2 * half - 1)) < half
    partner = jnp.where(first, pltpu.roll(z, LANES - half, 1), pltpu.roll(z, half, 1))
    return z * cos + partner * sin


def _proj_kernel(x_ref, g_ref, w_ref, rope_ref,
                 qa_ref, ka_ref, va_ref, ga_ref, qb_ref, kb_ref, vb_ref, gb_ref,
                 qm_ref, gm_ref, qi_ref, kiwi_ref, ki4_ref, ki_ref, kbb_ref, vbb_ref, ak_ref, av_ref,
                 *, n_tiles, keep_tiles):
    i = pl.program_id(1)
    xn = _rms_scale(x_ref[0], g_ref[...]).astype(jnp.bfloat16)

    def proj(c0, n):
        return jnp.dot(xn, w_ref[:, c0:c0 + n], preferred_element_type=jnp.float32)

    scale = QK_SCALE
    z = proj(C_QA, 2 * D_A)
    qa_ref[0] = (z[:, :D_A] * scale).astype(jnp.bfloat16)
    ka = z[:, D_A:]
    ka_ref[0] = ka.astype(jnp.bfloat16)
    z = proj(C_VA, 2 * D_A)
    va = z[:, :D_A]
    va_ref[0] = va.astype(jnp.bfloat16)
    ga_ref[0] = z[:, D_A:]

    @pl.when(i >= n_tiles - keep_tiles)
    def _():
        ak_ref[0] = ka
        av_ref[0] = va

    cos64, sin64 = rope_ref[0], rope_ref[1]
    cos32, sin32 = rope_ref[2], rope_ref[3]
    cos_kw, sin_kw = rope_ref[4], rope_ref[5]
    z = proj(C_QB, 2 * D_B)
    for s in range(D_B // LANES):
        sl = slice(s * LANES, (s + 1) * LANES)
        qb = _rope_slab(z[:, sl], cos64, sin64, D_HEAD // 2)
        qb_ref[0, :, sl] = (qb * scale).astype(jnp.bfloat16)
        kb = _rope_slab(z[:, D_B + s * LANES:D_B + (s + 1) * LANES], cos64, sin64, D_HEAD // 2)
        kb_ref[0, :, sl] = kb
        kbb_ref[0, :, sl] = kb.astype(jnp.bfloat16)
    z = proj(C_VB, 2 * D_B)
    vb_ref[0] = z[:, :D_B]
    vbb_ref[0] = z[:, :D_B].astype(jnp.bfloat16)
    gb_ref[0] = z[:, D_B:]
    qm_ref[0] = (proj(C_QM, D_M) * scale).astype(jnp.bfloat16)
    gm_ref[0] = proj(C_GM, D_M)
    z = proj(C_QI, H_IDX * D_IDX)
    for s in range(H_IDX * D_IDX // LANES):
        sl = slice(s * LANES, (s + 1) * LANES)
        qi_ref[0, :, sl] = _rope_slab(z[:, sl], cos32, sin32, D_IDX // 2).astype(jnp.bfloat16)
    z = proj(C_KIWI, 2 * LANES)
    kiwi = _rope_slab(z[:, :LANES], cos_kw, sin_kw, D_IDX // 2)
    kiwi_ref[0] = kiwi
    ki_ref[0] = kiwi[:, :D_IDX]
    ki4_ref[0] = _rope_slab(z[:, LANES:], cos32, sin32, D_IDX // 2).astype(jnp.bfloat16)


def _proj_call(x, g, w, rope, keep_rows):
    B, S, D = x.shape
    tm = min(PROJ_TM, S)
    n_tiles = S // tm
    keep_tiles = keep_rows // tm
    assert n_tiles * tm == S and keep_tiles * tm == keep_rows

    def tile(n, dtype):
        return (jax.ShapeDtypeStruct((B, S, n), dtype),
                pl.BlockSpec((1, tm, n), lambda b, i: (b, i, 0)))

    f32, bf16 = jnp.float32, jnp.bfloat16
    outs = [tile(D_A, bf16), tile(D_A, bf16), tile(D_A, bf16), tile(D_A, f32),
            tile(D_B, bf16), tile(D_B, f32), tile(D_B, f32), tile(D_B, f32),
            tile(D_M, bf16), tile(D_M, f32), tile(H_IDX * D_IDX, bf16),
            tile(LANES, f32), tile(LANES, bf16), tile(D_IDX, f32), tile(D_B, bf16), tile(D_B, bf16)]
    keep_spec = pl.BlockSpec(
        (1, tm, D_A), lambda b, i: (b, jnp.maximum(i - (n_tiles - keep_tiles), 0), 0))
    outs += [(jax.ShapeDtypeStruct((B, keep_rows, D_A), f32), keep_spec)] * 2
    return pl.pallas_call(
        functools.partial(_proj_kernel, n_tiles=n_tiles, keep_tiles=keep_tiles),
        grid=(B, n_tiles),
        in_specs=[pl.BlockSpec((1, tm, D), lambda b, i: (b, i, 0)),
                  pl.BlockSpec((1, D), lambda b, i: (0, 0)),
                  pl.BlockSpec((D, W_COLS), lambda b, i: (0, 0)),
                  pl.BlockSpec((6, tm, LANES), lambda b, i: (0, i, 0))],
        out_specs=[o[1] for o in outs],
        out_shape=[o[0] for o in outs],
        compiler_params=_cparams(2),
        name="proj",
    )(x, g, w, rope)


def _memkv_kernel(m_ref, g_ref, w_ref, mk_ref, mv_ref):
    xn = _rms_scale(m_ref[...], g_ref[...]).astype(jnp.bfloat16)
    mk_ref[...] = jnp.dot(xn, w_ref[:, :D_M], preferred_element_type=jnp.float32)
    mv_ref[...] = jnp.dot(xn, w_ref[:, D_M:], preferred_element_type=jnp.float32)


def _memkv_call(mem2d, g, w):
    R, D = mem2d.shape
    tm = min(PROJ_TM, R)
    assert R % tm == 0
    row = lambda n: pl.BlockSpec((tm, n), lambda i: (i, 0))
    return pl.pallas_call(
        _memkv_kernel,
        grid=(R // tm,),
        in_specs=[row(D), pl.BlockSpec((1, D), lambda i: (0, 0)),
                  pl.BlockSpec((D, 2 * D_M), lambda i: (0, 0))],
        out_specs=[row(D_M), row(D_M)],
        out_shape=[jax.ShapeDtypeStruct((R, D_M), jnp.float32)] * 2,
        compiler_params=_cparams(1),
        name="memkv",
    )(mem2d, g, w)


def _softmax_rows_pv(s, v):
    m = jnp.max(s, axis=1, keepdims=True)
    p = jnp.exp2(s - m)
    l = jnp.sum(p, axis=1, keepdims=True)
    o = jnp.dot(p.astype(jnp.bfloat16), v, preferred_element_type=jnp.float32)
    return o * (1.0 / l)


def _keep_lanes(slab, lo, hi):
    x = slab.astype(jnp.float32)
    lane = lax.broadcasted_iota(jnp.int32, x.shape, 1)
    return jnp.where(lane >= lo, jnp.where(lane < hi, x, 0.0), 0.0).astype(jnp.bfloat16)


def _pair_attend(q_slab, k_slab, v_slab, bias_fn):
    lane = lax.broadcasted_iota(jnp.int32, q_slab.shape, 1)
    out = None
    for hh in range(2):
        s = lax.dot_general(_keep_lanes(q_slab, hh * D_HEAD, (hh + 1) * D_HEAD), k_slab, _NT,
                            preferred_element_type=jnp.float32)
        o = _softmax_rows_pv(bias_fn(s, hh), v_slab)
        out = o if hh == 0 else jnp.where(lane < D_HEAD, out, o)
    return out


def _mem_attend(qm_ref, gm_ref, mk_ref, mv_ref, om_ref):
    for p in range(D_M // LANES):
        sl = slice(p * LANES, (p + 1) * LANES)
        o = _pair_attend(qm_ref[0, :, sl], mk_ref[0, :, sl].astype(jnp.bfloat16),
                         mv_ref[0, :, sl].astype(jnp.bfloat16), lambda s, hh: s)
        om_ref[0, :, sl] = (o * _silu(gm_ref[0, :, sl])).astype(jnp.bfloat16)


def _build_band_bias(vec_ref, bias_ref, n_q, n_valid, chunked):
    n_keys = bias_ref.shape[2]
    i = lax.broadcasted_iota(jnp.int32, (n_q, n_keys), 0)
    r = lax.broadcasted_iota(jnp.int32, (n_q, n_keys), 1)
    lo = ((i >> 6) << 6) if chunked else jnp.zeros_like(i)
    for h in range(bias_ref.shape[0]):
        base = jnp.broadcast_to(vec_ref[h:h + 1, :], (n_q, BIAS_VEC))
        rolled = pltpu.roll(base, 0, 1, stride=1, stride_axis=0)
        b = rolled[:, :n_keys] * LOG2E
        bias_ref[h] = jnp.where(r >= lo, jnp.where(r < lo + n_valid, b, _NEG_INF), _NEG_INF)


def _build_band_bias_t(rvec_ref, bias_ref):
    n_keys, n_q = bias_ref.shape[1], bias_ref.shape[2]
    r = lax.broadcasted_iota(jnp.int32, (n_keys, n_q), 0)
    i = lax.broadcasted_iota(jnp.int32, (n_keys, n_q), 1)
    lo = (i >> 6) << 6
    for h in range(bias_ref.shape[0]):
        base = jnp.broadcast_to(rvec_ref[h:h + 1, :], (n_keys, BIAS_VEC))
        rolled = pltpu.roll(base, 0, 1, stride=1, stride_axis=0)
        b = rolled[:, :n_q] * LOG2E
        bias_ref[h] = jnp.where(r >= lo, jnp.where(r < lo + A_WINDOW + CHUNK, b, _NEG_INF), _NEG_INF)


def _band_prompt_kernel(qa_ref, ka_ref, va_ref, ga_ref, qm_ref, gm_ref, mk_ref, mv_ref, rvec_ref,
                        oa_ref, om_ref, kpad, vt_blk, mk_bf, mv_t, bias_ref):
    j = pl.program_id(1)
    n_front = A_WINDOW // BAND_TQ
    n_win = BAND_WIN // BAND_TQ

    @pl.when(jnp.logical_and(pl.program_id(0) == 0, j == 0))
    def _():
        _build_band_bias_t(rvec_ref, bias_ref)

    @pl.when(j == 0)
    def _():
        kpad[:A_WINDOW, :] = jnp.zeros((A_WINDOW, D_A), jnp.bfloat16)
        kpad[A_WINDOW:, :] = ka_ref[0]
        for t in range(vt_blk.shape[0]):
            if t < n_front:
                vt_blk[t] = jnp.zeros((D_A, BAND_TQ), jnp.bfloat16)
            else:
                rows = slice((t - n_front) * BAND_TQ, (t - n_front + 1) * BAND_TQ)
                vt_blk[t] = va_ref[0, rows, :].astype(jnp.float32).T.astype(jnp.bfloat16)
        mk_bf[...] = mk_ref[0].astype(jnp.bfloat16)
        mv_t[...] = mv_ref[0].T.astype(jnp.bfloat16)

    start = pl.multiple_of(j * BAND_TQ, BAND_TQ)

    def store_a(sl, o):
        oa_ref[0, :, sl] = o

    def store_m(sl, o):
        om_ref[0, :, sl] = o

    def run(mask_front):
        if mask_front:
            pos = lax.broadcasted_iota(jnp.int32, (BAND_WIN, BAND_TQ), 0) + j * BAND_TQ
            front = jnp.where(pos >= A_WINDOW, 0.0, _NEG_INF)
        _kq_attend(lambda h: bias_ref[h] + front if mask_front else bias_ref[h],
                   qa_ref[0], ga_ref[0],
                   lambda sl: kpad[pl.ds(start, BAND_WIN), sl],
                   lambda sl: jnp.concatenate([vt_blk[j + u, sl, :] for u in range(n_win)], axis=1),
                   store_a)

    pl.when(j < n_front)(lambda: run(True))
    pl.when(j >= n_front)(lambda: run(False))
    _kq_attend(lambda h: None, qm_ref[0], gm_ref[0], lambda sl: mk_bf[:, sl], lambda sl: mv_t[sl, :],
               store_m)


def _band_prompt_call(qa, ka, va, ga, qm, gm, mk, mv, bias_rvec):
    B, S, _ = qa.shape
    nq = S // BAND_TQ
    n_mem = mk.shape[1]
    assert nq * BAND_TQ == S and A_WINDOW % BAND_TQ == 0
    qblk = lambda n: pl.BlockSpec((1, BAND_TQ, n), lambda b, j: (b, j, 0))
    full = lambda r, n: pl.BlockSpec((1, r, n), lambda b, j: (b, 0, 0))
    return pl.pallas_call(
        _band_prompt_kernel,
        grid=(B, nq),
        in_specs=[qblk(D_A), full(S, D_A), full(S, D_A), qblk(D_A), qblk(D_M), qblk(D_M),
                  full(n_mem, D_M), full(n_mem, D_M),
                  pl.BlockSpec((H_A, BIAS_VEC), lambda b, j: (0, 0))],
        out_specs=[qblk(D_A), qblk(D_M)],
        out_shape=[jax.ShapeDtypeStruct((B, S, D_A), jnp.bfloat16),
                   jax.ShapeDtypeStruct((B, S, D_M), jnp.bfloat16)],
        scratch_shapes=[pltpu.VMEM((S + A_WINDOW, D_A), jnp.bfloat16),
                        pltpu.VMEM((nq + A_WINDOW // BAND_TQ, D_A, BAND_TQ), jnp.bfloat16),
                        pltpu.VMEM((n_mem, D_M), jnp.bfloat16),
                        pltpu.VMEM((D_M, n_mem), jnp.bfloat16),
                        pltpu.VMEM((H_A, BAND_WIN, BAND_TQ), jnp.float32)],
        compiler_params=_cparams(2),
        name="band_prompt",
    )(qa, ka, va, ga, qm, gm, mk, mv, bias_rvec)


def _band_sample_kernel(qa_ref, kc_ref, vc_ref, kn_ref, vn_ref, ga_ref, qm_ref, gm_ref,
                        mk_ref, mv_ref, vec_ref, oa_ref, om_ref, kcat, vcat, bias_ref):
    P = kc_ref.shape[1]
    T = kn_ref.shape[1]
    pad = kcat.shape[0] - P - T

    @pl.when(pl.program_id(0) == 0)
    def _():
        _build_band_bias(vec_ref, bias_ref, T, P + T, False)

    kcat[:P, :] = kc_ref[0].astype(jnp.bfloat16)
    vcat[:P, :] = vc_ref[0].astype(jnp.bfloat16)
    kcat[P:P + T, :] = kn_ref[0]
    vcat[P:P + T, :] = vn_ref[0]
    zeros = jnp.zeros((pad, D_A), jnp.bfloat16)
    kcat[P + T:, :] = zeros
    vcat[P + T:, :] = zeros
    for p in range(D_A // LANES):
        sl = slice(p * LANES, (p + 1) * LANES)
        o = _pair_attend(qa_ref[0, :, sl], kcat[:, sl], vcat[:, sl],
                         lambda s, hh, p=p: s + bias_ref[2 * p + hh])
        oa_ref[0, :, sl] = (o * _silu(ga_ref[0, :, sl])).astype(jnp.bfloat16)
    _mem_attend(qm_ref, gm_ref, mk_ref, mv_ref, om_ref)


def _band_sample_call(qa, kc, vc, kn, vn, ga, qm, gm, mk, mv, bias_vec):
    B, T, _ = qa.shape
    P = kc.shape[1]
    n_keys = -(-(P + T) // LANES) * LANES
    n_mem = mk.shape[1]
    blk = lambda r, n: pl.BlockSpec((1, r, n), lambda b: (b, 0, 0))
    return pl.pallas_call(
        _band_sample_kernel,
        grid=(B,),
        in_specs=[blk(T, D_A), blk(P, D_A), blk(P, D_A), blk(T, D_A), blk(T, D_A), blk(T, D_A),
                  blk(T, D_M), blk(T, D_M), blk(n_mem, D_M), blk(n_mem, D_M),
                  pl.BlockSpec((H_A, BIAS_VEC), lambda b: (0, 0))],
        out_specs=[blk(T, D_A), blk(T, D_M)],
        out_shape=[jax.ShapeDtypeStruct((B, T, D_A), jnp.bfloat16),
                   jax.ShapeDtypeStruct((B, T, D_M), jnp.bfloat16)],
        scratch_shapes=[pltpu.VMEM((n_keys, D_A), jnp.bfloat16)] * 2
        + [pltpu.VMEM((H_A, T, n_keys), jnp.float32)],
        compiler_params=_cparams(1),
        name="band_sample",
    )(qa, kc, vc, kn, vn, ga, qm, gm, mk, mv, bias_vec)


def _key_to_f32(k):
    bits = k ^ ((k >> 31) & jnp.int32(0x7FFFFFFF))
    return lax.bitcast_convert_type(bits, jnp.float32)


def _count(pred_f32):
    ones = jnp.ones((8, pred_f32.shape[0]), jnp.bfloat16)
    c = jnp.dot(ones, pred_f32.astype(jnp.bfloat16), preferred_element_type=jnp.float32)
    return c[0:1, :]


def _dsa_select(sc_ref, probs, tq, topk):
    one, zero = jnp.float32(1.0), jnp.float32(0.0)
    live = [(off, tk) for off, tk in probs if tk > topk]
    n_live = len(live)

    def sc_of(n):
        off, tk = live[n]
        return sc_ref[off:off + tk, :]

    if live:
        def thr_step(i, ts):
            bit = lax.shift_left(jnp.int32(1), 31 - i)
            out = []
            for n in range(n_live):
                cand = ts[n] + bit
                c = _count(jnp.where(sc_of(n) >= _key_to_f32(cand), one, zero))
                out.append(jnp.where(c >= topk, cand, ts[n]))
            return tuple(out)

        start = tuple(jnp.full((1, tq), _INT_MIN, jnp.int32) for _ in range(n_live))
        t_keys = lax.fori_loop(0, 32, thr_step, start)
        thrs = [jnp.where(t == _INT_MIN, _NEG_INF, _key_to_f32(t)) for t in t_keys]

        jmaxs = []
        for n in range(n_live):
            tk = live[n][1]
            sc = sc_of(n)
            c_gt = _count(jnp.where(sc > thrs[n], one, zero))
            c_ge = _count(jnp.where(sc >= thrs[n], one, zero))
            need = topk - c_gt
            nbits = tk.bit_length()

            def tie_search(n=n, tk=tk, need=need, nbits=nbits):
                def tie_step(i, jmax):
                    cand = jmax + lax.shift_left(jnp.int32(1), nbits - 1 - i)
                    kidx = lax.broadcasted_iota(jnp.int32, (tk, tq), 0)
                    f = _count(jnp.where(sc_of(n) == thrs[n], jnp.where(kidx < cand, one, zero), zero))
                    return jnp.where(f < need, cand, jmax)

                return lax.fori_loop(0, nbits, tie_step, jnp.zeros((1, tq), jnp.int32))

            has_excess = jnp.max(jnp.where(c_ge > topk, one, zero)) > 0
            jmaxs.append(lax.cond(has_excess, tie_search,
                                  lambda tk=tk: jnp.full((1, tq), tk, jnp.int32)))

    for off, tk in probs:
        sc = sc_ref[off:off + tk, :]
        keep = jnp.where(jnp.abs(sc) < jnp.float32(jnp.inf), zero, _NEG_INF)
        if tk > topk:
            n = live.index((off, tk))
            kidx = lax.broadcasted_iota(jnp.int32, (tk, tq), 0)
            keep = jnp.where(sc > thrs[n], keep,
                             jnp.where(sc == thrs[n], jnp.where(kidx <= jmaxs[n], keep, _NEG_INF),
                                       _NEG_INF))
        sc_ref[off:off + tk, :] = keep


def _dsa_scores(limit, qi, kiwi, ki4):
    tk, tq = ki4.shape[0], qi.shape[0]
    kiwi_t = kiwi.T
    acc = jnp.zeros((tk, tq), jnp.float32)
    heads_per_slab = LANES // D_IDX

    def head_q(h):
        slab = qi[:, (h // heads_per_slab) * LANES:(h // heads_per_slab + 1) * LANES]
        lo = (h % heads_per_slab) * D_IDX
        return _keep_lanes(slab, lo, lo + D_IDX)

    for h in range(0, H_IDX, 2):
        d = lax.dot_general(ki4, jnp.concatenate([head_q(h), head_q(h + 1)], axis=0), _NT,
                            preferred_element_type=jnp.float32)
        for e in range(2):
            w = kiwi_t[D_IDX + h + e:D_IDX + h + e + 1, :] * ((D_IDX ** -0.5) * (H_IDX ** -0.5))
            acc = acc + w * jnp.maximum(d[:, e * tq:(e + 1) * tq], 0.0)
    if limit is not None:
        kidx = lax.broadcasted_iota(jnp.int32, (tk, tq), 0)
        acc = jnp.where(kidx < limit, acc, _NEG_INF)
    return acc


def _kq_attend(bias_of, q, g, k_slab_of, vt_slab_of, store):
    tq = q.shape[0]
    for p in range(q.shape[1] // LANES):
        sl = slice(p * LANES, (p + 1) * LANES)
        q_slab = q[:, sl]
        q_pair = jnp.concatenate([_keep_lanes(q_slab, 0, D_HEAD), _keep_lanes(q_slab, D_HEAD, LANES)],
                                 axis=0)
        s_pair = lax.dot_general(k_slab_of(sl), q_pair, _NT, preferred_element_type=jnp.float32)
        probs, inv_l = [], []
        for hh in range(2):
            s = s_pair[:, hh * tq:(hh + 1) * tq]
            bias = bias_of(2 * p + hh)
            if bias is not None:
                s = s + bias
            m = jnp.max(s, axis=0, keepdims=True)
            pr = jnp.exp2(s - m)
            inv_l.append(1.0 / jnp.sum(pr, axis=0, keepdims=True))
            probs.append(pr.astype(jnp.bfloat16))
        o_pair = jnp.dot(vt_slab_of(sl), jnp.concatenate(probs, axis=1),
                         preferred_element_type=jnp.float32)
        ot = jnp.concatenate([o_pair[:D_HEAD, :tq] * inv_l[0], o_pair[D_HEAD:, tq:] * inv_l[1]], axis=0)
        store(sl, (ot.T * _silu(g[:, sl])).astype(jnp.bfloat16))


def _dsa_prompt_kernel(q_ref, qi_ref, kiwi_ref, g_ref, k_ref, v_ref, ki4_ref, o_ref, vt, sc_ref,
                       *, topk, tq):
    S = k_ref.shape[1]
    vt[...] = v_ref[0].astype(jnp.float32).T.astype(jnp.bfloat16)
    own_region = pl.when(pl.program_id(0) >= 0)
    probs, off = [], 0
    for j in range(S // tq):
        tk = (j + 1) * tq
        rows = slice(j * tq, (j + 1) * tq)

        @own_region
        def _(j=j, tk=tk, off=off, rows=rows):
            qpos = j * tq + lax.broadcasted_iota(jnp.int32, (1, tq), 1)
            limit = ((qpos >> 6) + 1) << 6
            sc_ref[off:off + tk, :] = _dsa_scores(limit, qi_ref[0, rows, :], kiwi_ref[0, rows, :],
                                                  ki4_ref[0, :tk, :])

        probs.append((off, tk))
        off += tk
    _dsa_select(sc_ref, probs, tq, topk)
    for j, (off, tk) in enumerate(probs):
        rows = slice(j * tq, (j + 1) * tq)

        @own_region
        def _(tk=tk, off=off, rows=rows):
            def store(sl, o):
                o_ref[0, rows, sl] = o

            _kq_attend(lambda h: sc_ref[off:off + tk, :], q_ref[0, rows, :], g_ref[0, rows, :],
                       lambda sl: k_ref[0, :tk, sl], lambda sl: vt[sl, :tk], store)


def _dsa_prompt_call(qb, qi, kiwi, gb, kb, vb, ki4):
    B, S, _ = qb.shape
    tq = DSA_TQ
    nq = S // tq
    assert CHUNK == 64 and nq * tq == S
    topk = min(TOPK_MAX, S // 4)
    full = lambda n: pl.BlockSpec((1, S, n), lambda b: (b, 0, 0))
    return pl.pallas_call(
        functools.partial(_dsa_prompt_kernel, topk=topk, tq=tq),
        grid=(B,),
        in_specs=[full(D_B), full(H_IDX * D_IDX), full(LANES), full(D_B),
                  full(D_B), full(D_B), full(LANES)],
        out_specs=full(D_B),
        out_shape=jax.ShapeDtypeStruct((B, S, D_B), jnp.bfloat16),
        scratch_shapes=[pltpu.VMEM((D_B, S), jnp.bfloat16),
                        pltpu.VMEM((tq * nq * (nq + 1) // 2, tq), jnp.float32)],
        compiler_params=_cparams(1),
        name="dsa_prompt",
    )(qb, qi, kiwi, gb, kb, vb, ki4)


def _dsa_sample_kernel(q_ref, qi_ref, kiwi_ref, g_ref, kc_ref, vc_ref, kic_ref,
                       kn_ref, vn_ref, kin_ref, o_ref, kbf, vt, ki4s, sc_ref, *, topk):
    P = kc_ref.shape[1]
    T = kn_ref.shape[1]
    tq = q_ref.shape[1]
    kbf[:P, :] = kc_ref[0].astype(jnp.bfloat16)
    kbf[P:, :] = kn_ref[0].astype(jnp.bfloat16)
    vt[:, :P] = vc_ref[0].T.astype(jnp.bfloat16)
    vn_tile = jnp.concatenate([vn_ref[0], jnp.zeros((LANES - T, D_B), jnp.float32)], axis=0)
    vt[:, P:] = vn_tile.T[:, :T].astype(jnp.bfloat16)
    ki4s[:P, :] = kic_ref[0]
    ki4s[P:, :] = kin_ref[0]
    sc_ref[...] = _dsa_scores(None, qi_ref[0], kiwi_ref[0], ki4s[...])
    _dsa_select(sc_ref, [(0, P + T)], tq, topk)

    def store(sl, o):
        o_ref[0, :, sl] = o

    _kq_attend(lambda h: sc_ref[...], q_ref[0], g_ref[0], lambda sl: kbf[:, sl], lambda sl: vt[sl, :],
               store)


def _dsa_sample_call(qb, qi, kiwi, gb, kc, vc, kic, kn, vn, kin):
    B, tq, _ = qb.shape
    P, T = kc.shape[1], kn.shape[1]
    tk = P + T
    topk = min(TOPK_MAX, tk // 4)
    blk = lambda r, n: pl.BlockSpec((1, r, n), lambda b: (b, 0, 0))
    return pl.pallas_call(
        functools.partial(_dsa_sample_kernel, topk=topk),
        grid=(B,),
        in_specs=[blk(tq, D_B), blk(tq, H_IDX * D_IDX), blk(tq, LANES), blk(tq, D_B),
                  blk(P, D_B), blk(P, D_B), blk(P, LANES),
                  blk(T, D_B), blk(T, D_B), blk(T, LANES)],
        out_specs=blk(tq, D_B),
        out_shape=jax.ShapeDtypeStruct((B, tq, D_B), jnp.bfloat16),
        scratch_shapes=[pltpu.VMEM((tk, D_B), jnp.bfloat16),
                        pltpu.VMEM((D_B, tk), jnp.bfloat16),
                        pltpu.VMEM((tk, LANES), jnp.bfloat16),
                        pltpu.VMEM((tk, tq), jnp.float32)],
        compiler_params=_cparams(1),
        name="dsa_sample",
    )(qb, qi, kiwi, gb, kc, vc, kic, kn, vn, kin)


def _merge_kernel(x_ref, oa_ref, ob_ref, om_ref, w_ref, g_ref, y_ref, *, last):
    acc = x_ref[...]
    acc = acc + jnp.dot(oa_ref[...], w_ref[:D_A, :], preferred_element_type=jnp.float32)
    acc = acc + jnp.dot(ob_ref[...], w_ref[D_A:D_A + D_B, :], preferred_element_type=jnp.float32)
    acc = acc + jnp.dot(om_ref[...], w_ref[D_A + D_B:, :], preferred_element_type=jnp.float32)
    y_ref[...] = _rms_scale(acc, g_ref[...]) if last else acc


def _merge_call(x2d, oa, ob, om, w, g, last):
    R, D = x2d.shape
    tm = min(PROJ_TM, R)
    assert R % tm == 0
    row = lambda n: pl.BlockSpec((tm, n), lambda i: (i, 0))
    return pl.pallas_call(
        functools.partial(_merge_kernel, last=last),
        grid=(R // tm,),
        in_specs=[row(D), row(D_A), row(D_B), row(D_M),
                  pl.BlockSpec((D_A + D_B + D_M, D), lambda i: (0, 0)),
                  pl.BlockSpec((1, D), lambda i: (0, 0))],
        out_specs=row(D),
        out_shape=jax.ShapeDtypeStruct((R, D), jnp.float32),
        compiler_params=_cparams(1),
        name="merge",
    )(x2d, oa, ob, om, w, g)


def _pack_w_in(w):
    D = w.shape[0]
    ki = w[:, C_KIWI:C_KIWI + D_IDX]
    wi = w[:, C_KIWI + D_IDX:C_KIWI + D_IDX + H_IDX]
    pad = jnp.zeros((D, LANES - D_IDX - H_IDX), w.dtype)
    packed = jnp.concatenate([w[:, :C_KIWI], ki, wi, pad, jnp.tile(ki, (1, LANES // D_IDX))], axis=1)
    return packed.astype(jnp.bfloat16)


def _rope_tables(pos):
    posf = pos.astype(jnp.float32)

    def tables(d):
        half = d // 2
        inv_freq = ROPE_THETA ** (-jnp.arange(half, dtype=jnp.float32) * 2.0 / d)
        ang = posf[:, None] * inv_freq[None, :]
        cos, sin = jnp.cos(ang), jnp.sin(ang)
        return jnp.concatenate([cos, cos], axis=1), jnp.concatenate([-sin, sin], axis=1)

    cos64, sin64 = tables(D_HEAD)
    cos32, sin32 = tables(D_IDX)
    n = pos.shape[0]
    rest = LANES - D_IDX
    cos_kw = jnp.concatenate([cos32, jnp.ones((n, rest), jnp.float32)], axis=1)
    sin_kw = jnp.concatenate([sin32, jnp.zeros((n, rest), jnp.float32)], axis=1)
    rep = lambda t, d: jnp.tile(t, (1, LANES // d))
    return jnp.stack([rep(cos64, D_HEAD), rep(sin64, D_HEAD), rep(cos32, D_IDX), rep(sin32, D_IDX),
                      cos_kw, sin_kw])


def _band_bias_vec(table):
    n_far = A_WINDOW - REL_CLIP + 1
    lo_idx = A_WINDOW + REL_CLIP - (BAND_WIN - 1)
    assert lo_idx >= 0 and BIAS_VEC >= BAND_WIN + BAND_TQ - 1
    far = table[:, 2 * REL_CLIP:2 * REL_CLIP + 1].astype(jnp.float32)
    mid = table[:, lo_idx:2 * REL_CLIP][:, ::-1].astype(jnp.float32)
    vec = jnp.concatenate([jnp.tile(far, (1, n_far)), mid, jnp.tile(far, (1, BIAS_VEC - BAND_WIN))], axis=1)
    rvec = jnp.concatenate([vec[:, :1], vec[:, 1:][:, ::-1]], axis=1)
    return vec, rvec


def kernel(x_prompt, x_sample, mem_prompt, cache_a_k, cache_a_v, cache_b_k, cache_b_v, cache_b_kidx,
           cache_mem_k, cache_mem_v, norm_mix_g, w_in, rel_bias_a, norm_mem_g, w_mem_kv, w_out,
           norm_final_g):
    B, S, D = x_prompt.shape
    Bs, T, _ = x_sample.shape
    depth = w_in.shape[0]
    P = cache_b_k.shape[2]
    Pa = cache_a_k.shape[2]
    n_mem = mem_prompt.shape[1]
    keep = min(A_WINDOW, S)
    assert Pa == A_WINDOW and T <= CHUNK and LANES % T == 0

    rope_p = _rope_tables(jnp.arange(S))
    rope_s = jnp.tile(_rope_tables(P + jnp.arange(T)), (1, Bs, 1))
    g_final = norm_final_g.reshape(1, D)

    xp, xs = x_prompt, x_sample
    outs_p = [[] for _ in range(7)]
    outs_s = [[] for _ in range(5)]
    for l in range(depth):
        last = l == depth - 1
        w = _pack_w_in(w_in[l])
        g_mix = norm_mix_g[l].reshape(1, D)
        w_o = w_out[l].astype(jnp.bfloat16)
        bias_vec, bias_rvec = _band_bias_vec(rel_bias_a[l])

        (qa, ka, va, ga, qb, kb, vb, gb, qm, gm, qi, kiwi, ki4, ki, kbb, vbb, ak, av) = _proj_call(
            xp, g_mix, w, rope_p, keep)
        mk, mv = _memkv_call(mem_prompt.reshape(B * n_mem, D), norm_mem_g[l].reshape(1, D),
                             w_mem_kv[l].astype(jnp.bfloat16))
        mk = mk.reshape(B, n_mem, D_M)
        mv = mv.reshape(B, n_mem, D_M)
        oa, om = _band_prompt_call(qa, ka, va, ga, qm, gm, mk, mv, bias_rvec)
        ob = _dsa_prompt_call(qb, qi, kiwi, gb, kbb, vbb, ki4)
        xp = _merge_call(xp.reshape(B * S, D), oa.reshape(B * S, D_A), ob.reshape(B * S, D_B),
                         om.reshape(B * S, D_M), w_o, g_final, last).reshape(B, S, D)
        for lst, t in zip(outs_p, (ak.reshape(B, keep, H_A, D_HEAD), av.reshape(B, keep, H_A, D_HEAD),
                                   kb.reshape(B, S, H_B, D_HEAD), vb.reshape(B, S, H_B, D_HEAD), ki,
                                   mk.reshape(B, n_mem, H_M, D_HEAD), mv.reshape(B, n_mem, H_M, D_HEAD))):
            lst.append(t)

        (qa, ka, va, ga, qb, kb, vb, gb, qm, gm, qi, kiwi, ki4, ki, kbb, vbb, ak, av) = _proj_call(
            xs.reshape(1, Bs * T, D), g_mix, w, rope_s, Bs * T)
        per_b = lambda t: t.reshape(Bs, T, t.shape[-1])
        oa, om = _band_sample_call(
            per_b(qa), cache_a_k[l].reshape(Bs, Pa, D_A), cache_a_v[l].reshape(Bs, Pa, D_A),
            per_b(ka), per_b(va), per_b(ga), per_b(qm), per_b(gm),
            cache_mem_k[l].reshape(Bs, n_mem, D_M), cache_mem_v[l].reshape(Bs, n_mem, D_M), bias_vec)
        rep = lambda t: jnp.tile(per_b(t), (1, LANES // T, 1))
        kic = jnp.tile(cache_b_kidx[l], (1, 1, LANES // D_IDX)).astype(jnp.bfloat16)
        ob = _dsa_sample_call(rep(qb), rep(qi), rep(kiwi), rep(gb),
                              cache_b_k[l].reshape(Bs, P, D_B), cache_b_v[l].reshape(Bs, P, D_B), kic,
                              per_b(kb), per_b(vb), per_b(ki4))[:, :T]
        xs = _merge_call(xs.reshape(Bs * T, D), oa.reshape(Bs * T, D_A), ob.reshape(Bs * T, D_B),
                         om.reshape(Bs * T, D_M), w_o, g_final, last).reshape(Bs, T, D)
        for lst, t in zip(outs_s, (ak.reshape(Bs, T, H_A, D_HEAD), av.reshape(Bs, T, H_A, D_HEAD),
                                   kb.reshape(Bs, T, H_B, D_HEAD), vb.reshape(Bs, T, H_B, D_HEAD),
                                   ki.reshape(Bs, T, D_IDX))):
            lst.append(t)

    st = lambda ts: jnp.stack(ts, axis=0)
    return (xp, xs) + tuple(st(t) for t in outs_p) + tuple(st(t) for t in outs_s)
```

```python
import functools
import math

import jax
import jax.numpy as jnp
from jax import lax
from jax.experimental import pallas as pl
from jax.experimental.pallas import tpu as pltpu

CHUNK = 64
D_HEAD = 64
H_A = 6
H_B = 6
H_M = 4
D_A = H_A * D_HEAD
D_B = H_B * D_HEAD
D_M = H_M * D_HEAD
A_LEFT_CHUNKS = 8
A_WINDOW = A_LEFT_CHUNKS * CHUNK
REL_CLIP = 256
H_IDX = 8
D_IDX = 32
TOPK_MAX = 256
ROPE_THETA = 10000.0
EPS = 1e-6

LANES = 128
VMEM_LIMIT_BYTES = 56 * 1024 * 1024

C_QA, C_KA, C_VA, C_GA = 0, D_A, 2 * D_A, 3 * D_A
C_QB = 4 * D_A
C_KB, C_VB, C_GB = C_QB + D_B, C_QB + 2 * D_B, C_QB + 3 * D_B
C_QM = C_QB + 4 * D_B
C_GM = C_QM + D_M
C_QI = C_GM + D_M
C_KIWI = C_QI + H_IDX * D_IDX
C_KI4 = C_KIWI + LANES
W_COLS = C_KI4 + LANES

PROJ_TM = 512
BAND_TQ = 256
BAND_WIN = A_WINDOW + BAND_TQ
DSA_TQ = 256

LOG2E = math.log2(math.e)
QK_SCALE = (D_HEAD ** -0.5) * LOG2E
BIAS_VEC = 8 * LANES

_NT = (((1,), (1,)), ((), ()))
_INT_MIN = -2 ** 31
_NEG_INF = float("-inf")


def _cparams(n_axes):
    return pltpu.CompilerParams(
        dimension_semantics=("arbitrary",) * n_axes,
        vmem_limit_bytes=VMEM_LIMIT_BYTES)


def _silu(g):
    return g * (1.0 / (1.0 + jnp.exp(-g)))


def _rms_scale(x, g):
    ms = jnp.mean(x * x, axis=-1, keepdims=True)
    return (x * lax.rsqrt(ms + EPS)) * g


def _rope_slab(z, cos, sin, half):
    lane = lax.broadcasted_iota(jnp.int32, z.shape, 1)
    first = (lane & (2 * half - 1)) < half
    partner = jnp.where(first, pltpu.roll(z, LANES - half, 1), pltpu.roll(z, half, 1))
    return z * cos + partner * sin


def _proj_kernel(x_ref, g_ref, w_ref, rope_ref,
                 qa_ref, ka_ref, va_ref, ga_ref, qb_ref, kb_ref, vb_ref, gb_ref,
                 qm_ref, gm_ref, qi_ref, kiwi_ref, ki4_ref, ki_ref, kbb_ref, vbb_ref, ak_ref, av_ref,
                 *, n_tiles, keep_tiles):
    i = pl.program_id(1)
    xn = _rms_scale(x_ref[0], g_ref[...]).astype(jnp.bfloat16)

    def proj(c0, n):
        return jnp.dot(xn, w_ref[:, c0:c0 + n], preferred_element_type=jnp.float32)

    scale = QK_SCALE
    z = proj(C_QA, 2 * D_A)
    qa_ref[0] = (z[:, :D_A] * scale).astype(jnp.bfloat16)
    ka = z[:, D_A:]
    ka_ref[0] = ka.astype(jnp.bfloat16)
    z = proj(C_VA, 2 * D_A)
    va = z[:, :D_A]
    va_ref[0] = va.astype(jnp.bfloat16)
    ga_ref[0] = z[:, D_A:]

    @pl.when(i >= n_tiles - keep_tiles)
    def _():
        ak_ref[0] = ka
        av_ref[0] = va

    cos64, sin64 = rope_ref[0], rope_ref[1]
    cos32, sin32 = rope_ref[2], rope_ref[3]
    cos_kw, sin_kw = rope_ref[4], rope_ref[5]
    z = proj(C_QB, 2 * D_B)
    for s in range(D_B // LANES):
        sl = slice(s * LANES, (s + 1) * LANES)
        qb = _rope_slab(z[:, sl], cos64, sin64, D_HEAD // 2)
        qb_ref[0, :, sl] = (qb * scale).astype(jnp.bfloat16)
        kb = _rope_slab(z[:, D_B + s * LANES:D_B + (s + 1) * LANES], cos64, sin64, D_HEAD // 2)
        kb_ref[0, :, sl] = kb
        kbb_ref[0, :, sl] = kb.astype(jnp.bfloat16)
    z = proj(C_VB, 2 * D_B)
    vb_ref[0] = z[:, :D_B]
    vbb_ref[0] = z[:, :D_B].astype(jnp.bfloat16)
    gb_ref[0] = z[:, D_B:]
    qm_ref[0] = (proj(C_QM, D_M) * scale).astype(jnp.bfloat16)
    gm_ref[0] = proj(C_GM, D_M)
    z = proj(C_QI, H_IDX * D_IDX)
    for s in range(H_IDX * D_IDX // LANES):
        sl = slice(s * LANES, (s + 1) * LANES)
        qi_ref[0, :, sl] = _rope_slab(z[:, sl], cos32, sin32, D_IDX // 2).astype(jnp.bfloat16)
    z = proj(C_KIWI, 2 * LANES)
    kiwi = _rope_slab(z[:, :LANES], cos_kw, sin_kw, D_IDX // 2)
    kiwi_ref[0] = kiwi
    ki_ref[0] = kiwi[:, :D_IDX]
    ki4_ref[0] = _rope_slab(z[:, LANES:], cos32, sin32, D_IDX // 2).astype(jnp.bfloat16)


def _proj_call(x, g, w, rope, keep_rows):
    B, S, D = x.shape
    tm = min(PROJ_TM, S)
    n_tiles = S // tm
    keep_tiles = keep_rows // tm
    assert n_tiles * tm == S and keep_tiles * tm == keep_rows

    def tile(n, dtype):
        return (jax.ShapeDtypeStruct((B, S, n), dtype),
                pl.BlockSpec((1, tm, n), lambda b, i: (b, i, 0)))

    f32, bf16 = jnp.float32, jnp.bfloat16
    outs = [tile(D_A, bf16), tile(D_A, bf16), tile(D_A, bf16), tile(D_A, f32),
            tile(D_B, bf16), tile(D_B, f32), tile(D_B, f32), tile(D_B, f32),
            tile(D_M, bf16), tile(D_M, f32), tile(H_IDX * D_IDX, bf16),
            tile(LANES, f32), tile(LANES, bf16), tile(D_IDX, f32), tile(D_B, bf16), tile(D_B, bf16)]
    keep_spec = pl.BlockSpec(
        (1, tm, D_A), lambda b, i: (b, jnp.maximum(i - (n_tiles - keep_tiles), 0), 0))
    outs += [(jax.ShapeDtypeStruct((B, keep_rows, D_A), f32), keep_spec)] * 2
    return pl.pallas_call(
        functools.partial(_proj_kernel, n_tiles=n_tiles, keep_tiles=keep_tiles),
        grid=(B, n_tiles),
        in_specs=[pl.BlockSpec((1, tm, D), lambda b, i: (b, i, 0)),
                  pl.BlockSpec((1, D), lambda b, i: (0, 0)),
                  pl.BlockSpec((D, W_COLS), lambda b, i: (0, 0)),
                  pl.BlockSpec((6, tm, LANES), lambda b, i: (0, i, 0))],
        out_specs=[o[1] for o in outs],
        out_shape=[o[0] for o in outs],
        compiler_params=_cparams(2),
        name="proj",
    )(x, g, w, rope)


def _memkv_kernel(m_ref, g_ref, w_ref, mk_ref, mv_ref):
    xn = _rms_scale(m_ref[...], g_ref[...]).astype(jnp.bfloat16)
    mk_ref[...] = jnp.dot(xn, w_ref[:, :D_M], preferred_element_type=jnp.float32)
    mv_ref[...] = jnp.dot(xn, w_ref[:, D_M:], preferred_element_type=jnp.float32)


def _memkv_call(mem2d, g, w):
    R, D = mem2d.shape
    tm = min(PROJ_TM, R)
    assert R % tm == 0
    row = lambda n: pl.BlockSpec((tm, n), lambda i: (i, 0))
    return pl.pallas_call(
        _memkv_kernel,
        grid=(R // tm,),
        in_specs=[row(D), pl.BlockSpec((1, D), lambda i: (0, 0)),
                  pl.BlockSpec((D, 2 * D_M), lambda i: (0, 0))],
        out_specs=[row(D_M), row(D_M)],
        out_shape=[jax.ShapeDtypeStruct((R, D_M), jnp.float32)] * 2,
        compiler_params=_cparams(1),
        name="memkv",
    )(mem2d, g, w)


def _softmax_rows_pv(s, v):
    m = jnp.max(s, axis=1, keepdims=True)
    p = jnp.exp2(s - m)
    l = jnp.sum(p, axis=1, keepdims=True)
    o = jnp.dot(p.astype(jnp.bfloat16), v, preferred_element_type=jnp.float32)
    return o * (1.0 / l)


def _keep_lanes(slab, lo, hi):
    x = slab.astype(jnp.float32)
    lane = lax.broadcasted_iota(jnp.int32, x.shape, 1)
    return jnp.where(lane >= lo, jnp.where(lane < hi, x, 0.0), 0.0).astype(jnp.bfloat16)


def _pair_attend(q_slab, k_slab, v_slab, bias_fn):
    lane = lax.broadcasted_iota(jnp.int32, q_slab.shape, 1)
    out = None
    for hh in range(2):
        s = lax.dot_general(_keep_lanes(q_slab, hh * D_HEAD, (hh + 1) * D_HEAD), k_slab, _NT,
                            preferred_element_type=jnp.float32)
        o = _softmax_rows_pv(bias_fn(s, hh), v_slab)
        out = o if hh == 0 else jnp.where(lane < D_HEAD, out, o)
    return out


def _mem_attend(qm_ref, gm_ref, mk_ref, mv_ref, om_ref):
    for p in range(D_M // LANES):
        sl = slice(p * LANES, (p + 1) * LANES)
        o = _pair_attend(qm_ref[0, :, sl], mk_ref[0, :, sl].astype(jnp.bfloat16),
                         mv_ref[0, :, sl].astype(jnp.bfloat16), lambda s, hh: s)
        om_ref[0, :, sl] = (o * _silu(gm_ref[0, :, sl])).astype(jnp.bfloat16)


def _build_band_bias(vec_ref, bias_ref, n_q, n_valid, chunked):
    n_keys = bias_ref.shape[2]
    i = lax.broadcasted_iota(jnp.int32, (n_q, n_keys), 0)
    r = lax.broadcasted_iota(jnp.int32, (n_q, n_keys), 1)
    lo = ((i >> 6) << 6) if chunked else jnp.zeros_like(i)
    for h in range(bias_ref.shape[0]):
        base = jnp.broadcast_to(vec_ref[h:h + 1, :], (n_q, BIAS_VEC))
        rolled = pltpu.roll(base, 0, 1, stride=1, stride_axis=0)
        b = rolled[:, :n_keys] * LOG2E
        bias_ref[h] = jnp.where(r >= lo, jnp.where(r < lo + n_valid, b, _NEG_INF), _NEG_INF)


def _build_band_bias_t(rvec_ref, bias_ref):
    n_keys, n_q = bias_ref.shape[1], bias_ref.shape[2]
    r = lax.broadcasted_iota(jnp.int32, (n_keys, n_q), 0)
    i = lax.broadcasted_iota(jnp.int32, (n_keys, n_q), 1)
    lo = (i >> 6) << 6
    for h in range(bias_ref.shape[0]):
        base = jnp.broadcast_to(rvec_ref[h:h + 1, :], (n_keys, BIAS_VEC))
        rolled = pltpu.roll(base, 0, 1, stride=1, stride_axis=0)
        b = rolled[:, :n_q] * LOG2E
        bias_ref[h] = jnp.where(r >= lo, jnp.where(r < lo + A_WINDOW + CHUNK, b, _NEG_INF), _NEG_INF)


def _band_prompt_kernel(qa_ref, ka_ref, va_ref, ga_ref, qm_ref, gm_ref, mk_ref, mv_ref, rvec_ref,
                        oa_ref, om_ref, kpad, vt_blk, mk_bf, mv_t, bias_ref):
    j = pl.program_id(1)
    n_front = A_WINDOW // BAND_TQ
    n_win = BAND_WIN // BAND_TQ

    @pl.when(jnp.logical_and(pl.program_id(0) == 0, j == 0))
    def _():
        _build_band_bias_t(rvec_ref, bias_ref)

    @pl.when(j == 0)
    def _():
        kpad[:A_WINDOW, :] = jnp.zeros((A_WINDOW, D_A), jnp.bfloat16)
        kpad[A_WINDOW:, :] = ka_ref[0]
        for t in range(vt_blk.shape[0]):
            if t < n_front:
                vt_blk[t] = jnp.zeros((D_A, BAND_TQ), jnp.bfloat16)
            else:
                rows = slice((t - n_front) * BAND_TQ, (t - n_front + 1) * BAND_TQ)
                vt_blk[t] = va_ref[0, rows, :].astype(jnp.float32).T.astype(jnp.bfloat16)
        mk_bf[...] = mk_ref[0].astype(jnp.bfloat16)
        mv_t[...] = mv_ref[0].T.astype(jnp.bfloat16)

    start = pl.multiple_of(j * BAND_TQ, BAND_TQ)

    def store_a(sl, o):
        oa_ref[0, :, sl] = o

    def store_m(sl, o):
        om_ref[0, :, sl] = o

    def run(mask_front):
        if mask_front:
            pos = lax.broadcasted_iota(jnp.int32, (BAND_WIN, BAND_TQ), 0) + j * BAND_TQ
            front = jnp.where(pos >= A_WINDOW, 0.0, _NEG_INF)
        _kq_attend(lambda h: bias_ref[h] + front if mask_front else bias_ref[h],
                   qa_ref[0], ga_ref[0],
                   lambda sl: kpad[pl.ds(start, BAND_WIN), sl],
                   lambda sl: jnp.concatenate([vt_blk[j + u, sl, :] for u in range(n_win)], axis=1),
                   store_a)

    pl.when(j < n_front)(lambda: run(True))
    pl.when(j >= n_front)(lambda: run(False))
    _kq_attend(lambda h: None, qm_ref[0], gm_ref[0], lambda sl: mk_bf[:, sl], lambda sl: mv_t[sl, :],
               store_m)


def _band_prompt_call(qa, ka, va, ga, qm, gm, mk, mv, bias_rvec):
    B, S, _ = qa.shape
    nq = S // BAND_TQ
    n_mem = mk.shape[1]
    assert nq * BAND_TQ == S and A_WINDOW % BAND_TQ == 0
    qblk = lambda n: pl.BlockSpec((1, BAND_TQ, n), lambda b, j: (b, j, 0))
    full = lambda r, n: pl.BlockSpec((1, r, n), lambda b, j: (b, 0, 0))
    return pl.pallas_call(
        _band_prompt_kernel,
        grid=(B, nq),
        in_specs=[qblk(D_A), full(S, D_A), full(S, D_A), qblk(D_A), qblk(D_M), qblk(D_M),
                  full(n_mem, D_M), full(n_mem, D_M),
                  pl.BlockSpec((H_A, BIAS_VEC), lambda b, j: (0, 0))],
        out_specs=[qblk(D_A), qblk(D_M)],
        out_shape=[jax.ShapeDtypeStruct((B, S, D_A), jnp.bfloat16),
                   jax.ShapeDtypeStruct((B, S, D_M), jnp.bfloat16)],
        scratch_shapes=[pltpu.VMEM((S + A_WINDOW, D_A), jnp.bfloat16),
                        pltpu.VMEM((nq + A_WINDOW // BAND_TQ, D_A, BAND_TQ), jnp.bfloat16),
                        pltpu.VMEM((n_mem, D_M), jnp.bfloat16),
                        pltpu.VMEM((D_M, n_mem), jnp.bfloat16),
                        pltpu.VMEM((H_A, BAND_WIN, BAND_TQ), jnp.float32)],
        compiler_params=_cparams(2),
        name="band_prompt",
    )(qa, ka, va, ga, qm, gm, mk, mv, bias_rvec)


def _band_sample_kernel(qa_ref, kc_ref, vc_ref, kn_ref, vn_ref, ga_ref, qm_ref, gm_ref,
                        mk_ref, mv_ref, vec_ref, oa_ref, om_ref, kcat, vcat, bias_ref):
    P = kc_ref.shape[1]
    T = kn_ref.shape[1]
    pad = kcat.shape[0] - P - T

    @pl.when(pl.program_id(0) == 0)
    def _():
        _build_band_bias(vec_ref, bias_ref, T, P + T, False)

    kcat[:P, :] = kc_ref[0].astype(jnp.bfloat16)
    vcat[:P, :] = vc_ref[0].astype(jnp.bfloat16)
    kcat[P:P + T, :] = kn_ref[0]
    vcat[P:P + T, :] = vn_ref[0]
    zeros = jnp.zeros((pad, D_A), jnp.bfloat16)
    kcat[P + T:, :] = zeros
    vcat[P + T:, :] = zeros
    for p in range(D_A // LANES):
        sl = slice(p * LANES, (p + 1) * LANES)
        o = _pair_attend(qa_ref[0, :, sl], kcat[:, sl], vcat[:, sl],
                         lambda s, hh, p=p: s + bias_ref[2 * p + hh])
        oa_ref[0, :, sl] = (o * _silu(ga_ref[0, :, sl])).astype(jnp.bfloat16)
    _mem_attend(qm_ref, gm_ref, mk_ref, mv_ref, om_ref)


def _band_sample_call(qa, kc, vc, kn, vn, ga, qm, gm, mk, mv, bias_vec):
    B, T, _ = qa.shape
    P = kc.shape[1]
    n_keys = -(-(P + T) // LANES) * LANES
    n_mem = mk.shape[1]
    blk = lambda r, n: pl.BlockSpec((1, r, n), lambda b: (b, 0, 0))
    return pl.pallas_call(
        _band_sample_kernel,
        grid=(B,),
        in_specs=[blk(T, D_A), blk(P, D_A), blk(P, D_A), blk(T, D_A), blk(T, D_A), blk(T, D_A),
                  blk(T, D_M), blk(T, D_M), blk(n_mem, D_M), blk(n_mem, D_M),
                  pl.BlockSpec((H_A, BIAS_VEC), lambda b: (0, 0))],
        out_specs=[blk(T, D_A), blk(T, D_M)],
        out_shape=[jax.ShapeDtypeStruct((B, T, D_A), jnp.bfloat16),
                   jax.ShapeDtypeStruct((B, T, D_M), jnp.bfloat16)],
        scratch_shapes=[pltpu.VMEM((n_keys, D_A), jnp.bfloat16)] * 2
        + [pltpu.VMEM((H_A, T, n_keys), jnp.float32)],
        compiler_params=_cparams(1),
        name="band_sample",
    )(qa, kc, vc, kn, vn, ga, qm, gm, mk, mv, bias_vec)


def _key_to_f32(k):
    bits = k ^ ((k >> 31) & jnp.int32(0x7FFFFFFF))
    return lax.bitcast_convert_type(bits, jnp.float32)


def _count(pred_f32):
    ones = jnp.ones((8, pred_f32.shape[0]), jnp.bfloat16)
    c = jnp.dot(ones, pred_f32.astype(jnp.bfloat16), preferred_element_type=jnp.float32)
    return c[0:1, :]


def _dsa_select(sc_ref, probs, tq, topk):
    one, zero = jnp.float32(1.0), jnp.float32(0.0)
    live = [(off, tk) for off, tk in probs if tk > topk]
    n_live = len(live)

    def sc_of(n):
        off, tk = live[n]
        return sc_ref[off:off + tk, :]

    if live:
        def thr_step(i, ts):
            bit = lax.shift_left(jnp.int32(1), 31 - i)
            out = []
            for n in range(n_live):
                cand = ts[n] + bit
                c = _count(jnp.where(sc_of(n) >= _key_to_f32(cand), one, zero))
                out.append(jnp.where(c >= topk, cand, ts[n]))
            return tuple(out)

        start = tuple(jnp.full((1, tq), _INT_MIN, jnp.int32) for _ in range(n_live))
        t_keys = lax.fori_loop(0, 32, thr_step, start)
        thrs = [jnp.where(t == _INT_MIN, _NEG_INF, _key_to_f32(t)) for t in t_keys]

        jmaxs = []
        for n in range(n_live):
            tk = live[n][1]
            sc = sc_of(n)
            c_gt = _count(jnp.where(sc > thrs[n], one, zero))
            c_ge = _count(jnp.where(sc >= thrs[n], one, zero))
            need = topk - c_gt

            def tie_search(n=n, tk=tk, need=need):
                off = live[n][0]
                tile = 2 * LANES

                def step(first, size, state):
                    before, jmax = state
                    r = lax.broadcasted_iota(jnp.int32, (size, size), 0)
                    c = lax.broadcasted_iota(jnp.int32, (size, size), 1)
                    lower = jnp.where(c <= r, one, zero).astype(jnp.bfloat16)
                    tied = jnp.where(sc_ref[pl.ds(off + first, size), :] == thrs[n], one, zero)
                    rank = before + jnp.dot(lower, tied.astype(jnp.bfloat16),
                                            preferred_element_type=jnp.float32)
                    kidx = (lax.broadcasted_iota(jnp.int32, (size, tq), 0) + first).astype(jnp.float32)
                    taken = jnp.where(tied > 0, jnp.where(rank <= need, kidx, -one), -one)
                    jmax = jnp.maximum(jmax, jnp.max(taken, axis=0, keepdims=True))
                    return rank[size - 1:size, :], jmax

                state = (jnp.zeros((1, tq), jnp.float32), jnp.full((1, tq), -1.0, jnp.float32))
                n_full, rest = divmod(tk, tile)
                state = lax.fori_loop(
                    0, n_full, lambda t, st: step(pl.multiple_of(t * tile, tile), tile, st), state)
                if rest:
                    state = step(n_full * tile, rest, state)
                return state[1].astype(jnp.int32)

            has_excess = jnp.max(jnp.where(c_ge > topk, one, zero)) > 0
            jmaxs.append(lax.cond(has_excess, tie_search,
                                  lambda tk=tk: jnp.full((1, tq), tk, jnp.int32)))

    for off, tk in probs:
        sc = sc_ref[off:off + tk, :]
        keep = jnp.where(jnp.abs(sc) < jnp.float32(jnp.inf), zero, _NEG_INF)
        if tk > topk:
            n = live.index((off, tk))
            kidx = lax.broadcasted_iota(jnp.int32, (tk, tq), 0)
            keep = jnp.where(sc > thrs[n], keep,
                             jnp.where(sc == thrs[n], jnp.where(kidx <= jmaxs[n], keep, _NEG_INF),
                                       _NEG_INF))
        sc_ref[off:off + tk, :] = keep


def _dsa_scores(limit, qi, kiwi, ki4):
    tk, tq = ki4.shape[0], qi.shape[0]
    kiwi_t = kiwi.T
    acc = jnp.zeros((tk, tq), jnp.float32)
    heads_per_slab = LANES // D_IDX

    def head_q(h):
        slab = qi[:, (h // heads_per_slab) * LANES:(h // heads_per_slab + 1) * LANES]
        lo = (h % heads_per_slab) * D_IDX
        return _keep_lanes(slab, lo, lo + D_IDX)

    for h in range(0, H_IDX, 2):
        d = lax.dot_general(ki4, jnp.concatenate([head_q(h), head_q(h + 1)], axis=0), _NT,
                            preferred_element_type=jnp.float32)
        for e in range(2):
            w = kiwi_t[D_IDX + h + e:D_IDX + h + e + 1, :] * ((D_IDX ** -0.5) * (H_IDX ** -0.5))
            acc = acc + w * jnp.maximum(d[:, e * tq:(e + 1) * tq], 0.0)
    if limit is not None:
        kidx = lax.broadcasted_iota(jnp.int32, (tk, tq), 0)
        acc = jnp.where(kidx < limit, acc, _NEG_INF)
    return acc


def _kq_attend(bias_of, q, g, k_slab_of, vt_slab_of, store):
    tq = q.shape[0]
    for p in range(q.shape[1] // LANES):
        sl = slice(p * LANES, (p + 1) * LANES)
        q_slab = q[:, sl]
        q_pair = jnp.concatenate([_keep_lanes(q_slab, 0, D_HEAD), _keep_lanes(q_slab, D_HEAD, LANES)],
                                 axis=0)
        s_pair = lax.dot_general(k_slab_of(sl), q_pair, _NT, preferred_element_type=jnp.float32)
        probs, inv_l = [], []
        for hh in range(2):
            s = s_pair[:, hh * tq:(hh + 1) * tq]
            bias = bias_of(2 * p + hh)
            if bias is not None:
                s = s + bias
            m = jnp.max(s, axis=0, keepdims=True)
            pr = jnp.exp2(s - m)
            inv_l.append(1.0 / jnp.sum(pr, axis=0, keepdims=True))
            probs.append(pr.astype(jnp.bfloat16))
        o_pair = jnp.dot(vt_slab_of(sl), jnp.concatenate(probs, axis=1),
                         preferred_element_type=jnp.float32)
        ot = jnp.concatenate([o_pair[:D_HEAD, :tq] * inv_l[0], o_pair[D_HEAD:, tq:] * inv_l[1]], axis=0)
        store(sl, (ot.T * _silu(g[:, sl])).astype(jnp.bfloat16))


def _dsa_prompt_kernel(q_ref, qi_ref, kiwi_ref, g_ref, k_ref, v_ref, ki4_ref, o_ref, vt, sc_ref,
                       *, topk, tq):
    S = k_ref.shape[1]
    nq = S // tq
    for t in range(nq):
        vt[t] = v_ref[0, t * tq:(t + 1) * tq, :].astype(jnp.float32).T.astype(jnp.bfloat16)
    own_region = pl.when(pl.program_id(0) >= 0)
    probs, off = [], 0
    for j in range(nq):
        tk = (j + 1) * tq
        rows = slice(j * tq, (j + 1) * tq)

        @own_region
        def _(j=j, tk=tk, off=off, rows=rows):
            qpos = j * tq + lax.broadcasted_iota(jnp.int32, (1, tq), 1)
            limit = ((qpos >> 6) + 1) << 6
            sc_ref[off:off + tk, :] = _dsa_scores(limit, qi_ref[0, rows, :], kiwi_ref[0, rows, :],
                                                  ki4_ref[0, :tk, :])

        probs.append((off, tk))
        off += tk
    _dsa_select(sc_ref, probs, tq, topk)

    def block_body(j, carry):
        rows = pl.ds(pl.multiple_of(j * tq, tq), tq)
        base = (j * (j + 1) // 2) * tq
        for p in range(D_B // LANES):
            sl = slice(p * LANES, (p + 1) * LANES)
            q_slab = q_ref[0, rows, sl]
            q_pair = jnp.concatenate(
                [_keep_lanes(q_slab, 0, D_HEAD), _keep_lanes(q_slab, D_HEAD, LANES)], axis=0)

            def tile_body(kt, state, sl=sl, q_pair=q_pair):
                m, l, acc = state
                krows = pl.ds(pl.multiple_of(kt * tq, tq), tq)
                s = lax.dot_general(k_ref[0, krows, sl], q_pair, _NT,
                                    preferred_element_type=jnp.float32)
                bias = sc_ref[pl.ds(pl.multiple_of(base + kt * tq, tq), tq), :]
                s = s + jnp.concatenate([bias, bias], axis=1)
                m_new = jnp.maximum(m, jnp.max(s, axis=0, keepdims=True))
                m_ref = jnp.where(m_new == _NEG_INF, 0.0, m_new)
                pr = jnp.exp2(s - m_ref)
                alpha = jnp.exp2(m - m_ref)
                l = alpha * l + jnp.sum(pr, axis=0, keepdims=True)
                acc = alpha * acc + jnp.dot(vt[kt, sl, :], pr.astype(jnp.bfloat16),
                                            preferred_element_type=jnp.float32)
                return m_new, l, acc

            init = (jnp.full((1, 2 * tq), _NEG_INF, jnp.float32), jnp.zeros((1, 2 * tq), jnp.float32),
                    jnp.zeros((LANES, 2 * tq), jnp.float32))
            _, l, acc = lax.fori_loop(0, j + 1, tile_body, init)
            inv_l = 1.0 / l
            ot = jnp.concatenate([acc[:D_HEAD, :tq] * inv_l[:, :tq], acc[D_HEAD:, tq:] * inv_l[:, tq:]],
                                 axis=0)
            o_ref[0, rows, sl] = (ot.T * _silu(g_ref[0, rows, sl])).astype(jnp.bfloat16)
        return carry

    lax.fori_loop(0, nq, block_body, 0)


def _dsa_prompt_call(qb, qi, kiwi, gb, kb, vb, ki4):
    B, S, _ = qb.shape
    tq = DSA_TQ
    nq = S // tq
    assert CHUNK == 64 and nq * tq == S
    topk = min(TOPK_MAX, S // 4)
    full = lambda n: pl.BlockSpec((1, S, n), lambda b: (b, 0, 0))
    return pl.pallas_call(
        functools.partial(_dsa_prompt_kernel, topk=topk, tq=tq),
        grid=(B,),
        in_specs=[full(D_B), full(H_IDX * D_IDX), full(LANES), full(D_B),
                  full(D_B), full(D_B), full(LANES)],
        out_specs=full(D_B),
        out_shape=jax.ShapeDtypeStruct((B, S, D_B), jnp.bfloat16),
        scratch_shapes=[pltpu.VMEM((nq, D_B, tq), jnp.bfloat16),
                        pltpu.VMEM((tq * nq * (nq + 1) // 2, tq), jnp.float32)],
        compiler_params=_cparams(1),
        name="dsa_prompt",
    )(qb, qi, kiwi, gb, kb, vb, ki4)


def _dsa_sample_kernel(q_ref, qi_ref, kiwi_ref, g_ref, kc_ref, vc_ref, kic_ref,
                       kn_ref, vn_ref, kin_ref, o_ref, kbf, vt, ki4s, sc_ref, *, topk):
    P = kc_ref.shape[1]
    T = kn_ref.shape[1]
    tq = q_ref.shape[1]
    kbf[:P, :] = kc_ref[0].astype(jnp.bfloat16)
    kbf[P:, :] = kn_ref[0].astype(jnp.bfloat16)
    vt[:, :P] = vc_ref[0].T.astype(jnp.bfloat16)
    vn_tile = jnp.concatenate([vn_ref[0], jnp.zeros((LANES - T, D_B), jnp.float32)], axis=0)
    vt[:, P:] = vn_tile.T[:, :T].astype(jnp.bfloat16)
    ki4s[:P, :] = kic_ref[0]
    ki4s[P:, :] = kin_ref[0]
    sc_ref[...] = _dsa_scores(None, qi_ref[0], kiwi_ref[0], ki4s[...])
    _dsa_select(sc_ref, [(0, P + T)], tq, topk)

    def store(sl, o):
        o_ref[0, :, sl] = o

    _kq_attend(lambda h: sc_ref[...], q_ref[0], g_ref[0], lambda sl: kbf[:, sl], lambda sl: vt[sl, :],
               store)


def _dsa_sample_call(qb, qi, kiwi, gb, kc, vc, kic, kn, vn, kin):
    B, tq, _ = qb.shape
    P, T = kc.shape[1], kn.shape[1]
    tk = P + T
    topk = min(TOPK_MAX, tk // 4)
    blk = lambda r, n: pl.BlockSpec((1, r, n), lambda b: (b, 0, 0))
    return pl.pallas_call(
        functools.partial(_dsa_sample_kernel, topk=topk),
        grid=(B,),
        in_specs=[blk(tq, D_B), blk(tq, H_IDX * D_IDX), blk(tq, LANES), blk(tq, D_B),
                  blk(P, D_B), blk(P, D_B), blk(P, LANES),
                  blk(T, D_B), blk(T, D_B), blk(T, LANES)],
        out_specs=blk(tq, D_B),
        out_shape=jax.ShapeDtypeStruct((B, tq, D_B), jnp.bfloat16),
        scratch_shapes=[pltpu.VMEM((tk, D_B), jnp.bfloat16),
                        pltpu.VMEM((D_B, tk), jnp.bfloat16),
                        pltpu.VMEM((tk, LANES), jnp.bfloat16),
                        pltpu.VMEM((tk, tq), jnp.float32)],
        compiler_params=_cparams(1),
        name="dsa_sample",
    )(qb, qi, kiwi, gb, kc, vc, kic, kn, vn, kin)


def _merge_kernel(x_ref, oa_ref, ob_ref, om_ref, w_ref, g_ref, y_ref, *, last):
    acc = x_ref[...]
    acc = acc + jnp.dot(oa_ref[...], w_ref[:D_A, :], preferred_element_type=jnp.float32)
    acc = acc + jnp.dot(ob_ref[...], w_ref[D_A:D_A + D_B, :], preferred_element_type=jnp.float32)
    acc = acc + jnp.dot(om_ref[...], w_ref[D_A + D_B:, :], preferred_element_type=jnp.float32)
    y_ref[...] = _rms_scale(acc, g_ref[...]) if last else acc


def _merge_call(x2d, oa, ob, om, w, g, last):
    R, D = x2d.shape
    tm = min(PROJ_TM, R)
    assert R % tm == 0
    row = lambda n: pl.BlockSpec((tm, n), lambda i: (i, 0))
    return pl.pallas_call(
        functools.partial(_merge_kernel, last=last),
        grid=(R // tm,),
        in_specs=[row(D), row(D_A), row(D_B), row(D_M),
                  pl.BlockSpec((D_A + D_B + D_M, D), lambda i: (0, 0)),
                  pl.BlockSpec((1, D), lambda i: (0, 0))],
        out_specs=row(D),
        out_shape=jax.ShapeDtypeStruct((R, D), jnp.float32),
        compiler_params=_cparams(1),
        name="merge",
    )(x2d, oa, ob, om, w, g)


def _pack_w_in(w):
    D = w.shape[0]
    ki = w[:, C_KIWI:C_KIWI + D_IDX]
    wi = w[:, C_KIWI + D_IDX:C_KIWI + D_IDX + H_IDX]
    pad = jnp.zeros((D, LANES - D_IDX - H_IDX), w.dtype)
    packed = jnp.concatenate([w[:, :C_KIWI], ki, wi, pad, jnp.tile(ki, (1, LANES // D_IDX))], axis=1)
    return packed.astype(jnp.bfloat16)


def _rope_tables(pos):
    posf = pos.astype(jnp.float32)

    def tables(d):
        half = d // 2
        inv_freq = ROPE_THETA ** (-jnp.arange(half, dtype=jnp.float32) * 2.0 / d)
        ang = posf[:, None] * inv_freq[None, :]
        cos, sin = jnp.cos(ang), jnp.sin(ang)
        return jnp.concatenate([cos, cos], axis=1), jnp.concatenate([-sin, sin], axis=1)

    cos64, sin64 = tables(D_HEAD)
    cos32, sin32 = tables(D_IDX)
    n = pos.shape[0]
    rest = LANES - D_IDX
    cos_kw = jnp.concatenate([cos32, jnp.ones((n, rest), jnp.float32)], axis=1)
    sin_kw = jnp.concatenate([sin32, jnp.zeros((n, rest), jnp.float32)], axis=1)
    rep = lambda t, d: jnp.tile(t, (1, LANES // d))
    return jnp.stack([rep(cos64, D_HEAD), rep(sin64, D_HEAD), rep(cos32, D_IDX), rep(sin32, D_IDX),
                      cos_kw, sin_kw])


def _band_bias_vec(table):
    n_far = A_WINDOW - REL_CLIP + 1
    lo_idx = A_WINDOW + REL_CLIP - (BAND_WIN - 1)
    assert lo_idx >= 0 and BIAS_VEC >= BAND_WIN + BAND_TQ - 1
    far = table[:, 2 * REL_CLIP:2 * REL_CLIP + 1].astype(jnp.float32)
    mid = table[:, lo_idx:2 * REL_CLIP][:, ::-1].astype(jnp.float32)
    vec = jnp.concatenate([jnp.tile(far, (1, n_far)), mid, jnp.tile(far, (1, BIAS_VEC - BAND_WIN))], axis=1)
    rvec = jnp.concatenate([vec[:, :1], vec[:, 1:][:, ::-1]], axis=1)
    return vec, rvec


def kernel(x_prompt, x_sample, mem_prompt, cache_a_k, cache_a_v, cache_b_k, cache_b_v, cache_b_kidx,
           cache_mem_k, cache_mem_v, norm_mix_g, w_in, rel_bias_a, norm_mem_g, w_mem_kv, w_out,
           norm_final_g):
    B, S, D = x_prompt.shape
    Bs, T, _ = x_sample.shape
    depth = w_in.shape[0]
    P = cache_b_k.shape[2]
    Pa = cache_a_k.shape[2]
    n_mem = mem_prompt.shape[1]
    keep = min(A_WINDOW, S)
    assert Pa == A_WINDOW and T <= CHUNK and LANES % T == 0

    rope_p = _rope_tables(jnp.arange(S))
    rope_s = jnp.tile(_rope_tables(P + jnp.arange(T)), (1, Bs, 1))
    g_final = norm_final_g.reshape(1, D)

    xp, xs = x_prompt, x_sample
    outs_p = [[] for _ in range(7)]
    outs_s = [[] for _ in range(5)]
    for l in range(depth):
        last = l == depth - 1
        w = _pack_w_in(w_in[l])
        g_mix = norm_mix_g[l].reshape(1, D)
        w_o = w_out[l].astype(jnp.bfloat16)
        bias_vec, bias_rvec = _band_bias_vec(rel_bias_a[l])

        (qa, ka, va, ga, qb, kb, vb, gb, qm, gm, qi, kiwi, ki4, ki, kbb, vbb, ak, av) = _proj_call(
            xp, g_mix, w, rope_p, keep)
        mk, mv = _memkv_call(mem_prompt.reshape(B * n_mem, D), norm_mem_g[l].reshape(1, D),
                             w_mem_kv[l].astype(jnp.bfloat16))
        mk = mk.reshape(B, n_mem, D_M)
        mv = mv.reshape(B, n_mem, D_M)
        oa, om = _band_prompt_call(qa, ka, va, ga, qm, gm, mk, mv, bias_rvec)
        ob = _dsa_prompt_call(qb, qi, kiwi, gb, kbb, vbb, ki4)
        xp = _merge_call(xp.reshape(B * S, D), oa.reshape(B * S, D_A), ob.reshape(B * S, D_B),
                         om.reshape(B * S, D_M), w_o, g_final, last).reshape(B, S, D)
        for lst, t in zip(outs_p, (ak.reshape(B, keep, H_A, D_HEAD), av.reshape(B, keep, H_A, D_HEAD),
                                   kb.reshape(B, S, H_B, D_HEAD), vb.reshape(B, S, H_B, D_HEAD), ki,
                                   mk.reshape(B, n_mem, H_M, D_HEAD), mv.reshape(B, n_mem, H_M, D_HEAD))):
            lst.append(t)

        (qa, ka, va, ga, qb, kb, vb, gb, qm, gm, qi, kiwi, ki4, ki, kbb, vbb, ak, av) = _proj_call(
            xs.reshape(1, Bs * T, D), g_mix, w, rope_s, Bs * T)
        per_b = lambda t: t.reshape(Bs, T, t.shape[-1])
        oa, om = _band_sample_call(
            per_b(qa), cache_a_k[l].reshape(Bs, Pa, D_A), cache_a_v[l].reshape(Bs, Pa, D_A),
            per_b(ka), per_b(va), per_b(ga), per_b(qm), per_b(gm),
            cache_mem_k[l].reshape(Bs, n_mem, D_M), cache_mem_v[l].reshape(Bs, n_mem, D_M), bias_vec)
        rep = lambda t: jnp.tile(per_b(t), (1, LANES // T, 1))
        kic = jnp.tile(cache_b_kidx[l], (1, 1, LANES // D_IDX)).astype(jnp.bfloat16)
        ob = _dsa_sample_call(rep(qb), rep(qi), rep(kiwi), rep(gb),
                              cache_b_k[l].reshape(Bs, P, D_B), cache_b_v[l].reshape(Bs, P, D_B), kic,
                              per_b(kb), per_b(vb), per_b(ki4))[:, :T]
        xs = _merge_call(xs.reshape(Bs * T, D), oa.reshape(Bs * T, D_A), ob.reshape(Bs * T, D_B),
                         om.reshape(Bs * T, D_M), w_o, g_final, last).reshape(Bs, T, D)
        for lst, t in zip(outs_s, (ak.reshape(Bs, T, H_A, D_HEAD), av.reshape(Bs, T, H_A, D_HEAD),
                                   kb.reshape(Bs, T, H_B, D_HEAD), vb.reshape(Bs, T, H_B, D_HEAD),
                                   ki.reshape(Bs, T, D_IDX))):
            lst.append(t)

    st = lambda ts: jnp.stack(ts, axis=0)
    return (xp, xs) + tuple(st(t) for t in outs_p) + tuple(st(t) for t in outs_s)
```

```python
import functools
import math

import jax
import jax.numpy as jnp
from jax import lax
from jax.experimental import pallas as pl
from jax.experimental.pallas import tpu as pltpu

CHUNK = 64
D_HEAD = 64
H_A = 6
H_B = 6
H_M = 4
D_A = H_A * D_HEAD
D_B = H_B * D_HEAD
D_M = H_M * D_HEAD
A_LEFT_CHUNKS = 8
A_WINDOW = A_LEFT_CHUNKS * CHUNK
REL_CLIP = 256
H_IDX = 8
D_IDX = 32
TOPK_MAX = 256
ROPE_THETA = 10000.0
EPS = 1e-6

LANES = 128
VMEM_LIMIT_BYTES = 56 * 1024 * 1024

C_QA, C_KA, C_VA, C_GA = 0, D_A, 2 * D_A, 3 * D_A
C_QB = 4 * D_A
C_KB, C_VB, C_GB = C_QB + D_B, C_QB + 2 * D_B, C_QB + 3 * D_B
C_QM = C_QB + 4 * D_B
C_GM = C_QM + D_M
C_QI = C_GM + D_M
C_KIWI = C_QI + H_IDX * D_IDX
C_KI4 = C_KIWI + LANES
W_COLS = C_KI4 + LANES

PROJ_TM = 512
BAND_TQ = 256
BAND_WIN = A_WINDOW + BAND_TQ
DSA_TQ = 256

LOG2E = math.log2(math.e)
QK_SCALE = (D_HEAD ** -0.5) * LOG2E
BIAS_VEC = 8 * LANES

_NT = (((1,), (1,)), ((), ()))
_INT_MIN = -2 ** 31
_NEG_INF = float("-inf")


def _cparams(n_axes):
    return pltpu.CompilerParams(
        dimension_semantics=("arbitrary",) * n_axes,
        vmem_limit_bytes=VMEM_LIMIT_BYTES)


def _silu(g):
    return g * (1.0 / (1.0 + jnp.exp(-g)))


def _rms_scale(x, g):
    ms = jnp.mean(x * x, axis=-1, keepdims=True)
    return (x * lax.rsqrt(ms + EPS)) * g


def _rope_slab(z, cos, sin, half):
    lane = lax.broadcasted_iota(jnp.int32, z.shape, 1)
    first = (lane & (2 * half - 1)) < half
    partner = jnp.where(first, pltpu.roll(z, LANES - half, 1), pltpu.roll(z, half, 1))
    return z * cos + partner * sin


def _proj_kernel(x_ref, g_ref, w_ref, rope_ref,
                 qa_ref, ka_ref, va_ref, ga_ref, qb_ref, kb_ref, vb_ref, gb_ref,
                 qm_ref, gm_ref, qi_ref, kiwi_ref, ki4_ref, ki_ref, kbb_ref, vbb_ref, ak_ref, av_ref,
                 *, n_tiles, keep_tiles):
    i = pl.program_id(1)
    xn = _rms_scale(x_ref[0], g_ref[...]).astype(jnp.bfloat16)

    def proj(c0, n):
        return jnp.dot(xn, w_ref[:, c0:c0 + n], preferred_element_type=jnp.float32)

    scale = QK_SCALE
    z = proj(C_QA, 2 * D_A)
    qa_ref[0] = (z[:, :D_A] * scale).astype(jnp.bfloat16)
    ka = z[:, D_A:]
    ka_ref[0] = ka.astype(jnp.bfloat16)
    z = proj(C_VA, 2 * D_A)
    va = z[:, :D_A]
    va_ref[0] = va.astype(jnp.bfloat16)
    ga_ref[0] = z[:, D_A:]

    @pl.when(i >= n_tiles - keep_tiles)
    def _():
        ak_ref[0] = ka
        av_ref[0] = va

    cos64, sin64 = rope_ref[0], rope_ref[1]
    cos32, sin32 = rope_ref[2], rope_ref[3]
    cos_kw, sin_kw = rope_ref[4], rope_ref[5]
    z = proj(C_QB, 2 * D_B)
    for s in range(D_B // LANES):
        sl = slice(s * LANES, (s + 1) * LANES)
        qb = _rope_slab(z[:, sl], cos64, sin64, D_HEAD // 2)
        qb_ref[0, :, sl] = (qb * scale).astype(jnp.bfloat16)
        kb = _rope_slab(z[:, D_B + s * LANES:D_B + (s + 1) * LANES], cos64, sin64, D_HEAD // 2)
        kb_ref[0, :, sl] = kb
        kbb_ref[0, :, sl] = kb.astype(jnp.bfloat16)
    z = proj(C_VB, 2 * D_B)
    vb_ref[0] = z[:, :D_B]
    vbb_ref[0] = z[:, :D_B].astype(jnp.bfloat16)
    gb_ref[0] = z[:, D_B:]
    qm_ref[0] = (proj(C_QM, D_M) * scale).astype(jnp.bfloat16)
    gm_ref[0] = proj(C_GM, D_M)
    z = proj(C_QI, H_IDX * D_IDX)
    for s in range(H_IDX * D_IDX // LANES):
        sl = slice(s * LANES, (s + 1) * LANES)
        qi_ref[0, :, sl] = _rope_slab(z[:, sl], cos32, sin32, D_IDX // 2).astype(jnp.bfloat16)
    z = proj(C_KIWI, 2 * LANES)
    kiwi = _rope_slab(z[:, :LANES], cos_kw, sin_kw, D_IDX // 2)
    kiwi_ref[0] = kiwi
    ki_ref[0] = kiwi[:, :D_IDX]
    ki4_ref[0] = _rope_slab(z[:, LANES:], cos32, sin32, D_IDX // 2).astype(jnp.bfloat16)


def _proj_call(x, g, w, rope, keep_rows):
    B, S, D = x.shape
    tm = min(PROJ_TM, S)
    n_tiles = S // tm
    keep_tiles = keep_rows // tm
    assert n_tiles * tm == S and keep_tiles * tm == keep_rows

    def tile(n, dtype):
        return (jax.ShapeDtypeStruct((B, S, n), dtype),
                pl.BlockSpec((1, tm, n), lambda b, i: (b, i, 0)))

    f32, bf16 = jnp.float32, jnp.bfloat16
    outs = [tile(D_A, bf16), tile(D_A, bf16), tile(D_A, bf16), tile(D_A, f32),
            tile(D_B, bf16), tile(D_B, f32), tile(D_B, f32), tile(D_B, f32),
            tile(D_M, bf16), tile(D_M, f32), tile(H_IDX * D_IDX, bf16),
            tile(LANES, f32), tile(LANES, bf16), tile(D_IDX, f32), tile(D_B, bf16), tile(D_B, bf16)]
    keep_spec = pl.BlockSpec(
        (1, tm, D_A), lambda b, i: (b, jnp.maximum(i - (n_tiles - keep_tiles), 0), 0))
    outs += [(jax.ShapeDtypeStruct((B, keep_rows, D_A), f32), keep_spec)] * 2
    return pl.pallas_call(
        functools.partial(_proj_kernel, n_tiles=n_tiles, keep_tiles=keep_tiles),
        grid=(B, n_tiles),
        in_specs=[pl.BlockSpec((1, tm, D), lambda b, i: (b, i, 0)),
                  pl.BlockSpec((1, D), lambda b, i: (0, 0)),
                  pl.BlockSpec((D, W_COLS), lambda b, i: (0, 0)),
                  pl.BlockSpec((6, tm, LANES), lambda b, i: (0, i, 0))],
        out_specs=[o[1] for o in outs],
        out_shape=[o[0] for o in outs],
        compiler_params=_cparams(2),
        name="proj",
    )(x, g, w, rope)


def _memkv_kernel(m_ref, g_ref, w_ref, mk_ref, mv_ref):
    xn = _rms_scale(m_ref[...], g_ref[...]).astype(jnp.bfloat16)
    mk_ref[...] = jnp.dot(xn, w_ref[:, :D_M], preferred_element_type=jnp.float32)
    mv_ref[...] = jnp.dot(xn, w_ref[:, D_M:], preferred_element_type=jnp.float32)


def _memkv_call(mem2d, g, w):
    R, D = mem2d.shape
    tm = min(PROJ_TM, R)
    assert R % tm == 0
    row = lambda n: pl.BlockSpec((tm, n), lambda i: (i, 0))
    return pl.pallas_call(
        _memkv_kernel,
        grid=(R // tm,),
        in_specs=[row(D), pl.BlockSpec((1, D), lambda i: (0, 0)),
                  pl.BlockSpec((D, 2 * D_M), lambda i: (0, 0))],
        out_specs=[row(D_M), row(D_M)],
        out_shape=[jax.ShapeDtypeStruct((R, D_M), jnp.float32)] * 2,
        compiler_params=_cparams(1),
        name="memkv",
    )(mem2d, g, w)


def _softmax_rows_pv(s, v):
    m = jnp.max(s, axis=1, keepdims=True)
    p = jnp.exp2(s - m)
    l = jnp.sum(p, axis=1, keepdims=True)
    o = jnp.dot(p.astype(jnp.bfloat16), v, preferred_element_type=jnp.float32)
    return o * (1.0 / l)


def _keep_lanes(slab, lo, hi):
    x = slab.astype(jnp.float32)
    lane = lax.broadcasted_iota(jnp.int32, x.shape, 1)
    return jnp.where(lane >= lo, jnp.where(lane < hi, x, 0.0), 0.0).astype(jnp.bfloat16)


def _pair_attend(q_slab, k_slab, v_slab, bias_fn):
    lane = lax.broadcasted_iota(jnp.int32, q_slab.shape, 1)
    out = None
    for hh in range(2):
        s = lax.dot_general(_keep_lanes(q_slab, hh * D_HEAD, (hh + 1) * D_HEAD), k_slab, _NT,
                            preferred_element_type=jnp.float32)
        o = _softmax_rows_pv(bias_fn(s, hh), v_slab)
        out = o if hh == 0 else jnp.where(lane < D_HEAD, out, o)
    return out


def _mem_attend(qm_ref, gm_ref, mk_ref, mv_ref, om_ref):
    for p in range(D_M // LANES):
        sl = slice(p * LANES, (p + 1) * LANES)
        o = _pair_attend(qm_ref[0, :, sl], mk_ref[0, :, sl].astype(jnp.bfloat16),
                         mv_ref[0, :, sl].astype(jnp.bfloat16), lambda s, hh: s)
        om_ref[0, :, sl] = (o * _silu(gm_ref[0, :, sl])).astype(jnp.bfloat16)


def _build_band_bias(vec_ref, bias_ref, n_q, n_valid, chunked):
    n_keys = bias_ref.shape[2]
    i = lax.broadcasted_iota(jnp.int32, (n_q, n_keys), 0)
    r = lax.broadcasted_iota(jnp.int32, (n_q, n_keys), 1)
    lo = ((i >> 6) << 6) if chunked else jnp.zeros_like(i)
    for h in range(bias_ref.shape[0]):
        base = jnp.broadcast_to(vec_ref[h:h + 1, :], (n_q, BIAS_VEC))
        rolled = pltpu.roll(base, 0, 1, stride=1, stride_axis=0)
        b = rolled[:, :n_keys] * LOG2E
        bias_ref[h] = jnp.where(r >= lo, jnp.where(r < lo + n_valid, b, _NEG_INF), _NEG_INF)


def _build_band_bias_t(rvec_ref, bias_ref):
    n_keys, n_q = bias_ref.shape[1], bias_ref.shape[2]
    r = lax.broadcasted_iota(jnp.int32, (n_keys, n_q), 0)
    i = lax.broadcasted_iota(jnp.int32, (n_keys, n_q), 1)
    lo = (i >> 6) << 6
    for h in range(bias_ref.shape[0]):
        base = jnp.broadcast_to(rvec_ref[h:h + 1, :], (n_keys, BIAS_VEC))
        rolled = pltpu.roll(base, 0, 1, stride=1, stride_axis=0)
        b = rolled[:, :n_q] * LOG2E
        bias_ref[h] = jnp.where(r >= lo, jnp.where(r < lo + A_WINDOW + CHUNK, b, _NEG_INF), _NEG_INF)


def _band_prompt_kernel(qa_ref, ka_ref, va_ref, ga_ref, qm_ref, gm_ref, mk_ref, mv_ref, rvec_ref,
                        oa_ref, om_ref, kpad, vt_blk, mk_bf, mv_t, bias_ref):
    j = pl.program_id(1)
    n_front = A_WINDOW // BAND_TQ
    n_win = BAND_WIN // BAND_TQ

    @pl.when(jnp.logical_and(pl.program_id(0) == 0, j == 0))
    def _():
        _build_band_bias_t(rvec_ref, bias_ref)

    @pl.when(j == 0)
    def _():
        kpad[:A_WINDOW, :] = jnp.zeros((A_WINDOW, D_A), jnp.bfloat16)
        kpad[A_WINDOW:, :] = ka_ref[0]
        for t in range(vt_blk.shape[0]):
            if t < n_front:
                vt_blk[t] = jnp.zeros((D_A, BAND_TQ), jnp.bfloat16)
            else:
                rows = slice((t - n_front) * BAND_TQ, (t - n_front + 1) * BAND_TQ)
                vt_blk[t] = va_ref[0, rows, :].astype(jnp.float32).T.astype(jnp.bfloat16)
        mk_bf[...] = mk_ref[0].astype(jnp.bfloat16)
        mv_t[...] = mv_ref[0].T.astype(jnp.bfloat16)

    start = pl.multiple_of(j * BAND_TQ, BAND_TQ)

    def store_a(sl, o):
        oa_ref[0, :, sl] = o

    def store_m(sl, o):
        om_ref[0, :, sl] = o

    def run(mask_front):
        if mask_front:
            pos = lax.broadcasted_iota(jnp.int32, (BAND_WIN, BAND_TQ), 0) + j * BAND_TQ
            front = jnp.where(pos >= A_WINDOW, 0.0, _NEG_INF)
        _kq_attend(lambda h: bias_ref[h] + front if mask_front else bias_ref[h],
                   qa_ref[0], ga_ref[0],
                   lambda sl: kpad[pl.ds(start, BAND_WIN), sl],
                   lambda sl: jnp.concatenate([vt_blk[j + u, sl, :] for u in range(n_win)], axis=1),
                   store_a)

    pl.when(j < n_front)(lambda: run(True))
    pl.when(j >= n_front)(lambda: run(False))
    _kq_attend(lambda h: None, qm_ref[0], gm_ref[0], lambda sl: mk_bf[:, sl], lambda sl: mv_t[sl, :],
               store_m)


def _band_prompt_call(qa, ka, va, ga, qm, gm, mk, mv, bias_rvec):
    B, S, _ = qa.shape
    nq = S // BAND_TQ
    n_mem = mk.shape[1]
    assert nq * BAND_TQ == S and A_WINDOW % BAND_TQ == 0
    qblk = lambda n: pl.BlockSpec((1, BAND_TQ, n), lambda b, j: (b, j, 0))
    full = lambda r, n: pl.BlockSpec((1, r, n), lambda b, j: (b, 0, 0))
    return pl.pallas_call(
        _band_prompt_kernel,
        grid=(B, nq),
        in_specs=[qblk(D_A), full(S, D_A), full(S, D_A), qblk(D_A), qblk(D_M), qblk(D_M),
                  full(n_mem, D_M), full(n_mem, D_M),
                  pl.BlockSpec((H_A, BIAS_VEC), lambda b, j: (0, 0))],
        out_specs=[qblk(D_A), qblk(D_M)],
        out_shape=[jax.ShapeDtypeStruct((B, S, D_A), jnp.bfloat16),
                   jax.ShapeDtypeStruct((B, S, D_M), jnp.bfloat16)],
        scratch_shapes=[pltpu.VMEM((S + A_WINDOW, D_A), jnp.bfloat16),
                        pltpu.VMEM((nq + A_WINDOW // BAND_TQ, D_A, BAND_TQ), jnp.bfloat16),
                        pltpu.VMEM((n_mem, D_M), jnp.bfloat16),
                        pltpu.VMEM((D_M, n_mem), jnp.bfloat16),
                        pltpu.VMEM((H_A, BAND_WIN, BAND_TQ), jnp.float32)],
        compiler_params=_cparams(2),
        name="band_prompt",
    )(qa, ka, va, ga, qm, gm, mk, mv, bias_rvec)


def _band_sample_kernel(qa_ref, kc_ref, vc_ref, kn_ref, vn_ref, ga_ref, qm_ref, gm_ref,
                        mk_ref, mv_ref, vec_ref, oa_ref, om_ref, kcat, vcat, bias_ref):
    P = kc_ref.shape[1]
    T = kn_ref.shape[1]
    pad = kcat.shape[0] - P - T

    @pl.when(pl.program_id(0) == 0)
    def _():
        _build_band_bias(vec_ref, bias_ref, T, P + T, False)

    kcat[:P, :] = kc_ref[0].astype(jnp.bfloat16)
    vcat[:P, :] = vc_ref[0].astype(jnp.bfloat16)
    kcat[P:P + T, :] = kn_ref[0]
    vcat[P:P + T, :] = vn_ref[0]
    zeros = jnp.zeros((pad, D_A), jnp.bfloat16)
    kcat[P + T:, :] = zeros
    vcat[P + T:, :] = zeros
    for p in range(D_A // LANES):
        sl = slice(p * LANES, (p + 1) * LANES)
        o = _pair_attend(qa_ref[0, :, sl], kcat[:, sl], vcat[:, sl],
                         lambda s, hh, p=p: s + bias_ref[2 * p + hh])
        oa_ref[0, :, sl] = (o * _silu(ga_ref[0, :, sl])).astype(jnp.bfloat16)
    _mem_attend(qm_ref, gm_ref, mk_ref, mv_ref, om_ref)


def _band_sample_call(qa, kc, vc, kn, vn, ga, qm, gm, mk, mv, bias_vec):
    B, T, _ = qa.shape
    P = kc.shape[1]
    n_keys = -(-(P + T) // LANES) * LANES
    n_mem = mk.shape[1]
    blk = lambda r, n: pl.BlockSpec((1, r, n), lambda b: (b, 0, 0))
    return pl.pallas_call(
        _band_sample_kernel,
        grid=(B,),
        in_specs=[blk(T, D_A), blk(P, D_A), blk(P, D_A), blk(T, D_A), blk(T, D_A), blk(T, D_A),
                  blk(T, D_M), blk(T, D_M), blk(n_mem, D_M), blk(n_mem, D_M),
                  pl.BlockSpec((H_A, BIAS_VEC), lambda b: (0, 0))],
        out_specs=[blk(T, D_A), blk(T, D_M)],
        out_shape=[jax.ShapeDtypeStruct((B, T, D_A), jnp.bfloat16),
                   jax.ShapeDtypeStruct((B, T, D_M), jnp.bfloat16)],
        scratch_shapes=[pltpu.VMEM((n_keys, D_A), jnp.bfloat16)] * 2
        + [pltpu.VMEM((H_A, T, n_keys), jnp.float32)],
        compiler_params=_cparams(1),
        name="band_sample",
    )(qa, kc, vc, kn, vn, ga, qm, gm, mk, mv, bias_vec)


def _key_to_f32(k):
    bits = k ^ ((k >> 31) & jnp.int32(0x7FFFFFFF))
    return lax.bitcast_convert_type(bits, jnp.float32)


def _count(pred_f32):
    ones = jnp.ones((8, pred_f32.shape[0]), jnp.bfloat16)
    c = jnp.dot(ones, pred_f32.astype(jnp.bfloat16), preferred_element_type=jnp.float32)
    return c[0:1, :]


def _dsa_select(sc_ref, probs, tq, topk):
    one, zero = jnp.float32(1.0), jnp.float32(0.0)
    live = [(off, tk) for off, tk in probs if tk > topk]
    n_live = len(live)

    def sc_of(n):
        off, tk = live[n]
        return sc_ref[off:off + tk, :]

    if live:
        def thr_step(i, ts):
            bit = lax.shift_left(jnp.int32(1), 31 - i)
            out = []
            for n in range(n_live):
                cand = ts[n] + bit
                c = _count(jnp.where(sc_of(n) >= _key_to_f32(cand), one, zero))
                out.append(jnp.where(c >= topk, cand, ts[n]))
            return tuple(out)

        start = tuple(jnp.full((1, tq), _INT_MIN, jnp.int32) for _ in range(n_live))
        t_keys = lax.fori_loop(0, 32, thr_step, start)
        thrs = [jnp.where(t == _INT_MIN, _NEG_INF, _key_to_f32(t)) for t in t_keys]

        jmaxs = []
        for n in range(n_live):
            tk = live[n][1]
            sc = sc_of(n)
            c_gt = _count(jnp.where(sc > thrs[n], one, zero))
            c_ge = _count(jnp.where(sc >= thrs[n], one, zero))
            need = topk - c_gt

            def tie_search(n=n, tk=tk, need=need):
                off = live[n][0]
                tile = 2 * LANES

                def step(first, size, state):
                    before, jmax = state
                    r = lax.broadcasted_iota(jnp.int32, (size, size), 0)
                    c = lax.broadcasted_iota(jnp.int32, (size, size), 1)
                    lower = jnp.where(c <= r, one, zero).astype(jnp.bfloat16)
                    tied = jnp.where(sc_ref[pl.ds(off + first, size), :] == thrs[n], one, zero)
                    rank = before + jnp.dot(lower, tied.astype(jnp.bfloat16),
                                            preferred_element_type=jnp.float32)
                    kidx = (lax.broadcasted_iota(jnp.int32, (size, tq), 0) + first).astype(jnp.float32)
                    taken = jnp.where(tied > 0, jnp.where(rank <= need, kidx, -one), -one)
                    jmax = jnp.maximum(jmax, jnp.max(taken, axis=0, keepdims=True))
                    return rank[size - 1:size, :], jmax

                state = (jnp.zeros((1, tq), jnp.float32), jnp.full((1, tq), -1.0, jnp.float32))
                n_full, rest = divmod(tk, tile)
                state = lax.fori_loop(
                    0, n_full, lambda t, st: step(pl.multiple_of(t * tile, tile), tile, st), state)
                if rest:
                    state = step(n_full * tile, rest, state)
                return state[1].astype(jnp.int32)

            has_excess = jnp.max(jnp.where(c_ge > topk, one, zero)) > 0
            jmaxs.append(lax.cond(has_excess, tie_search,
                                  lambda tk=tk: jnp.full((1, tq), tk, jnp.int32)))

    for off, tk in probs:
        sc = sc_ref[off:off + tk, :]
        keep = jnp.where(jnp.abs(sc) < jnp.float32(jnp.inf), zero, _NEG_INF)
        if tk > topk:
            n = live.index((off, tk))
            kidx = lax.broadcasted_iota(jnp.int32, (tk, tq), 0)
            keep = jnp.where(sc > thrs[n], keep,
                             jnp.where(sc == thrs[n], jnp.where(kidx <= jmaxs[n], keep, _NEG_INF),
                                       _NEG_INF))
        sc_ref[off:off + tk, :] = keep


def _dsa_scores(limit, qi, kiwi, ki4):
    tk, tq = ki4.shape[0], qi.shape[0]
    kiwi_t = kiwi.T
    acc = jnp.zeros((tk, tq), jnp.float32)
    heads_per_slab = LANES // D_IDX

    def head_q(h):
        slab = qi[:, (h // heads_per_slab) * LANES:(h // heads_per_slab + 1) * LANES]
        lo = (h % heads_per_slab) * D_IDX
        return _keep_lanes(slab, lo, lo + D_IDX)

    for h in range(0, H_IDX, 2):
        d = lax.dot_general(ki4, jnp.concatenate([head_q(h), head_q(h + 1)], axis=0), _NT,
                            preferred_element_type=jnp.float32)
        for e in range(2):
            w = kiwi_t[D_IDX + h + e:D_IDX + h + e + 1, :] * ((D_IDX ** -0.5) * (H_IDX ** -0.5))
            acc = acc + w * jnp.maximum(d[:, e * tq:(e + 1) * tq], 0.0)
    if limit is not None:
        kidx = lax.broadcasted_iota(jnp.int32, (tk, tq), 0)
        acc = jnp.where(kidx < limit, acc, _NEG_INF)
    return acc


def _kq_attend(bias_of, q, g, k_slab_of, vt_slab_of, store):
    tq = q.shape[0]
    for p in range(q.shape[1] // LANES):
        sl = slice(p * LANES, (p + 1) * LANES)
        q_slab = q[:, sl]
        q_pair = jnp.concatenate([_keep_lanes(q_slab, 0, D_HEAD), _keep_lanes(q_slab, D_HEAD, LANES)],
                                 axis=0)
        s_pair = lax.dot_general(k_slab_of(sl), q_pair, _NT, preferred_element_type=jnp.float32)
        probs, inv_l = [], []
        for hh in range(2):
            s = s_pair[:, hh * tq:(hh + 1) * tq]
            bias = bias_of(2 * p + hh)
            if bias is not None:
                s = s + bias
            m = jnp.max(s, axis=0, keepdims=True)
            pr = jnp.exp2(s - m)
            inv_l.append(1.0 / jnp.sum(pr, axis=0, keepdims=True))
            probs.append(pr.astype(jnp.bfloat16))
        o_pair = jnp.dot(vt_slab_of(sl), jnp.concatenate(probs, axis=1),
                         preferred_element_type=jnp.float32)
        ot = jnp.concatenate([o_pair[:D_HEAD, :tq] * inv_l[0], o_pair[D_HEAD:, tq:] * inv_l[1]], axis=0)
        store(sl, (ot.T * _silu(g[:, sl])).astype(jnp.bfloat16))


def _dsa_blocks(S, tq):
    return [(tq * j * (j + 1) // 2, (j + 1) * tq) for j in range(S // tq)]


def _own_region():
    return pl.when(pl.program_id(0) >= 0)


def _dsa_select_kernel(qi_ref, kiwi_ref, ki4_ref, keep_ref, sc_ref, *, topk, tq):
    own_region = _own_region()
    probs = _dsa_blocks(ki4_ref.shape[1], tq)
    for j, (off, tk) in enumerate(probs):
        rows = slice(j * tq, (j + 1) * tq)

        @own_region
        def _(j=j, tk=tk, off=off, rows=rows):
            qpos = j * tq + lax.broadcasted_iota(jnp.int32, (1, tq), 1)
            limit = ((qpos >> 6) + 1) << 6
            sc_ref[off:off + tk, :] = _dsa_scores(limit, qi_ref[0, rows, :], kiwi_ref[0, rows, :],
                                                  ki4_ref[0, :tk, :])

    _dsa_select(sc_ref, probs, tq, topk)
    keep_ref[0] = sc_ref[...].astype(jnp.bfloat16)


def _dsa_attend_kernel(q_ref, g_ref, k_ref, v_ref, keep_ref, o_ref, vt, *, tq):
    own_region = _own_region()
    S = k_ref.shape[1]
    for t in range(S // tq):
        vt[t] = v_ref[0, t * tq:(t + 1) * tq, :].astype(jnp.float32).T.astype(jnp.bfloat16)
    for j, (off, tk) in enumerate(_dsa_blocks(S, tq)):
        rows = slice(j * tq, (j + 1) * tq)

        @own_region
        def _(j=j, tk=tk, off=off, rows=rows):
            def store(sl, o):
                o_ref[0, rows, sl] = o

            bias = keep_ref[0, off:off + tk, :].astype(jnp.float32)
            _kq_attend(lambda h: bias, q_ref[0, rows, :], g_ref[0, rows, :],
                       lambda sl: k_ref[0, :tk, sl],
                       lambda sl: jnp.concatenate([vt[t, sl, :] for t in range(j + 1)], axis=1), store)


def _dsa_prompt_call(qb, qi, kiwi, gb, kb, vb, ki4):
    B, S, _ = qb.shape
    tq = DSA_TQ
    nq = S // tq
    assert CHUNK == 64 and nq * tq == S
    topk = min(TOPK_MAX, S // 4)
    n_rows = tq * nq * (nq + 1) // 2
    full = lambda n: pl.BlockSpec((1, S, n), lambda b: (b, 0, 0))
    keep_spec = pl.BlockSpec((1, n_rows, tq), lambda b: (b, 0, 0))
    keep = pl.pallas_call(
        functools.partial(_dsa_select_kernel, topk=topk, tq=tq),
        grid=(B,),
        in_specs=[full(H_IDX * D_IDX), full(LANES), full(LANES)],
        out_specs=keep_spec,
        out_shape=jax.ShapeDtypeStruct((B, n_rows, tq), jnp.bfloat16),
        scratch_shapes=[pltpu.VMEM((n_rows, tq), jnp.float32)],
        compiler_params=_cparams(1),
        name="dsa_select",
    )(qi, kiwi, ki4)
    return pl.pallas_call(
        functools.partial(_dsa_attend_kernel, tq=tq),
        grid=(B,),
        in_specs=[full(D_B), full(D_B), full(D_B), full(D_B), keep_spec],
        out_specs=full(D_B),
        out_shape=jax.ShapeDtypeStruct((B, S, D_B), jnp.bfloat16),
        scratch_shapes=[pltpu.VMEM((nq, D_B, tq), jnp.bfloat16)],
        compiler_params=_cparams(1),
        name="dsa_attend",
    )(qb, gb, kb, vb, keep)


def _dsa_sample_kernel(q_ref, qi_ref, kiwi_ref, g_ref, kc_ref, vc_ref, kic_ref,
                       kn_ref, vn_ref, kin_ref, o_ref, kbf, vt, ki4s, sc_ref, *, topk):
    P = kc_ref.shape[1]
    T = kn_ref.shape[1]
    tq = q_ref.shape[1]
    kbf[:P, :] = kc_ref[0].astype(jnp.bfloat16)
    kbf[P:, :] = kn_ref[0].astype(jnp.bfloat16)
    vt[:, :P] = vc_ref[0].T.astype(jnp.bfloat16)
    vn_tile = jnp.concatenate([vn_ref[0], jnp.zeros((LANES - T, D_B), jnp.float32)], axis=0)
    vt[:, P:] = vn_tile.T[:, :T].astype(jnp.bfloat16)
    ki4s[:P, :] = kic_ref[0]
    ki4s[P:, :] = kin_ref[0]
    sc_ref[...] = _dsa_scores(None, qi_ref[0], kiwi_ref[0], ki4s[...])
    _dsa_select(sc_ref, [(0, P + T)], tq, topk)

    def store(sl, o):
        o_ref[0, :, sl] = o

    _kq_attend(lambda h: sc_ref[...], q_ref[0], g_ref[0], lambda sl: kbf[:, sl], lambda sl: vt[sl, :],
               store)


def _dsa_sample_call(qb, qi, kiwi, gb, kc, vc, kic, kn, vn, kin):
    B, tq, _ = qb.shape
    P, T = kc.shape[1], kn.shape[1]
    tk = P + T
    topk = min(TOPK_MAX, tk // 4)
    blk = lambda r, n: pl.BlockSpec((1, r, n), lambda b: (b, 0, 0))
    return pl.pallas_call(
        functools.partial(_dsa_sample_kernel, topk=topk),
        grid=(B,),
        in_specs=[blk(tq, D_B), blk(tq, H_IDX * D_IDX), blk(tq, LANES), blk(tq, D_B),
                  blk(P, D_B), blk(P, D_B), blk(P, LANES),
                  blk(T, D_B), blk(T, D_B), blk(T, LANES)],
        out_specs=blk(tq, D_B),
        out_shape=jax.ShapeDtypeStruct((B, tq, D_B), jnp.bfloat16),
        scratch_shapes=[pltpu.VMEM((tk, D_B), jnp.bfloat16),
                        pltpu.VMEM((D_B, tk), jnp.bfloat16),
                        pltpu.VMEM((tk, LANES), jnp.bfloat16),
                        pltpu.VMEM((tk, tq), jnp.float32)],
        compiler_params=_cparams(1),
        name="dsa_sample",
    )(qb, qi, kiwi, gb, kc, vc, kic, kn, vn, kin)


def _merge_kernel(x_ref, oa_ref, ob_ref, om_ref, w_ref, g_ref, y_ref, *, last):
    acc = x_ref[...]
    acc = acc + jnp.dot(oa_ref[...], w_ref[:D_A, :], preferred_element_type=jnp.float32)
    acc = acc + jnp.dot(ob_ref[...], w_ref[D_A:D_A + D_B, :], preferred_element_type=jnp.float32)
    acc = acc + jnp.dot(om_ref[...], w_ref[D_A + D_B:, :], preferred_element_type=jnp.float32)
    y_ref[...] = _rms_scale(acc, g_ref[...]) if last else acc


def _merge_call(x2d, oa, ob, om, w, g, last):
    R, D = x2d.shape
    tm = min(PROJ_TM, R)
    assert R % tm == 0
    row = lambda n: pl.BlockSpec((tm, n), lambda i: (i, 0))
    return pl.pallas_call(
        functools.partial(_merge_kernel, last=last),
        grid=(R // tm,),
        in_specs=[row(D), row(D_A), row(D_B), row(D_M),
                  pl.BlockSpec((D_A + D_B + D_M, D), lambda i: (0, 0)),
                  pl.BlockSpec((1, D), lambda i: (0, 0))],
        out_specs=row(D),
        out_shape=jax.ShapeDtypeStruct((R, D), jnp.float32),
        compiler_params=_cparams(1),
        name="merge",
    )(x2d, oa, ob, om, w, g)


def _pack_w_in(w):
    D = w.shape[0]
    ki = w[:, C_KIWI:C_KIWI + D_IDX]
    wi = w[:, C_KIWI + D_IDX:C_KIWI + D_IDX + H_IDX]
    pad = jnp.zeros((D, LANES - D_IDX - H_IDX), w.dtype)
    packed = jnp.concatenate([w[:, :C_KIWI], ki, wi, pad, jnp.tile(ki, (1, LANES // D_IDX))], axis=1)
    return packed.astype(jnp.bfloat16)


def _rope_tables(pos):
    posf = pos.astype(jnp.float32)

    def tables(d):
        half = d // 2
        inv_freq = ROPE_THETA ** (-jnp.arange(half, dtype=jnp.float32) * 2.0 / d)
        ang = posf[:, None] * inv_freq[None, :]
        cos, sin = jnp.cos(ang), jnp.sin(ang)
        return jnp.concatenate([cos, cos], axis=1), jnp.concatenate([-sin, sin], axis=1)

    cos64, sin64 = tables(D_HEAD)
    cos32, sin32 = tables(D_IDX)
    n = pos.shape[0]
    rest = LANES - D_IDX
    cos_kw = jnp.concatenate([cos32, jnp.ones((n, rest), jnp.float32)], axis=1)
    sin_kw = jnp.concatenate([sin32, jnp.zeros((n, rest), jnp.float32)], axis=1)
    rep = lambda t, d: jnp.tile(t, (1, LANES // d))
    return jnp.stack([rep(cos64, D_HEAD), rep(sin64, D_HEAD), rep(cos32, D_IDX), rep(sin32, D_IDX),
                      cos_kw, sin_kw])


def _band_bias_vec(table):
    n_far = A_WINDOW - REL_CLIP + 1
    lo_idx = A_WINDOW + REL_CLIP - (BAND_WIN - 1)
    assert lo_idx >= 0 and BIAS_VEC >= BAND_WIN + BAND_TQ - 1
    far = table[:, 2 * REL_CLIP:2 * REL_CLIP + 1].astype(jnp.float32)
    mid = table[:, lo_idx:2 * REL_CLIP][:, ::-1].astype(jnp.float32)
    vec = jnp.concatenate([jnp.tile(far, (1, n_far)), mid, jnp.tile(far, (1, BIAS_VEC - BAND_WIN))], axis=1)
    rvec = jnp.concatenate([vec[:, :1], vec[:, 1:][:, ::-1]], axis=1)
    return vec, rvec


def kernel(x_prompt, x_sample, mem_prompt, cache_a_k, cache_a_v, cache_b_k, cache_b_v, cache_b_kidx,
           cache_mem_k, cache_mem_v, norm_mix_g, w_in, rel_bias_a, norm_mem_g, w_mem_kv, w_out,
           norm_final_g):
    B, S, D = x_prompt.shape
    Bs, T, _ = x_sample.shape
    depth = w_in.shape[0]
    P = cache_b_k.shape[2]
    Pa = cache_a_k.shape[2]
    n_mem = mem_prompt.shape[1]
    keep = min(A_WINDOW, S)
    assert Pa == A_WINDOW and T <= CHUNK and LANES % T == 0

    rope_p = _rope_tables(jnp.arange(S))
    rope_s = jnp.tile(_rope_tables(P + jnp.arange(T)), (1, Bs, 1))
    g_final = norm_final_g.reshape(1, D)

    xp, xs = x_prompt, x_sample
    outs_p = [[] for _ in range(7)]
    outs_s = [[] for _ in range(5)]
    for l in range(depth):
        last = l == depth - 1
        w = _pack_w_in(w_in[l])
        g_mix = norm_mix_g[l].reshape(1, D)
        w_o = w_out[l].astype(jnp.bfloat16)
        bias_vec, bias_rvec = _band_bias_vec(rel_bias_a[l])

        (qa, ka, va, ga, qb, kb, vb, gb, qm, gm, qi, kiwi, ki4, ki, kbb, vbb, ak, av) = _proj_call(
            xp, g_mix, w, rope_p, keep)
        mk, mv = _memkv_call(mem_prompt.reshape(B * n_mem, D), norm_mem_g[l].reshape(1, D),
                             w_mem_kv[l].astype(jnp.bfloat16))
        mk = mk.reshape(B, n_mem, D_M)
        mv = mv.reshape(B, n_mem, D_M)
        oa, om = _band_prompt_call(qa, ka, va, ga, qm, gm, mk, mv, bias_rvec)
        ob = _dsa_prompt_call(qb, qi, kiwi, gb, kbb, vbb, ki4)
        xp = _merge_call(xp.reshape(B * S, D), oa.reshape(B * S, D_A), ob.reshape(B * S, D_B),
                         om.reshape(B * S, D_M), w_o, g_final, last).reshape(B, S, D)
        for lst, t in zip(outs_p, (ak.reshape(B, keep, H_A, D_HEAD), av.reshape(B, keep, H_A, D_HEAD),
                                   kb.reshape(B, S, H_B, D_HEAD), vb.reshape(B, S, H_B, D_HEAD), ki,
                                   mk.reshape(B, n_mem, H_M, D_HEAD), mv.reshape(B, n_mem, H_M, D_HEAD))):
            lst.append(t)

        (qa, ka, va, ga, qb, kb, vb, gb, qm, gm, qi, kiwi, ki4, ki, kbb, vbb, ak, av) = _proj_call(
            xs.reshape(1, Bs * T, D), g_mix, w, rope_s, Bs * T)
        per_b = lambda t: t.reshape(Bs, T, t.shape[-1])
        oa, om = _band_sample_call(
            per_b(qa), cache_a_k[l].reshape(Bs, Pa, D_A), cache_a_v[l].reshape(Bs, Pa, D_A),
            per_b(ka), per_b(va), per_b(ga), per_b(qm), per_b(gm),
            cache_mem_k[l].reshape(Bs, n_mem, D_M), cache_mem_v[l].reshape(Bs, n_mem, D_M), bias_vec)
        rep = lambda t: jnp.tile(per_b(t), (1, LANES // T, 1))
        kic = jnp.tile(cache_b_kidx[l], (1, 1, LANES // D_IDX)).astype(jnp.bfloat16)
        ob = _dsa_sample_call(rep(qb), rep(qi), rep(kiwi), rep(gb),
                              cache_b_k[l].reshape(Bs, P, D_B), cache_b_v[l].reshape(Bs, P, D_B), kic,
                              per_b(kb), per_b(vb), per_b(ki4))[:, :T]
        xs = _merge_call(xs.reshape(Bs * T, D), oa.reshape(Bs * T, D_A), ob.reshape(Bs * T, D_B),
                         om.reshape(Bs * T, D_M), w_o, g_final, last).reshape(Bs, T, D)
        for lst, t in zip(outs_s, (ak.reshape(Bs, T, H_A, D_HEAD), av.reshape(Bs, T, H_A, D_HEAD),
                                   kb.reshape(Bs, T, H_B, D_HEAD), vb.reshape(Bs, T, H_B, D_HEAD),
                                   ki.reshape(Bs, T, D_IDX))):
            lst.append(t)

    st = lambda ts: jnp.stack(ts, axis=0)
    return (xp, xs) + tuple(st(t) for t in outs_p) + tuple(st(t) for t in outs_s)
```

```python
import functools
import math

import jax
import jax.numpy as jnp
from jax import lax
from jax.experimental import pallas as pl
from jax.experimental.pallas import tpu as pltpu

CHUNK = 64
D_HEAD = 64
H_A = 6
H_B = 6
H_M = 4
D_A = H_A * D_HEAD
D_B = H_B * D_HEAD
D_M = H_M * D_HEAD
A_LEFT_CHUNKS = 8
A_WINDOW = A_LEFT_CHUNKS * CHUNK
REL_CLIP = 256
H_IDX = 8
D_IDX = 32
TOPK_MAX = 256
ROPE_THETA = 10000.0
EPS = 1e-6

LANES = 128
VMEM_LIMIT_BYTES = 56 * 1024 * 1024

C_QA, C_KA, C_VA, C_GA = 0, D_A, 2 * D_A, 3 * D_A
C_QB = 4 * D_A
C_KB, C_VB, C_GB = C_QB + D_B, C_QB + 2 * D_B, C_QB + 3 * D_B
C_QM = C_QB + 4 * D_B
C_GM = C_QM + D_M
C_QI = C_GM + D_M
C_KIWI = C_QI + H_IDX * D_IDX
C_KI4 = C_KIWI + LANES
W_COLS = C_KI4 + LANES

PROJ_TM = 512
BAND_TQ = 256
BAND_WIN = A_WINDOW + BAND_TQ
DSA_TQ = 256

LOG2E = math.log2(math.e)
QK_SCALE = (D_HEAD ** -0.5) * LOG2E
BIAS_VEC = 8 * LANES

_NT = (((1,), (1,)), ((), ()))
_INT_MIN = -2 ** 31
_NEG_INF = float("-inf")


def _cparams(n_axes):
    return pltpu.CompilerParams(
        dimension_semantics=("arbitrary",) * n_axes,
        vmem_limit_bytes=VMEM_LIMIT_BYTES)


def _silu(g):
    return g * (1.0 / (1.0 + jnp.exp(-g)))


def _rms_scale(x, g):
    ms = jnp.mean(x * x, axis=-1, keepdims=True)
    return (x * lax.rsqrt(ms + EPS)) * g


def _rope_slab(z, cos, sin, half):
    lane = lax.broadcasted_iota(jnp.int32, z.shape, 1)
    first = (lane & (2 * half - 1)) < half
    partner = jnp.where(first, pltpu.roll(z, LANES - half, 1), pltpu.roll(z, half, 1))
    return z * cos + partner * sin


def _proj_kernel(x_ref, g_ref, w_ref, rope_ref,
                 qa_ref, ka_ref, va_ref, ga_ref, qb_ref, kb_ref, vb_ref, gb_ref,
                 qm_ref, gm_ref, qi_ref, kiwi_ref, ki4_ref, ki_ref, kbb_ref, vbb_ref, ak_ref, av_ref,
                 *, n_tiles, keep_tiles):
    i = pl.program_id(1)
    xn = _rms_scale(x_ref[0], g_ref[...]).astype(jnp.bfloat16)

    def proj(c0, n):
        return jnp.dot(xn, w_ref[:, c0:c0 + n], preferred_element_type=jnp.float32)

    scale = QK_SCALE
    z = proj(C_QA, 2 * D_A)
    qa_ref[0] = (z[:, :D_A] * scale).astype(jnp.bfloat16)
    ka = z[:, D_A:]
    ka_ref[0] = ka.astype(jnp.bfloat16)
    z = proj(C_VA, 2 * D_A)
    va = z[:, :D_A]
    va_ref[0] = va.astype(jnp.bfloat16)
    ga_ref[0] = z[:, D_A:]

    @pl.when(i >= n_tiles - keep_tiles)
    def _():
        ak_ref[0] = ka
        av_ref[0] = va

    cos64, sin64 = rope_ref[0], rope_ref[1]
    cos32, sin32 = rope_ref[2], rope_ref[3]
    cos_kw, sin_kw = rope_ref[4], rope_ref[5]
    z = proj(C_QB, 2 * D_B)
    for s in range(D_B // LANES):
        sl = slice(s * LANES, (s + 1) * LANES)
        qb = _rope_slab(z[:, sl], cos64, sin64, D_HEAD // 2)
        qb_ref[0, :, sl] = (qb * scale).astype(jnp.bfloat16)
        kb = _rope_slab(z[:, D_B + s * LANES:D_B + (s + 1) * LANES], cos64, sin64, D_HEAD // 2)
        kb_ref[0, :, sl] = kb
        kbb_ref[0, :, sl] = kb.astype(jnp.bfloat16)
    z = proj(C_VB, 2 * D_B)
    vb_ref[0] = z[:, :D_B]
    vbb_ref[0] = z[:, :D_B].astype(jnp.bfloat16)
    gb_ref[0] = z[:, D_B:]
    qm_ref[0] = (proj(C_QM, D_M) * scale).astype(jnp.bfloat16)
    gm_ref[0] = proj(C_GM, D_M)
    z = proj(C_QI, H_IDX * D_IDX)
    for s in range(H_IDX * D_IDX // LANES):
        sl = slice(s * LANES, (s + 1) * LANES)
        qi_ref[0, :, sl] = _rope_slab(z[:, sl], cos32, sin32, D_IDX // 2).astype(jnp.bfloat16)
    z = proj(C_KIWI, 2 * LANES)
    kiwi = _rope_slab(z[:, :LANES], cos_kw, sin_kw, D_IDX // 2)
    kiwi_ref[0] = kiwi
    ki_ref[0] = kiwi[:, :D_IDX]
    ki4_ref[0] = _rope_slab(z[:, LANES:], cos32, sin32, D_IDX // 2).astype(jnp.bfloat16)


def _proj_call(x, g, w, rope, keep_rows):
    B, S, D = x.shape
    tm = min(PROJ_TM, S)
    n_tiles = S // tm
    keep_tiles = keep_rows // tm
    assert n_tiles * tm == S and keep_tiles * tm == keep_rows

    def tile(n, dtype):
        return (jax.ShapeDtypeStruct((B, S, n), dtype),
                pl.BlockSpec((1, tm, n), lambda b, i: (b, i, 0)))

    f32, bf16 = jnp.float32, jnp.bfloat16
    outs = [tile(D_A, bf16), tile(D_A, bf16), tile(D_A, bf16), tile(D_A, f32),
            tile(D_B, bf16), tile(D_B, f32), tile(D_B, f32), tile(D_B, f32),
            tile(D_M, bf16), tile(D_M, f32), tile(H_IDX * D_IDX, bf16),
            tile(LANES, f32), tile(LANES, bf16), tile(D_IDX, f32), tile(D_B, bf16), tile(D_B, bf16)]
    keep_spec = pl.BlockSpec(
        (1, tm, D_A), lambda b, i: (b, jnp.maximum(i - (n_tiles - keep_tiles), 0), 0))
    outs += [(jax.ShapeDtypeStruct((B, keep_rows, D_A), f32), keep_spec)] * 2
    return pl.pallas_call(
        functools.partial(_proj_kernel, n_tiles=n_tiles, keep_tiles=keep_tiles),
        grid=(B, n_tiles),
        in_specs=[pl.BlockSpec((1, tm, D), lambda b, i: (b, i, 0)),
                  pl.BlockSpec((1, D), lambda b, i: (0, 0)),
                  pl.BlockSpec((D, W_COLS), lambda b, i: (0, 0)),
                  pl.BlockSpec((6, tm, LANES), lambda b, i: (0, i, 0))],
        out_specs=[o[1] for o in outs],
        out_shape=[o[0] for o in outs],
        compiler_params=_cparams(2),
        name="proj",
    )(x, g, w, rope)


def _memkv_kernel(m_ref, g_ref, w_ref, mk_ref, mv_ref):
    xn = _rms_scale(m_ref[...], g_ref[...]).astype(jnp.bfloat16)
    mk_ref[...] = jnp.dot(xn, w_ref[:, :D_M], preferred_element_type=jnp.float32)
    mv_ref[...] = jnp.dot(xn, w_ref[:, D_M:], preferred_element_type=jnp.float32)


def _memkv_call(mem2d, g, w):
    R, D = mem2d.shape
    tm = min(PROJ_TM, R)
    assert R % tm == 0
    row = lambda n: pl.BlockSpec((tm, n), lambda i: (i, 0))
    return pl.pallas_call(
        _memkv_kernel,
        grid=(R // tm,),
        in_specs=[row(D), pl.BlockSpec((1, D), lambda i: (0, 0)),
                  pl.BlockSpec((D, 2 * D_M), lambda i: (0, 0))],
        out_specs=[row(D_M), row(D_M)],
        out_shape=[jax.ShapeDtypeStruct((R, D_M), jnp.float32)] * 2,
        compiler_params=_cparams(1),
        name="memkv",
    )(mem2d, g, w)


def _softmax_rows_pv(s, v):
    m = jnp.max(s, axis=1, keepdims=True)
    p = jnp.exp2(s - m)
    l = jnp.sum(p, axis=1, keepdims=True)
    o = jnp.dot(p.astype(jnp.bfloat16), v, preferred_element_type=jnp.float32)
    return o * (1.0 / l)


def _keep_lanes(slab, lo, hi):
    x = slab.astype(jnp.float32)
    lane = lax.broadcasted_iota(jnp.int32, x.shape, 1)
    return jnp.where(lane >= lo, jnp.where(lane < hi, x, 0.0), 0.0).astype(jnp.bfloat16)


def _pair_attend(q_slab, k_slab, v_slab, bias_fn):
    lane = lax.broadcasted_iota(jnp.int32, q_slab.shape, 1)
    out = None
    for hh in range(2):
        s = lax.dot_general(_keep_lanes(q_slab, hh * D_HEAD, (hh + 1) * D_HEAD), k_slab, _NT,
                            preferred_element_type=jnp.float32)
        o = _softmax_rows_pv(bias_fn(s, hh), v_slab)
        out = o if hh == 0 else jnp.where(lane < D_HEAD, out, o)
    return out


def _mem_attend(qm_ref, gm_ref, mk_ref, mv_ref, om_ref):
    for p in range(D_M // LANES):
        sl = slice(p * LANES, (p + 1) * LANES)
        o = _pair_attend(qm_ref[0, :, sl], mk_ref[0, :, sl].astype(jnp.bfloat16),
                         mv_ref[0, :, sl].astype(jnp.bfloat16), lambda s, hh: s)
        om_ref[0, :, sl] = (o * _silu(gm_ref[0, :, sl])).astype(jnp.bfloat16)


def _build_band_bias(vec_ref, bias_ref, n_q, n_valid, chunked):
    n_keys = bias_ref.shape[2]
    i = lax.broadcasted_iota(jnp.int32, (n_q, n_keys), 0)
    r = lax.broadcasted_iota(jnp.int32, (n_q, n_keys), 1)
    lo = ((i >> 6) << 6) if chunked else jnp.zeros_like(i)
    for h in range(bias_ref.shape[0]):
        base = jnp.broadcast_to(vec_ref[h:h + 1, :], (n_q, BIAS_VEC))
        rolled = pltpu.roll(base, 0, 1, stride=1, stride_axis=0)
        b = rolled[:, :n_keys] * LOG2E
        bias_ref[h] = jnp.where(r >= lo, jnp.where(r < lo + n_valid, b, _NEG_INF), _NEG_INF)


def _build_band_bias_t(rvec_ref, bias_ref):
    n_keys, n_q = bias_ref.shape[1], bias_ref.shape[2]
    r = lax.broadcasted_iota(jnp.int32, (n_keys, n_q), 0)
    i = lax.broadcasted_iota(jnp.int32, (n_keys, n_q), 1)
    lo = (i >> 6) << 6
    for h in range(bias_ref.shape[0]):
        base = jnp.broadcast_to(rvec_ref[h:h + 1, :], (n_keys, BIAS_VEC))
        rolled = pltpu.roll(base, 0, 1, stride=1, stride_axis=0)
        b = rolled[:, :n_q] * LOG2E
        bias_ref[h] = jnp.where(r >= lo, jnp.where(r < lo + A_WINDOW + CHUNK, b, _NEG_INF), _NEG_INF)


def _band_prompt_kernel(qa_ref, ka_ref, va_ref, ga_ref, qm_ref, gm_ref, mk_ref, mv_ref, rvec_ref,
                        oa_ref, om_ref, kpad, vt_blk, mk_bf, mv_t, bias_ref):
    j = pl.program_id(1)
    n_front = A_WINDOW // BAND_TQ
    n_win = BAND_WIN // BAND_TQ

    @pl.when(jnp.logical_and(pl.program_id(0) == 0, j == 0))
    def _():
        _build_band_bias_t(rvec_ref, bias_ref)

    @pl.when(j == 0)
    def _():
        kpad[:A_WINDOW, :] = jnp.zeros((A_WINDOW, D_A), jnp.bfloat16)
        kpad[A_WINDOW:, :] = ka_ref[0]
        for t in range(vt_blk.shape[0]):
            if t < n_front:
                vt_blk[t] = jnp.zeros((D_A, BAND_TQ), jnp.bfloat16)
            else:
                rows = slice((t - n_front) * BAND_TQ, (t - n_front + 1) * BAND_TQ)
                vt_blk[t] = va_ref[0, rows, :].astype(jnp.float32).T.astype(jnp.bfloat16)
        mk_bf[...] = mk_ref[0].astype(jnp.bfloat16)
        mv_t[...] = mv_ref[0].T.astype(jnp.bfloat16)

    start = pl.multiple_of(j * BAND_TQ, BAND_TQ)

    def store_a(sl, o):
        oa_ref[0, :, sl] = o

    def store_m(sl, o):
        om_ref[0, :, sl] = o

    def run(mask_front):
        if mask_front:
            pos = lax.broadcasted_iota(jnp.int32, (BAND_WIN, BAND_TQ), 0) + j * BAND_TQ
            front = jnp.where(pos >= A_WINDOW, 0.0, _NEG_INF)
        _kq_attend(lambda h: bias_ref[h] + front if mask_front else bias_ref[h],
                   qa_ref[0], ga_ref[0],
                   lambda sl: kpad[pl.ds(start, BAND_WIN), sl],
                   lambda sl: jnp.concatenate([vt_blk[j + u, sl, :] for u in range(n_win)], axis=1),
                   store_a)
        _kq_attend(lambda h: None, qm_ref[0], gm_ref[0], lambda sl: mk_bf[:, sl], lambda sl: mv_t[sl, :],
                   store_m)

    pl.when(j < n_front)(lambda: run(True))
    pl.when(j >= n_front)(lambda: run(False))


def _band_prompt_call(qa, ka, va, ga, qm, gm, mk, mv, bias_rvec):
    B, S, _ = qa.shape
    nq = S // BAND_TQ
    n_mem = mk.shape[1]
    assert nq * BAND_TQ == S and A_WINDOW % BAND_TQ == 0
    qblk = lambda n: pl.BlockSpec((1, BAND_TQ, n), lambda b, j: (b, j, 0))
    full = lambda r, n: pl.BlockSpec((1, r, n), lambda b, j: (b, 0, 0))
    return pl.pallas_call(
        _band_prompt_kernel,
        grid=(B, nq),
        in_specs=[qblk(D_A), full(S, D_A), full(S, D_A), qblk(D_A), qblk(D_M), qblk(D_M),
                  full(n_mem, D_M), full(n_mem, D_M),
                  pl.BlockSpec((H_A, BIAS_VEC), lambda b, j: (0, 0))],
        out_specs=[qblk(D_A), qblk(D_M)],
        out_shape=[jax.ShapeDtypeStruct((B, S, D_A), jnp.bfloat16),
                   jax.ShapeDtypeStruct((B, S, D_M), jnp.bfloat16)],
        scratch_shapes=[pltpu.VMEM((S + A_WINDOW, D_A), jnp.bfloat16),
                        pltpu.VMEM((nq + A_WINDOW // BAND_TQ, D_A, BAND_TQ), jnp.bfloat16),
                        pltpu.VMEM((n_mem, D_M), jnp.bfloat16),
                        pltpu.VMEM((D_M, n_mem), jnp.bfloat16),
                        pltpu.VMEM((H_A, BAND_WIN, BAND_TQ), jnp.float32)],
        compiler_params=_cparams(2),
        name="band_prompt",
    )(qa, ka, va, ga, qm, gm, mk, mv, bias_rvec)


def _band_sample_kernel(qa_ref, kc_ref, vc_ref, kn_ref, vn_ref, ga_ref, qm_ref, gm_ref,
                        mk_ref, mv_ref, vec_ref, oa_ref, om_ref, kcat, vcat, bias_ref):
    P = kc_ref.shape[1]
    T = kn_ref.shape[1]
    pad = kcat.shape[0] - P - T

    @pl.when(pl.program_id(0) == 0)
    def _():
        _build_band_bias(vec_ref, bias_ref, T, P + T, False)

    kcat[:P, :] = kc_ref[0].astype(jnp.bfloat16)
    vcat[:P, :] = vc_ref[0].astype(jnp.bfloat16)
    kcat[P:P + T, :] = kn_ref[0]
    vcat[P:P + T, :] = vn_ref[0]
    zeros = jnp.zeros((pad, D_A), jnp.bfloat16)
    kcat[P + T:, :] = zeros
    vcat[P + T:, :] = zeros
    for p in range(D_A // LANES):
        sl = slice(p * LANES, (p + 1) * LANES)
        o = _pair_attend(qa_ref[0, :, sl], kcat[:, sl], vcat[:, sl],
                         lambda s, hh, p=p: s + bias_ref[2 * p + hh])
        oa_ref[0, :, sl] = (o * _silu(ga_ref[0, :, sl])).astype(jnp.bfloat16)
    _mem_attend(qm_ref, gm_ref, mk_ref, mv_ref, om_ref)


def _band_sample_call(qa, kc, vc, kn, vn, ga, qm, gm, mk, mv, bias_vec):
    B, T, _ = qa.shape
    P = kc.shape[1]
    n_keys = -(-(P + T) // LANES) * LANES
    n_mem = mk.shape[1]
    blk = lambda r, n: pl.BlockSpec((1, r, n), lambda b: (b, 0, 0))
    return pl.pallas_call(
        _band_sample_kernel,
        grid=(B,),
        in_specs=[blk(T, D_A), blk(P, D_A), blk(P, D_A), blk(T, D_A), blk(T, D_A), blk(T, D_A),
                  blk(T, D_M), blk(T, D_M), blk(n_mem, D_M), blk(n_mem, D_M),
                  pl.BlockSpec((H_A, BIAS_VEC), lambda b: (0, 0))],
        out_specs=[blk(T, D_A), blk(T, D_M)],
        out_shape=[jax.ShapeDtypeStruct((B, T, D_A), jnp.bfloat16),
                   jax.ShapeDtypeStruct((B, T, D_M), jnp.bfloat16)],
        scratch_shapes=[pltpu.VMEM((n_keys, D_A), jnp.bfloat16)] * 2
        + [pltpu.VMEM((H_A, T, n_keys), jnp.float32)],
        compiler_params=_cparams(1),
        name="band_sample",
    )(qa, kc, vc, kn, vn, ga, qm, gm, mk, mv, bias_vec)


def _key_to_f32(k):
    bits = k ^ ((k >> 31) & jnp.int32(0x7FFFFFFF))
    return lax.bitcast_convert_type(bits, jnp.float32)


def _count(pred_f32):
    ones = jnp.ones((8, pred_f32.shape[0]), jnp.bfloat16)
    c = jnp.dot(ones, pred_f32.astype(jnp.bfloat16), preferred_element_type=jnp.float32)
    return c[0:1, :]


def _dsa_select(sc_ref, probs, tq, topk):
    one, zero = jnp.float32(1.0), jnp.float32(0.0)
    live = [(off, tk) for off, tk in probs if tk > topk]
    n_live = len(live)

    def sc_of(n):
        off, tk = live[n]
        return sc_ref[off:off + tk, :]

    if live:
        def thr_step(i, ts):
            bit = lax.shift_left(jnp.int32(1), 31 - i)
            out = []
            for n in range(n_live):
                cand = ts[n] + bit
                c = _count(jnp.where(sc_of(n) >= _key_to_f32(cand), one, zero))
                out.append(jnp.where(c >= topk, cand, ts[n]))
            return tuple(out)

        start = tuple(jnp.full((1, tq), _INT_MIN, jnp.int32) for _ in range(n_live))
        t_keys = lax.fori_loop(0, 32, thr_step, start)
        thrs = [jnp.where(t == _INT_MIN, _NEG_INF, _key_to_f32(t)) for t in t_keys]

        for n in range(n_live):
            off, tk = live[n]
            c_ge = _count(jnp.where(sc_of(n) >= thrs[n], one, zero))

            def tie_search(n=n, off=off, tk=tk):
                need = topk - _count(jnp.where(sc_of(n) > thrs[n], one, zero))
                tile = 2 * LANES

                def step(first, size, state):
                    before, jmax = state
                    r = lax.broadcasted_iota(jnp.int32, (size, size), 0)
                    c = lax.broadcasted_iota(jnp.int32, (size, size), 1)
                    lower = jnp.where(c <= r, one, zero).astype(jnp.bfloat16)
                    tied = jnp.where(sc_ref[pl.ds(off + first, size), :] == thrs[n], one, zero)
                    rank = before + jnp.dot(lower, tied.astype(jnp.bfloat16),
                                            preferred_element_type=jnp.float32)
                    kidx = (lax.broadcasted_iota(jnp.int32, (size, tq), 0) + first).astype(jnp.float32)
                    taken = jnp.where(tied > 0, jnp.where(rank <= need, kidx, -one), -one)
                    jmax = jnp.maximum(jmax, jnp.max(taken, axis=0, keepdims=True))
                    return rank[size - 1:size, :], jmax

                state = (jnp.zeros((1, tq), jnp.float32), jnp.full((1, tq), -1.0, jnp.float32))
                n_full, rest = divmod(tk, tile)
                state = lax.fori_loop(
                    0, n_full, lambda t, st: step(pl.multiple_of(t * tile, tile), tile, st), state)
                if rest:
                    state = step(n_full * tile, rest, state)
                return state[1].astype(jnp.int32)

            def keep_with_ties(n=n, off=off, tk=tk, tie_search=tie_search):
                jmax = tie_search()
                sc = sc_of(n)
                kidx = lax.broadcasted_iota(jnp.int32, (tk, tq), 0)
                fin = jnp.where(jnp.abs(sc) < jnp.float32(jnp.inf), zero, _NEG_INF)
                sc_ref[off:off + tk, :] = jnp.where(
                    sc > thrs[n], fin,
                    jnp.where(sc == thrs[n], jnp.where(kidx <= jmax, fin, _NEG_INF), _NEG_INF))

            def keep_plain(n=n, off=off, tk=tk):
                sc = sc_of(n)
                fin = jnp.where(jnp.abs(sc) < jnp.float32(jnp.inf), zero, _NEG_INF)
                sc_ref[off:off + tk, :] = jnp.where(sc >= thrs[n], fin, _NEG_INF)

            has_excess = jnp.max(jnp.where(c_ge > topk, one, zero)) > 0
            lax.cond(has_excess, keep_with_ties, keep_plain)

    for off, tk in probs:
        if tk <= topk:
            sc = sc_ref[off:off + tk, :]
            sc_ref[off:off + tk, :] = jnp.where(jnp.abs(sc) < jnp.float32(jnp.inf), zero, _NEG_INF)


def _dsa_scores(limit, qi, kiwi, ki4):
    tk, tq = ki4.shape[0], qi.shape[0]
    kiwi_t = kiwi.T
    acc = jnp.zeros((tk, tq), jnp.float32)
    heads_per_slab = LANES // D_IDX

    def head_q(h):
        slab = qi[:, (h // heads_per_slab) * LANES:(h // heads_per_slab + 1) * LANES]
        lo = (h % heads_per_slab) * D_IDX
        return _keep_lanes(slab, lo, lo + D_IDX)

    for h in range(0, H_IDX, 2):
        d = lax.dot_general(ki4, jnp.concatenate([head_q(h), head_q(h + 1)], axis=0), _NT,
                            preferred_element_type=jnp.float32)
        for e in range(2):
            w = kiwi_t[D_IDX + h + e:D_IDX + h + e + 1, :] * ((D_IDX ** -0.5) * (H_IDX ** -0.5))
            acc = acc + w * jnp.maximum(d[:, e * tq:(e + 1) * tq], 0.0)
    if limit is not None:
        kidx = lax.broadcasted_iota(jnp.int32, (tk, tq), 0)
        acc = jnp.where(kidx < limit, acc, _NEG_INF)
    return acc


def _kq_attend(bias_of, q, g, k_slab_of, vt_slab_of, store):
    tq = q.shape[0]
    for p in range(q.shape[1] // LANES):
        sl = slice(p * LANES, (p + 1) * LANES)
        q_slab = q[:, sl]
        q_pair = jnp.concatenate([_keep_lanes(q_slab, 0, D_HEAD), _keep_lanes(q_slab, D_HEAD, LANES)],
                                 axis=0)
        s_pair = lax.dot_general(k_slab_of(sl), q_pair, _NT, preferred_element_type=jnp.float32)
        probs, inv_l = [], []
        for hh in range(2):
            s = s_pair[:, hh * tq:(hh + 1) * tq]
            bias = bias_of(2 * p + hh)
            if bias is not None:
                s = s + bias
            m = jnp.max(s, axis=0, keepdims=True)
            pr = jnp.exp2(s - m)
            inv_l.append(1.0 / jnp.sum(pr, axis=0, keepdims=True))
            probs.append(pr.astype(jnp.bfloat16))
        o_pair = jnp.dot(vt_slab_of(sl), jnp.concatenate(probs, axis=1),
                         preferred_element_type=jnp.float32)
        ot = jnp.concatenate([o_pair[:D_HEAD, :tq] * inv_l[0], o_pair[D_HEAD:, tq:] * inv_l[1]], axis=0)
        store(sl, (ot.T * _silu(g[:, sl])).astype(jnp.bfloat16))


def _dsa_blocks(S, tq):
    return [(tq * j * (j + 1) // 2, (j + 1) * tq) for j in range(S // tq)]


def _own_region():
    return pl.when(pl.program_id(0) >= 0)


def _dsa_select_kernel(qi_ref, kiwi_ref, ki4_ref, keep_ref, sc_ref, *, topk, tq):
    own_region = _own_region()
    probs = _dsa_blocks(ki4_ref.shape[1], tq)
    for j, (off, tk) in enumerate(probs):
        rows = slice(j * tq, (j + 1) * tq)

        @own_region
        def _(j=j, tk=tk, off=off, rows=rows):
            qpos = j * tq + lax.broadcasted_iota(jnp.int32, (1, tq), 1)
            limit = ((qpos >> 6) + 1) << 6
            sc_ref[off:off + tk, :] = _dsa_scores(limit, qi_ref[0, rows, :], kiwi_ref[0, rows, :],
                                                  ki4_ref[0, :tk, :])

    _dsa_select(sc_ref, probs, tq, topk)
    keep_ref[0] = sc_ref[...].astype(jnp.bfloat16)


def _dsa_attend_kernel(q_ref, g_ref, k_ref, v_ref, keep_ref, o_ref, vt, *, tq):
    own_region = _own_region()
    S = k_ref.shape[1]
    for t in range(S // tq):
        vt[t] = v_ref[0, t * tq:(t + 1) * tq, :].astype(jnp.float32).T.astype(jnp.bfloat16)
    for j, (off, tk) in enumerate(_dsa_blocks(S, tq)):
        rows = slice(j * tq, (j + 1) * tq)

        @own_region
        def _(j=j, tk=tk, off=off, rows=rows):
            def store(sl, o):
                o_ref[0, rows, sl] = o

            bias = keep_ref[0, off:off + tk, :].astype(jnp.float32)
            _kq_attend(lambda h: bias, q_ref[0, rows, :], g_ref[0, rows, :],
                       lambda sl: k_ref[0, :tk, sl],
                       lambda sl: jnp.concatenate([vt[t, sl, :] for t in range(j + 1)], axis=1), store)


def _dsa_prompt_call(qb, qi, kiwi, gb, kb, vb, ki4):
    B, S, _ = qb.shape
    tq = DSA_TQ
    nq = S // tq
    assert CHUNK == 64 and nq * tq == S
    topk = min(TOPK_MAX, S // 4)
    n_rows = tq * nq * (nq + 1) // 2
    full = lambda n: pl.BlockSpec((1, S, n), lambda b: (b, 0, 0))
    keep_spec = pl.BlockSpec((1, n_rows, tq), lambda b: (b, 0, 0))
    keep = pl.pallas_call(
        functools.partial(_dsa_select_kernel, topk=topk, tq=tq),
        grid=(B,),
        in_specs=[full(H_IDX * D_IDX), full(LANES), full(LANES)],
        out_specs=keep_spec,
        out_shape=jax.ShapeDtypeStruct((B, n_rows, tq), jnp.bfloat16),
        scratch_shapes=[pltpu.VMEM((n_rows, tq), jnp.float32)],
        compiler_params=_cparams(1),
        name="dsa_select",
    )(qi, kiwi, ki4)
    return pl.pallas_call(
        functools.partial(_dsa_attend_kernel, tq=tq),
        grid=(B,),
        in_specs=[full(D_B), full(D_B), full(D_B), full(D_B), keep_spec],
        out_specs=full(D_B),
        out_shape=jax.ShapeDtypeStruct((B, S, D_B), jnp.bfloat16),
        scratch_shapes=[pltpu.VMEM((nq, D_B, tq), jnp.bfloat16)],
        compiler_params=_cparams(1),
        name="dsa_attend",
    )(qb, gb, kb, vb, keep)


def _dsa_sample_kernel(q_ref, qi_ref, kiwi_ref, g_ref, kc_ref, vc_ref, kic_ref,
                       kn_ref, vn_ref, kin_ref, o_ref, kbf, vt, ki4s, sc_ref, *, topk):
    P = kc_ref.shape[1]
    T = kn_ref.shape[1]
    tq = q_ref.shape[1]
    kbf[:P, :] = kc_ref[0].astype(jnp.bfloat16)
    kbf[P:, :] = kn_ref[0].astype(jnp.bfloat16)
    vt[:, :P] = vc_ref[0].T.astype(jnp.bfloat16)
    vn_tile = jnp.concatenate([vn_ref[0], jnp.zeros((LANES - T, D_B), jnp.float32)], axis=0)
    vt[:, P:] = vn_tile.T[:, :T].astype(jnp.bfloat16)
    ki4s[:P, :] = kic_ref[0]
    ki4s[P:, :] = kin_ref[0]
    sc_ref[...] = _dsa_scores(None, qi_ref[0], kiwi_ref[0], ki4s[...])
    _dsa_select(sc_ref, [(0, P + T)], tq, topk)

    def store(sl, o):
        o_ref[0, :, sl] = o

    _kq_attend(lambda h: sc_ref[...], q_ref[0], g_ref[0], lambda sl: kbf[:, sl], lambda sl: vt[sl, :],
               store)


def _dsa_sample_call(qb, qi, kiwi, gb, kc, vc, kic, kn, vn, kin):
    B, tq, _ = qb.shape
    P, T = kc.shape[1], kn.shape[1]
    tk = P + T
    topk = min(TOPK_MAX, tk // 4)
    blk = lambda r, n: pl.BlockSpec((1, r, n), lambda b: (b, 0, 0))
    return pl.pallas_call(
        functools.partial(_dsa_sample_kernel, topk=topk),
        grid=(B,),
        in_specs=[blk(tq, D_B), blk(tq, H_IDX * D_IDX), blk(tq, LANES), blk(tq, D_B),
                  blk(P, D_B), blk(P, D_B), blk(P, LANES),
                  blk(T, D_B), blk(T, D_B), blk(T, LANES)],
        out_specs=blk(tq, D_B),
        out_shape=jax.ShapeDtypeStruct((B, tq, D_B), jnp.bfloat16),
        scratch_shapes=[pltpu.VMEM((tk, D_B), jnp.bfloat16),
                        pltpu.VMEM((D_B, tk), jnp.bfloat16),
                        pltpu.VMEM((tk, LANES), jnp.bfloat16),
                        pltpu.VMEM((tk, tq), jnp.float32)],
        compiler_params=_cparams(1),
        name="dsa_sample",
    )(qb, qi, kiwi, gb, kc, vc, kic, kn, vn, kin)


def _merge_kernel(x_ref, oa_ref, ob_ref, om_ref, w_ref, g_ref, y_ref, *, last):
    acc = x_ref[...]
    acc = acc + jnp.dot(oa_ref[...], w_ref[:D_A, :], preferred_element_type=jnp.float32)
    acc = acc + jnp.dot(ob_ref[...], w_ref[D_A:D_A + D_B, :], preferred_element_type=jnp.float32)
    acc = acc + jnp.dot(om_ref[...], w_ref[D_A + D_B:, :], preferred_element_type=jnp.float32)
    y_ref[...] = _rms_scale(acc, g_ref[...]) if last else acc


def _merge_call(x2d, oa, ob, om, w, g, last):
    R, D = x2d.shape
    tm = min(PROJ_TM, R)
    assert R % tm == 0
    row = lambda n: pl.BlockSpec((tm, n), lambda i: (i, 0))
    return pl.pallas_call(
        functools.partial(_merge_kernel, last=last),
        grid=(R // tm,),
        in_specs=[row(D), row(D_A), row(D_B), row(D_M),
                  pl.BlockSpec((D_A + D_B + D_M, D), lambda i: (0, 0)),
                  pl.BlockSpec((1, D), lambda i: (0, 0))],
        out_specs=row(D),
        out_shape=jax.ShapeDtypeStruct((R, D), jnp.float32),
        compiler_params=_cparams(1),
        name="merge",
    )(x2d, oa, ob, om, w, g)


def _pack_w_in(w):
    D = w.shape[0]
    ki = w[:, C_KIWI:C_KIWI + D_IDX]
    wi = w[:, C_KIWI + D_IDX:C_KIWI + D_IDX + H_IDX]
    pad = jnp.zeros((D, LANES - D_IDX - H_IDX), w.dtype)
    packed = jnp.concatenate([w[:, :C_KIWI], ki, wi, pad, jnp.tile(ki, (1, LANES // D_IDX))], axis=1)
    return packed.astype(jnp.bfloat16)


def _rope_tables(pos):
    posf = pos.astype(jnp.float32)

    def tables(d):
        half = d // 2
        inv_freq = ROPE_THETA ** (-jnp.arange(half, dtype=jnp.float32) * 2.0 / d)
        ang = posf[:, None] * inv_freq[None, :]
        cos, sin = jnp.cos(ang), jnp.sin(ang)
        return jnp.concatenate([cos, cos], axis=1), jnp.concatenate([-sin, sin], axis=1)

    cos64, sin64 = tables(D_HEAD)
    cos32, sin32 = tables(D_IDX)
    n = pos.shape[0]
    rest = LANES - D_IDX
    cos_kw = jnp.concatenate([cos32, jnp.ones((n, rest), jnp.float32)], axis=1)
    sin_kw = jnp.concatenate([sin32, jnp.zeros((n, rest), jnp.float32)], axis=1)
    rep = lambda t, d: jnp.tile(t, (1, LANES // d))
    return jnp.stack([rep(cos64, D_HEAD), rep(sin64, D_HEAD), rep(cos32, D_IDX), rep(sin32, D_IDX),
                      cos_kw, sin_kw])


def _band_bias_vec(table):
    n_far = A_WINDOW - REL_CLIP + 1
    lo_idx = A_WINDOW + REL_CLIP - (BAND_WIN - 1)
    assert lo_idx >= 0 and BIAS_VEC >= BAND_WIN + BAND_TQ - 1
    far = table[:, 2 * REL_CLIP:2 * REL_CLIP + 1].astype(jnp.float32)
    mid = table[:, lo_idx:2 * REL_CLIP][:, ::-1].astype(jnp.float32)
    vec = jnp.concatenate([jnp.tile(far, (1, n_far)), mid, jnp.tile(far, (1, BIAS_VEC - BAND_WIN))], axis=1)
    rvec = jnp.concatenate([vec[:, :1], vec[:, 1:][:, ::-1]], axis=1)
    return vec, rvec


def kernel(x_prompt, x_sample, mem_prompt, cache_a_k, cache_a_v, cache_b_k, cache_b_v, cache_b_kidx,
           cache_mem_k, cache_mem_v, norm_mix_g, w_in, rel_bias_a, norm_mem_g, w_mem_kv, w_out,
           norm_final_g):
    B, S, D = x_prompt.shape
    Bs, T, _ = x_sample.shape
    depth = w_in.shape[0]
    P = cache_b_k.shape[2]
    Pa = cache_a_k.shape[2]
    n_mem = mem_prompt.shape[1]
    keep = min(A_WINDOW, S)
    assert Pa == A_WINDOW and T <= CHUNK and LANES % T == 0

    rope_p = _rope_tables(jnp.arange(S))
    rope_s = jnp.tile(_rope_tables(P + jnp.arange(T)), (1, Bs, 1))
    g_final = norm_final_g.reshape(1, D)

    xp, xs = x_prompt, x_sample
    outs_p = [[] for _ in range(7)]
    outs_s = [[] for _ in range(5)]
    for l in range(depth):
        last = l == depth - 1
        w = _pack_w_in(w_in[l])
        g_mix = norm_mix_g[l].reshape(1, D)
        w_o = w_out[l].astype(jnp.bfloat16)
        bias_vec, bias_rvec = _band_bias_vec(rel_bias_a[l])

        (qa, ka, va, ga, qb, kb, vb, gb, qm, gm, qi, kiwi, ki4, ki, kbb, vbb, ak, av) = _proj_call(
            xp, g_mix, w, rope_p, keep)
        mk, mv = _memkv_call(mem_prompt.reshape(B * n_mem, D), norm_mem_g[l].reshape(1, D),
                             w_mem_kv[l].astype(jnp.bfloat16))
        mk = mk.reshape(B, n_mem, D_M)
        mv = mv.reshape(B, n_mem, D_M)
        oa, om = _band_prompt_call(qa, ka, va, ga, qm, gm, mk, mv, bias_rvec)
        ob = _dsa_prompt_call(qb, qi, kiwi, gb, kbb, vbb, ki4)
        xp = _merge_call(xp.reshape(B * S, D), oa.reshape(B * S, D_A), ob.reshape(B * S, D_B),
                         om.reshape(B * S, D_M), w_o, g_final, last).reshape(B, S, D)
        for lst, t in zip(outs_p, (ak.reshape(B, keep, H_A, D_HEAD), av.reshape(B, keep, H_A, D_HEAD),
                                   kb.reshape(B, S, H_B, D_HEAD), vb.reshape(B, S, H_B, D_HEAD), ki,
                                   mk.reshape(B, n_mem, H_M, D_HEAD), mv.reshape(B, n_mem, H_M, D_HEAD))):
            lst.append(t)

        (qa, ka, va, ga, qb, kb, vb, gb, qm, gm, qi, kiwi, ki4, ki, kbb, vbb, ak, av) = _proj_call(
            xs.reshape(1, Bs * T, D), g_mix, w, rope_s, Bs * T)
        per_b = lambda t: t.reshape(Bs, T, t.shape[-1])
        oa, om = _band_sample_call(
            per_b(qa), cache_a_k[l].reshape(Bs, Pa, D_A), cache_a_v[l].reshape(Bs, Pa, D_A),
            per_b(ka), per_b(va), per_b(ga), per_b(qm), per_b(gm),
            cache_mem_k[l].reshape(Bs, n_mem, D_M), cache_mem_v[l].reshape(Bs, n_mem, D_M), bias_vec)
        rep = lambda t: jnp.tile(per_b(t), (1, LANES // T, 1))
        kic = jnp.tile(cache_b_kidx[l], (1, 1, LANES // D_IDX)).astype(jnp.bfloat16)
        ob = _dsa_sample_call(rep(qb), rep(qi), rep(kiwi), rep(gb),
                              cache_b_k[l].reshape(Bs, P, D_B), cache_b_v[l].reshape(Bs, P, D_B), kic,
                              per_b(kb), per_b(vb), per_b(ki4))[:, :T]
        xs = _merge_call(xs.reshape(Bs * T, D), oa.reshape(Bs * T, D_A), ob.reshape(Bs * T, D_B),
                         om.reshape(Bs * T, D_M), w_o, g_final, last).reshape(Bs, T, D)
        for lst, t in zip(outs_s, (ak.reshape(Bs, T, H_A, D_HEAD), av.reshape(Bs, T, H_A, D_HEAD),
                                   kb.reshape(Bs, T, H_B, D_HEAD), vb.reshape(Bs, T, H_B, D_HEAD),
                                   ki.reshape(Bs, T, D_IDX))):
            lst.append(t)

    st = lambda ts: jnp.stack(ts, axis=0)
    return (xp, xs) + tuple(st(t) for t in outs_p) + tuple(st(t) for t in outs_s)
```

```python
import functools
import math

import jax
import jax.numpy as jnp
from jax import lax
from jax.experimental import pallas as pl
from jax.experimental.pallas import tpu as pltpu

CHUNK = 64
D_HEAD = 64
H_A = 6
H_B = 6
H_M = 4
D_A = H_A * D_HEAD
D_B = H_B * D_HEAD
D_M = H_M * D_HEAD
A_LEFT_CHUNKS = 8
A_WINDOW = A_LEFT_CHUNKS * CHUNK
REL_CLIP = 256
H_IDX = 8
D_IDX = 32
TOPK_MAX = 256
ROPE_THETA = 10000.0
EPS = 1e-6

LANES = 128
VMEM_LIMIT_BYTES = 56 * 1024 * 1024

C_QA, C_KA, C_VA, C_GA = 0, D_A, 2 * D_A, 3 * D_A
C_QB = 4 * D_A
C_KB, C_VB, C_GB = C_QB + D_B, C_QB + 2 * D_B, C_QB + 3 * D_B
C_QM = C_QB + 4 * D_B
C_GM = C_QM + D_M
C_QI = C_GM + D_M
C_KIWI = C_QI + H_IDX * D_IDX
C_KI4 = C_KIWI + LANES
W_COLS = C_KI4 + LANES

PROJ_TM = 512
MERGE_TM = 1024
BAND_TQ = 256
BAND_WIN = A_WINDOW + BAND_TQ
DSA_TQ = 256

LOG2E = math.log2(math.e)
QK_SCALE = (D_HEAD ** -0.5) * LOG2E
BIAS_VEC = 8 * LANES

_NT = (((1,), (1,)), ((), ()))
_INT_MIN = -2 ** 31
_NEG_INF = float("-inf")


def _cparams(n_axes):
    return pltpu.CompilerParams(
        dimension_semantics=("arbitrary",) * n_axes,
        vmem_limit_bytes=VMEM_LIMIT_BYTES)


def _silu(g):
    return g * (1.0 / (1.0 + jnp.exp(-g)))


def _rms_scale(x, g):
    ms = jnp.mean(x * x, axis=-1, keepdims=True)
    return (x * lax.rsqrt(ms + EPS)) * g


def _rope_slab(z, cos, sin, half):
    lane = lax.broadcasted_iota(jnp.int32, z.shape, 1)
    first = (lane & (2 * half - 1)) < half
    partner = jnp.where(first, pltpu.roll(z, LANES - half, 1), pltpu.roll(z, half, 1))
    return z * cos + partner * sin


def _proj_kernel(x_ref, g_ref, w_ref, rope_ref,
                 qa_ref, ka_ref, va_ref, ga_ref, qb_ref, kb_ref, vb_ref, gb_ref,
                 qm_ref, gm_ref, qi_ref, kiwi_ref, ki4_ref, ki_ref, kbb_ref, vbb_ref, ak_ref, av_ref,
                 *, n_tiles, keep_tiles):
    i = pl.program_id(1)
    xn = _rms_scale(x_ref[0], g_ref[...]).astype(jnp.bfloat16)

    def proj(c0, n):
        return jnp.dot(xn, w_ref[:, c0:c0 + n], preferred_element_type=jnp.float32)

    scale = QK_SCALE
    z = proj(C_QA, 2 * D_A)
    qa_ref[0] = (z[:, :D_A] * scale).astype(jnp.bfloat16)
    ka = z[:, D_A:]
    ka_ref[0] = ka.astype(jnp.bfloat16)
    z = proj(C_VA, 2 * D_A)
    va = z[:, :D_A]
    va_ref[0] = va.astype(jnp.bfloat16)
    ga_ref[0] = _silu(z[:, D_A:]).astype(jnp.bfloat16)

    @pl.when(i >= n_tiles - keep_tiles)
    def _():
        ak_ref[0] = ka
        av_ref[0] = va

    cos64, sin64 = rope_ref[0], rope_ref[1]
    cos32, sin32 = rope_ref[2], rope_ref[3]
    cos_kw, sin_kw = rope_ref[4], rope_ref[5]
    z = proj(C_QB, 2 * D_B)
    for s in range(D_B // LANES):
        sl = slice(s * LANES, (s + 1) * LANES)
        qb = _rope_slab(z[:, sl], cos64, sin64, D_HEAD // 2)
        qb_ref[0, :, sl] = (qb * scale).astype(jnp.bfloat16)
        kb = _rope_slab(z[:, D_B + s * LANES:D_B + (s + 1) * LANES], cos64, sin64, D_HEAD // 2)
        kb_ref[0, :, sl] = kb
        kbb_ref[0, :, sl] = kb.astype(jnp.bfloat16)
    z = proj(C_VB, 2 * D_B)
    vb_ref[0] = z[:, :D_B]
    vbb_ref[0] = z[:, :D_B].astype(jnp.bfloat16)
    gb_ref[0] = _silu(z[:, D_B:]).astype(jnp.bfloat16)
    qm_ref[0] = (proj(C_QM, D_M) * scale).astype(jnp.bfloat16)
    gm_ref[0] = _silu(proj(C_GM, D_M)).astype(jnp.bfloat16)
    z = proj(C_QI, H_IDX * D_IDX)
    for s in range(H_IDX * D_IDX // LANES):
        sl = slice(s * LANES, (s + 1) * LANES)
        qi_ref[0, :, sl] = _rope_slab(z[:, sl], cos32, sin32, D_IDX // 2).astype(jnp.bfloat16)
    z = proj(C_KIWI, 2 * LANES)
    kiwi = _rope_slab(z[:, :LANES], cos_kw, sin_kw, D_IDX // 2)
    kiwi_ref[0] = kiwi
    ki_ref[0] = kiwi[:, :D_IDX]
    ki4_ref[0] = _rope_slab(z[:, LANES:], cos32, sin32, D_IDX // 2).astype(jnp.bfloat16)


def _proj_call(x, g, w, rope, keep_rows):
    B, S, D = x.shape
    tm = min(PROJ_TM, S)
    n_tiles = S // tm
    keep_tiles = keep_rows // tm
    assert n_tiles * tm == S and keep_tiles * tm == keep_rows

    def tile(n, dtype):
        return (jax.ShapeDtypeStruct((B, S, n), dtype),
                pl.BlockSpec((1, tm, n), lambda b, i: (b, i, 0)))

    f32, bf16 = jnp.float32, jnp.bfloat16
    outs = [tile(D_A, bf16), tile(D_A, bf16), tile(D_A, bf16), tile(D_A, bf16),
            tile(D_B, bf16), tile(D_B, f32), tile(D_B, f32), tile(D_B, bf16),
            tile(D_M, bf16), tile(D_M, bf16), tile(H_IDX * D_IDX, bf16),
            tile(LANES, f32), tile(LANES, bf16), tile(D_IDX, f32), tile(D_B, bf16), tile(D_B, bf16)]
    keep_spec = pl.BlockSpec(
        (1, tm, D_A), lambda b, i: (b, jnp.maximum(i - (n_tiles - keep_tiles), 0), 0))
    outs += [(jax.ShapeDtypeStruct((B, keep_rows, D_A), f32), keep_spec)] * 2
    return pl.pallas_call(
        functools.partial(_proj_kernel, n_tiles=n_tiles, keep_tiles=keep_tiles),
        grid=(B, n_tiles),
        in_specs=[pl.BlockSpec((1, tm, D), lambda b, i: (b, i, 0)),
                  pl.BlockSpec((1, D), lambda b, i: (0, 0)),
                  pl.BlockSpec((D, W_COLS), lambda b, i: (0, 0)),
                  pl.BlockSpec((6, tm, LANES), lambda b, i: (0, i, 0))],
        out_specs=[o[1] for o in outs],
        out_shape=[o[0] for o in outs],
        compiler_params=_cparams(2),
        name="proj",
    )(x, g, w, rope)


def _memkv_kernel(m_ref, g_ref, w_ref, mk_ref, mv_ref):
    xn = _rms_scale(m_ref[...], g_ref[...]).astype(jnp.bfloat16)
    mk_ref[...] = jnp.dot(xn, w_ref[:, :D_M], preferred_element_type=jnp.float32)
    mv_ref[...] = jnp.dot(xn, w_ref[:, D_M:], preferred_element_type=jnp.float32)


def _memkv_call(mem2d, g, w):
    R, D = mem2d.shape
    tm = min(PROJ_TM, R)
    assert R % tm == 0
    row = lambda n: pl.BlockSpec((tm, n), lambda i: (i, 0))
    return pl.pallas_call(
        _memkv_kernel,
        grid=(R // tm,),
        in_specs=[row(D), pl.BlockSpec((1, D), lambda i: (0, 0)),
                  pl.BlockSpec((D, 2 * D_M), lambda i: (0, 0))],
        out_specs=[row(D_M), row(D_M)],
        out_shape=[jax.ShapeDtypeStruct((R, D_M), jnp.float32)] * 2,
        compiler_params=_cparams(1),
        name="memkv",
    )(mem2d, g, w)


def _softmax_rows_pv(s, v):
    m = jnp.max(s, axis=1, keepdims=True)
    p = jnp.exp2(s - m)
    l = jnp.sum(p, axis=1, keepdims=True)
    o = jnp.dot(p.astype(jnp.bfloat16), v, preferred_element_type=jnp.float32)
    return o * (1.0 / l)


def _keep_lanes(slab, lo, hi):
    x = slab.astype(jnp.float32)
    lane = lax.broadcasted_iota(jnp.int32, x.shape, 1)
    return jnp.where(lane >= lo, jnp.where(lane < hi, x, 0.0), 0.0).astype(jnp.bfloat16)


def _pair_attend(q_slab, k_slab, v_slab, bias_fn):
    lane = lax.broadcasted_iota(jnp.int32, q_slab.shape, 1)
    out = None
    for hh in range(2):
        s = lax.dot_general(_keep_lanes(q_slab, hh * D_HEAD, (hh + 1) * D_HEAD), k_slab, _NT,
                            preferred_element_type=jnp.float32)
        o = _softmax_rows_pv(bias_fn(s, hh), v_slab)
        out = o if hh == 0 else jnp.where(lane < D_HEAD, out, o)
    return out


def _mem_attend(qm_ref, gm_ref, mk_ref, mv_ref, om_ref):
    for p in range(D_M // LANES):
        sl = slice(p * LANES, (p + 1) * LANES)
        o = _pair_attend(qm_ref[0, :, sl], mk_ref[0, :, sl].astype(jnp.bfloat16),
                         mv_ref[0, :, sl].astype(jnp.bfloat16), lambda s, hh: s)
        om_ref[0, :, sl] = (o * gm_ref[0, :, sl].astype(jnp.float32)).astype(jnp.bfloat16)


def _build_band_bias(vec_ref, bias_ref, n_q, n_valid, chunked):
    n_keys = bias_ref.shape[2]
    i = lax.broadcasted_iota(jnp.int32, (n_q, n_keys), 0)
    r = lax.broadcasted_iota(jnp.int32, (n_q, n_keys), 1)
    lo = ((i >> 6) << 6) if chunked else jnp.zeros_like(i)
    for h in range(bias_ref.shape[0]):
        base = jnp.broadcast_to(vec_ref[h:h + 1, :], (n_q, BIAS_VEC))
        rolled = pltpu.roll(base, 0, 1, stride=1, stride_axis=0)
        b = rolled[:, :n_keys] * LOG2E
        bias_ref[h] = jnp.where(r >= lo, jnp.where(r < lo + n_valid, b, _NEG_INF), _NEG_INF)


def _build_band_bias_t(rvec_ref, bias_ref):
    n_keys, n_q = bias_ref.shape[1], bias_ref.shape[2]
    r = lax.broadcasted_iota(jnp.int32, (n_keys, n_q), 0)
    i = lax.broadcasted_iota(jnp.int32, (n_keys, n_q), 1)
    lo = (i >> 6) << 6
    for h in range(bias_ref.shape[0]):
        base = jnp.broadcast_to(rvec_ref[h:h + 1, :], (n_keys, BIAS_VEC))
        rolled = pltpu.roll(base, 0, 1, stride=1, stride_axis=0)
        b = rolled[:, :n_q] * LOG2E
        bias_ref[h] = jnp.where(r >= lo, jnp.where(r < lo + A_WINDOW + CHUNK, b, _NEG_INF), _NEG_INF)


def _band_prompt_kernel(qa_ref, ka_ref, va_ref, ga_ref, qm_ref, gm_ref, mk_ref, mv_ref, rvec_ref,
                        oa_ref, om_ref, kpad, vt_blk, mk_bf, mv_t, bias_ref):
    j = pl.program_id(1)
    n_front = A_WINDOW // BAND_TQ
    n_win = BAND_WIN // BAND_TQ

    @pl.when(jnp.logical_and(pl.program_id(0) == 0, j == 0))
    def _():
        _build_band_bias_t(rvec_ref, bias_ref)

    @pl.when(j == 0)
    def _():
        kpad[:A_WINDOW, :] = jnp.zeros((A_WINDOW, D_A), jnp.bfloat16)
        kpad[A_WINDOW:, :] = ka_ref[0]
        for t in range(vt_blk.shape[0]):
            if t < n_front:
                vt_blk[t] = jnp.zeros((D_A, BAND_TQ), jnp.bfloat16)
            else:
                rows = slice((t - n_front) * BAND_TQ, (t - n_front + 1) * BAND_TQ)
                vt_blk[t] = va_ref[0, rows, :].astype(jnp.float32).T.astype(jnp.bfloat16)
        mk_bf[...] = mk_ref[0].astype(jnp.bfloat16)
        mv_t[...] = mv_ref[0].T.astype(jnp.bfloat16)

    start = pl.multiple_of(j * BAND_TQ, BAND_TQ)

    def store_a(sl, o):
        oa_ref[0, :, sl] = o

    def store_m(sl, o):
        om_ref[0, :, sl] = o

    def run(mask_front):
        if mask_front:
            pos = lax.broadcasted_iota(jnp.int32, (BAND_WIN, BAND_TQ), 0) + j * BAND_TQ
            front = jnp.where(pos >= A_WINDOW, 0.0, _NEG_INF)
        _kq_attend(lambda h: bias_ref[h] + front if mask_front else bias_ref[h],
                   qa_ref[0], ga_ref[0],
                   lambda sl: kpad[pl.ds(start, BAND_WIN), sl],
                   lambda sl: jnp.concatenate([vt_blk[j + u, sl, :] for u in range(n_win)], axis=1),
                   store_a)
        _kq_attend(lambda h: None, qm_ref[0], gm_ref[0], lambda sl: mk_bf[:, sl], lambda sl: mv_t[sl, :],
                   store_m)

    pl.when(j < n_front)(lambda: run(True))
    pl.when(j >= n_front)(lambda: run(False))


def _band_prompt_call(qa, ka, va, ga, qm, gm, mk, mv, bias_rvec):
    B, S, _ = qa.shape
    nq = S // BAND_TQ
    n_mem = mk.shape[1]
    assert nq * BAND_TQ == S and A_WINDOW % BAND_TQ == 0
    qblk = lambda n: pl.BlockSpec((1, BAND_TQ, n), lambda b, j: (b, j, 0))
    full = lambda r, n: pl.BlockSpec((1, r, n), lambda b, j: (b, 0, 0))
    return pl.pallas_call(
        _band_prompt_kernel,
        grid=(B, nq),
        in_specs=[qblk(D_A), full(S, D_A), full(S, D_A), qblk(D_A), qblk(D_M), qblk(D_M),
                  full(n_mem, D_M), full(n_mem, D_M),
                  pl.BlockSpec((H_A, BIAS_VEC), lambda b, j: (0, 0))],
        out_specs=[qblk(D_A), qblk(D_M)],
        out_shape=[jax.ShapeDtypeStruct((B, S, D_A), jnp.bfloat16),
                   jax.ShapeDtypeStruct((B, S, D_M), jnp.bfloat16)],
        scratch_shapes=[pltpu.VMEM((S + A_WINDOW, D_A), jnp.bfloat16),
                        pltpu.VMEM((nq + A_WINDOW // BAND_TQ, D_A, BAND_TQ), jnp.bfloat16),
                        pltpu.VMEM((n_mem, D_M), jnp.bfloat16),
                        pltpu.VMEM((D_M, n_mem), jnp.bfloat16),
                        pltpu.VMEM((H_A, BAND_WIN, BAND_TQ), jnp.float32)],
        compiler_params=_cparams(2),
        name="band_prompt",
    )(qa, ka, va, ga, qm, gm, mk, mv, bias_rvec)


def _band_sample_kernel(qa_ref, kc_ref, vc_ref, kn_ref, vn_ref, ga_ref, qm_ref, gm_ref,
                        mk_ref, mv_ref, vec_ref, oa_ref, om_ref, kcat, vcat, bias_ref):
    P = kc_ref.shape[1]
    T = kn_ref.shape[1]
    pad = kcat.shape[0] - P - T

    @pl.when(pl.program_id(0) == 0)
    def _():
        _build_band_bias(vec_ref, bias_ref, T, P + T, False)

    kcat[:P, :] = kc_ref[0].astype(jnp.bfloat16)
    vcat[:P, :] = vc_ref[0].astype(jnp.bfloat16)
    kcat[P:P + T, :] = kn_ref[0]
    vcat[P:P + T, :] = vn_ref[0]
    zeros = jnp.zeros((pad, D_A), jnp.bfloat16)
    kcat[P + T:, :] = zeros
    vcat[P + T:, :] = zeros
    for p in range(D_A // LANES):
        sl = slice(p * LANES, (p + 1) * LANES)
        o = _pair_attend(qa_ref[0, :, sl], kcat[:, sl], vcat[:, sl],
                         lambda s, hh, p=p: s + bias_ref[2 * p + hh])
        oa_ref[0, :, sl] = (o * ga_ref[0, :, sl].astype(jnp.float32)).astype(jnp.bfloat16)
    _mem_attend(qm_ref, gm_ref, mk_ref, mv_ref, om_ref)


def _band_sample_call(qa, kc, vc, kn, vn, ga, qm, gm, mk, mv, bias_vec):
    B, T, _ = qa.shape
    P = kc.shape[1]
    n_keys = -(-(P + T) // LANES) * LANES
    n_mem = mk.shape[1]
    blk = lambda r, n: pl.BlockSpec((1, r, n), lambda b: (b, 0, 0))
    return pl.pallas_call(
        _band_sample_kernel,
        grid=(B,),
        in_specs=[blk(T, D_A), blk(P, D_A), blk(P, D_A), blk(T, D_A), blk(T, D_A), blk(T, D_A),
                  blk(T, D_M), blk(T, D_M), blk(n_mem, D_M), blk(n_mem, D_M),
                  pl.BlockSpec((H_A, BIAS_VEC), lambda b: (0, 0))],
        out_specs=[blk(T, D_A), blk(T, D_M)],
        out_shape=[jax.ShapeDtypeStruct((B, T, D_A), jnp.bfloat16),
                   jax.ShapeDtypeStruct((B, T, D_M), jnp.bfloat16)],
        scratch_shapes=[pltpu.VMEM((n_keys, D_A), jnp.bfloat16)] * 2
        + [pltpu.VMEM((H_A, T, n_keys), jnp.float32)],
        compiler_params=_cparams(1),
        name="band_sample",
    )(qa, kc, vc, kn, vn, ga, qm, gm, mk, mv, bias_vec)


def _key_to_f32(k):
    bits = k ^ ((k >> 31) & jnp.int32(0x7FFFFFFF))
    return lax.bitcast_convert_type(bits, jnp.float32)


def _count(pred_f32):
    ones = jnp.ones((8, pred_f32.shape[0]), jnp.bfloat16)
    c = jnp.dot(ones, pred_f32.astype(jnp.bfloat16), preferred_element_type=jnp.float32)
    return c[0:1, :]


def _dsa_select(sc_ref, probs, tq, topk):
    one, zero = jnp.float32(1.0), jnp.float32(0.0)
    live = [(off, tk) for off, tk in probs if tk > topk]
    n_live = len(live)

    def sc_of(n):
        off, tk = live[n]
        return sc_ref[off:off + tk, :]

    if live:
        def thr_step(i, ts):
            bit = lax.shift_left(jnp.int32(1), 31 - i)
            out = []
            for n in range(n_live):
                cand = ts[n] + bit
                c = _count(jnp.where(sc_of(n) >= _key_to_f32(cand), one, zero))
                out.append(jnp.where(c >= topk, cand, ts[n]))
            return tuple(out)

        start = tuple(jnp.full((1, tq), _INT_MIN, jnp.int32) for _ in range(n_live))
        t_keys = lax.fori_loop(0, 32, thr_step, start)
        thrs = [jnp.where(t == _INT_MIN, _NEG_INF, _key_to_f32(t)) for t in t_keys]

        for n in range(n_live):
            off, tk = live[n]
            c_ge = _count(jnp.where(sc_of(n) >= thrs[n], one, zero))

            def tie_search(n=n, off=off, tk=tk):
                need = topk - _count(jnp.where(sc_of(n) > thrs[n], one, zero))
                tile = 2 * LANES

                def step(first, size, state):
                    before, jmax = state
                    r = lax.broadcasted_iota(jnp.int32, (size, size), 0)
                    c = lax.broadcasted_iota(jnp.int32, (size, size), 1)
                    lower = jnp.where(c <= r, one, zero).astype(jnp.bfloat16)
                    tied = jnp.where(sc_ref[pl.ds(off + first, size), :] == thrs[n], one, zero)
                    rank = before + jnp.dot(lower, tied.astype(jnp.bfloat16),
                                            preferred_element_type=jnp.float32)
                    kidx = (lax.broadcasted_iota(jnp.int32, (size, tq), 0) + first).astype(jnp.float32)
                    taken = jnp.where(tied > 0, jnp.where(rank <= need, kidx, -one), -one)
                    jmax = jnp.maximum(jmax, jnp.max(taken, axis=0, keepdims=True))
                    return rank[size - 1:size, :], jmax

                state = (jnp.zeros((1, tq), jnp.float32), jnp.full((1, tq), -1.0, jnp.float32))
                n_full, rest = divmod(tk, tile)
                state = lax.fori_loop(
                    0, n_full, lambda t, st: step(pl.multiple_of(t * tile, tile), tile, st), state)
                if rest:
                    state = step(n_full * tile, rest, state)
                return state[1].astype(jnp.int32)

            def keep_with_ties(n=n, off=off, tk=tk, tie_search=tie_search):
                jmax = tie_search()
                sc = sc_of(n)
                kidx = lax.broadcasted_iota(jnp.int32, (tk, tq), 0)
                fin = jnp.where(jnp.abs(sc) < jnp.float32(jnp.inf), zero, _NEG_INF)
                sc_ref[off:off + tk, :] = jnp.where(
                    sc > thrs[n], fin,
                    jnp.where(sc == thrs[n], jnp.where(kidx <= jmax, fin, _NEG_INF), _NEG_INF))

            def keep_plain(n=n, off=off, tk=tk):
                sc = sc_of(n)
                fin = jnp.where(jnp.abs(sc) < jnp.float32(jnp.inf), zero, _NEG_INF)
                sc_ref[off:off + tk, :] = jnp.where(sc >= thrs[n], fin, _NEG_INF)

            has_excess = jnp.max(jnp.where(c_ge > topk, one, zero)) > 0
            lax.cond(has_excess, keep_with_ties, keep_plain)

    for off, tk in probs:
        if tk <= topk:
            sc = sc_ref[off:off + tk, :]
            sc_ref[off:off + tk, :] = jnp.where(jnp.abs(sc) < jnp.float32(jnp.inf), zero, _NEG_INF)


def _dsa_scores(limit, qi, kiwi, ki4):
    tk, tq = ki4.shape[0], qi.shape[0]
    kiwi_t = kiwi.T
    acc = jnp.zeros((tk, tq), jnp.float32)
    heads_per_slab = LANES // D_IDX

    def head_q(h):
        slab = qi[:, (h // heads_per_slab) * LANES:(h // heads_per_slab + 1) * LANES]
        lo = (h % heads_per_slab) * D_IDX
        return _keep_lanes(slab, lo, lo + D_IDX)

    for h in range(0, H_IDX, 2):
        d = lax.dot_general(ki4, jnp.concatenate([head_q(h), head_q(h + 1)], axis=0), _NT,
                            preferred_element_type=jnp.float32)
        for e in range(2):
            w = kiwi_t[D_IDX + h + e:D_IDX + h + e + 1, :] * ((D_IDX ** -0.5) * (H_IDX ** -0.5))
            acc = acc + w * jnp.maximum(d[:, e * tq:(e + 1) * tq], 0.0)
    if limit is not None:
        kidx = lax.broadcasted_iota(jnp.int32, (tk, tq), 0)
        acc = jnp.where(kidx < limit, acc, _NEG_INF)
    return acc


def _kq_attend(bias_of, q, g, k_slab_of, vt_slab_of, store):
    tq = q.shape[0]
    for p in range(q.shape[1] // LANES):
        sl = slice(p * LANES, (p + 1) * LANES)
        q_slab = q[:, sl]
        q_pair = jnp.concatenate([_keep_lanes(q_slab, 0, D_HEAD), _keep_lanes(q_slab, D_HEAD, LANES)],
                                 axis=0)
        s_pair = lax.dot_general(k_slab_of(sl), q_pair, _NT, preferred_element_type=jnp.float32)
        probs, inv_l = [], []
        for hh in range(2):
            s = s_pair[:, hh * tq:(hh + 1) * tq]
            bias = bias_of(2 * p + hh)
            if bias is not None:
                s = s + bias
            m = jnp.max(s, axis=0, keepdims=True)
            pr = jnp.exp2(s - m)
            inv_l.append(1.0 / jnp.sum(pr, axis=0, keepdims=True))
            probs.append(pr.astype(jnp.bfloat16))
        o_pair = jnp.dot(vt_slab_of(sl), jnp.concatenate(probs, axis=1),
                         preferred_element_type=jnp.float32)
        ot = jnp.concatenate([o_pair[:D_HEAD, :tq] * inv_l[0], o_pair[D_HEAD:, tq:] * inv_l[1]], axis=0)
        store(sl, (ot.T * g[:, sl].astype(jnp.float32)).astype(jnp.bfloat16))


def _dsa_blocks(S, tq):
    return [(tq * j * (j + 1) // 2, (j + 1) * tq) for j in range(S // tq)]


def _own_region():
    return pl.when(pl.program_id(0) >= 0)


def _dsa_select_kernel(qi_ref, kiwi_ref, ki4_ref, keep_ref, sc_ref, *, topk, tq):
    own_region = _own_region()
    probs = _dsa_blocks(ki4_ref.shape[1], tq)
    for j, (off, tk) in enumerate(probs):
        rows = slice(j * tq, (j + 1) * tq)

        @own_region
        def _(j=j, tk=tk, off=off, rows=rows):
            qpos = j * tq + lax.broadcasted_iota(jnp.int32, (1, tq), 1)
            limit = ((qpos >> 6) + 1) << 6
            sc_ref[off:off + tk, :] = _dsa_scores(limit, qi_ref[0, rows, :], kiwi_ref[0, rows, :],
                                                  ki4_ref[0, :tk, :])

    _dsa_select(sc_ref, probs, tq, topk)
    keep_ref[0] = sc_ref[...].astype(jnp.bfloat16)


def _dsa_attend_kernel(q_ref, g_ref, k_ref, v_ref, keep_ref, o_ref, vt, *, tq):
    own_region = _own_region()
    S = k_ref.shape[1]
    for t in range(S // tq):
        vt[t] = v_ref[0, t * tq:(t + 1) * tq, :].astype(jnp.float32).T.astype(jnp.bfloat16)
    for j, (off, tk) in enumerate(_dsa_blocks(S, tq)):
        rows = slice(j * tq, (j + 1) * tq)

        @own_region
        def _(j=j, tk=tk, off=off, rows=rows):
            def store(sl, o):
                o_ref[0, rows, sl] = o

            bias = keep_ref[0, off:off + tk, :].astype(jnp.float32)
            _kq_attend(lambda h: bias, q_ref[0, rows, :], g_ref[0, rows, :],
                       lambda sl: k_ref[0, :tk, sl],
                       lambda sl: jnp.concatenate([vt[t, sl, :] for t in range(j + 1)], axis=1), store)


def _dsa_prompt_call(qb, qi, kiwi, gb, kb, vb, ki4):
    B, S, _ = qb.shape
    tq = DSA_TQ
    nq = S // tq
    assert CHUNK == 64 and nq * tq == S
    topk = min(TOPK_MAX, S // 4)
    n_rows = tq * nq * (nq + 1) // 2
    full = lambda n: pl.BlockSpec((1, S, n), lambda b: (b, 0, 0))
    keep_spec = pl.BlockSpec((1, n_rows, tq), lambda b: (b, 0, 0))
    keep = pl.pallas_call(
        functools.partial(_dsa_select_kernel, topk=topk, tq=tq),
        grid=(B,),
        in_specs=[full(H_IDX * D_IDX), full(LANES), full(LANES)],
        out_specs=keep_spec,
        out_shape=jax.ShapeDtypeStruct((B, n_rows, tq), jnp.bfloat16),
        scratch_shapes=[pltpu.VMEM((n_rows, tq), jnp.float32)],
        compiler_params=_cparams(1),
        name="dsa_select",
    )(qi, kiwi, ki4)
    return pl.pallas_call(
        functools.partial(_dsa_attend_kernel, tq=tq),
        grid=(B,),
        in_specs=[full(D_B), full(D_B), full(D_B), full(D_B), keep_spec],
        out_specs=full(D_B),
        out_shape=jax.ShapeDtypeStruct((B, S, D_B), jnp.bfloat16),
        scratch_shapes=[pltpu.VMEM((nq, D_B, tq), jnp.bfloat16)],
        compiler_params=_cparams(1),
        name="dsa_attend",
    )(qb, gb, kb, vb, keep)


def _dsa_sample_kernel(q_ref, qi_ref, kiwi_ref, g_ref, kc_ref, vc_ref, kic_ref,
                       kn_ref, vn_ref, kin_ref, o_ref, kbf, vt, ki4s, sc_ref, *, topk):
    P = kc_ref.shape[1]
    T = kn_ref.shape[1]
    tq = q_ref.shape[1]
    kbf[:P, :] = kc_ref[0].astype(jnp.bfloat16)
    kbf[P:, :] = kn_ref[0].astype(jnp.bfloat16)
    vt[:, :P] = vc_ref[0].T.astype(jnp.bfloat16)
    vn_tile = jnp.concatenate([vn_ref[0], jnp.zeros((LANES - T, D_B), jnp.float32)], axis=0)
    vt[:, P:] = vn_tile.T[:, :T].astype(jnp.bfloat16)
    ki4s[:P, :] = kic_ref[0]
    ki4s[P:, :] = kin_ref[0]
    sc_ref[...] = _dsa_scores(None, qi_ref[0], kiwi_ref[0], ki4s[...])
    _dsa_select(sc_ref, [(0, P + T)], tq, topk)

    def store(sl, o):
        o_ref[0, :, sl] = o

    _kq_attend(lambda h: sc_ref[...], q_ref[0], g_ref[0], lambda sl: kbf[:, sl], lambda sl: vt[sl, :],
               store)


def _dsa_sample_call(qb, qi, kiwi, gb, kc, vc, kic, kn, vn, kin):
    B, tq, _ = qb.shape
    P, T = kc.shape[1], kn.shape[1]
    tk = P + T
    topk = min(TOPK_MAX, tk // 4)
    blk = lambda r, n: pl.BlockSpec((1, r, n), lambda b: (b, 0, 0))
    return pl.pallas_call(
        functools.partial(_dsa_sample_kernel, topk=topk),
        grid=(B,),
        in_specs=[blk(tq, D_B), blk(tq, H_IDX * D_IDX), blk(tq, LANES), blk(tq, D_B),
                  blk(P, D_B), blk(P, D_B), blk(P, LANES),
                  blk(T, D_B), blk(T, D_B), blk(T, LANES)],
        out_specs=blk(tq, D_B),
        out_shape=jax.ShapeDtypeStruct((B, tq, D_B), jnp.bfloat16),
        scratch_shapes=[pltpu.VMEM((tk, D_B), jnp.bfloat16),
                        pltpu.VMEM((D_B, tk), jnp.bfloat16),
                        pltpu.VMEM((tk, LANES), jnp.bfloat16),
                        pltpu.VMEM((tk, tq), jnp.float32)],
        compiler_params=_cparams(1),
        name="dsa_sample",
    )(qb, qi, kiwi, gb, kc, vc, kic, kn, vn, kin)


def _merge_kernel(x_ref, oa_ref, ob_ref, om_ref, w_ref, g_ref, y_ref, *, last):
    acc = x_ref[...]
    acc = acc + jnp.dot(oa_ref[...], w_ref[:D_A, :], preferred_element_type=jnp.float32)
    acc = acc + jnp.dot(ob_ref[...], w_ref[D_A:D_A + D_B, :], preferred_element_type=jnp.float32)
    acc = acc + jnp.dot(om_ref[...], w_ref[D_A + D_B:, :], preferred_element_type=jnp.float32)
    y_ref[...] = _rms_scale(acc, g_ref[...]) if last else acc


def _merge_call(x2d, oa, ob, om, w, g, last):
    R, D = x2d.shape
    tm = min(MERGE_TM, R)
    assert R % tm == 0
    row = lambda n: pl.BlockSpec((tm, n), lambda i: (i, 0))
    return pl.pallas_call(
        functools.partial(_merge_kernel, last=last),
        grid=(R // tm,),
        in_specs=[row(D), row(D_A), row(D_B), row(D_M),
                  pl.BlockSpec((D_A + D_B + D_M, D), lambda i: (0, 0)),
                  pl.BlockSpec((1, D), lambda i: (0, 0))],
        out_specs=row(D),
        out_shape=jax.ShapeDtypeStruct((R, D), jnp.float32),
        compiler_params=_cparams(1),
        name="merge",
    )(x2d, oa, ob, om, w, g)


def _pack_w_in(w):
    D = w.shape[0]
    ki = w[:, C_KIWI:C_KIWI + D_IDX]
    wi = w[:, C_KIWI + D_IDX:C_KIWI + D_IDX + H_IDX]
    pad = jnp.zeros((D, LANES - D_IDX - H_IDX), w.dtype)
    packed = jnp.concatenate([w[:, :C_KIWI], ki, wi, pad, jnp.tile(ki, (1, LANES // D_IDX))], axis=1)
    return packed.astype(jnp.bfloat16)


def _rope_tables(pos):
    posf = pos.astype(jnp.float32)

    def tables(d):
        half = d // 2
        inv_freq = ROPE_THETA ** (-jnp.arange(half, dtype=jnp.float32) * 2.0 / d)
        ang = posf[:, None] * inv_freq[None, :]
        cos, sin = jnp.cos(ang), jnp.sin(ang)
        return jnp.concatenate([cos, cos], axis=1), jnp.concatenate([-sin, sin], axis=1)

    cos64, sin64 = tables(D_HEAD)
    cos32, sin32 = tables(D_IDX)
    n = pos.shape[0]
    rest = LANES - D_IDX
    cos_kw = jnp.concatenate([cos32, jnp.ones((n, rest), jnp.float32)], axis=1)
    sin_kw = jnp.concatenate([sin32, jnp.zeros((n, rest), jnp.float32)], axis=1)
    rep = lambda t, d: jnp.tile(t, (1, LANES // d))
    return jnp.stack([rep(cos64, D_HEAD), rep(sin64, D_HEAD), rep(cos32, D_IDX), rep(sin32, D_IDX),
                      cos_kw, sin_kw])


def _band_bias_vec(table):
    n_far = A_WINDOW - REL_CLIP + 1
    lo_idx = A_WINDOW + REL_CLIP - (BAND_WIN - 1)
    assert lo_idx >= 0 and BIAS_VEC >= BAND_WIN + BAND_TQ - 1
    far = table[:, 2 * REL_CLIP:2 * REL_CLIP + 1].astype(jnp.float32)
    mid = table[:, lo_idx:2 * REL_CLIP][:, ::-1].astype(jnp.float32)
    vec = jnp.concatenate([jnp.tile(far, (1, n_far)), mid, jnp.tile(far, (1, BIAS_VEC - BAND_WIN))], axis=1)
    rvec = jnp.concatenate([vec[:, :1], vec[:, 1:][:, ::-1]], axis=1)
    return vec, rvec


def kernel(x_prompt, x_sample, mem_prompt, cache_a_k, cache_a_v, cache_b_k, cache_b_v, cache_b_kidx,
           cache_mem_k, cache_mem_v, norm_mix_g, w_in, rel_bias_a, norm_mem_g, w_mem_kv, w_out,
           norm_final_g):
    B, S, D = x_prompt.shape
    Bs, T, _ = x_sample.shape
    depth = w_in.shape[0]
    P = cache_b_k.shape[2]
    Pa = cache_a_k.shape[2]
    n_mem = mem_prompt.shape[1]
    keep = min(A_WINDOW, S)
    assert Pa == A_WINDOW and T <= CHUNK and LANES % T == 0

    rope_p = _rope_tables(jnp.arange(S))
    rope_s = jnp.tile(_rope_tables(P + jnp.arange(T)), (1, Bs, 1))
    g_final = norm_final_g.reshape(1, D)

    xp, xs = x_prompt, x_sample
    outs_p = [[] for _ in range(7)]
    outs_s = [[] for _ in range(5)]
    for l in range(depth):
        last = l == depth - 1
        w = _pack_w_in(w_in[l])
        g_mix = norm_mix_g[l].reshape(1, D)
        w_o = w_out[l].astype(jnp.bfloat16)
        bias_vec, bias_rvec = _band_bias_vec(rel_bias_a[l])

        (qa, ka, va, ga, qb, kb, vb, gb, qm, gm, qi, kiwi, ki4, ki, kbb, vbb, ak, av) = _proj_call(
            xp, g_mix, w, rope_p, keep)
        mk, mv = _memkv_call(mem_prompt.reshape(B * n_mem, D), norm_mem_g[l].reshape(1, D),
                             w_mem_kv[l].astype(jnp.bfloat16))
        mk = mk.reshape(B, n_mem, D_M)
        mv = mv.reshape(B, n_mem, D_M)
        oa, om = _band_prompt_call(qa, ka, va, ga, qm, gm, mk, mv, bias_rvec)
        ob = _dsa_prompt_call(qb, qi, kiwi, gb, kbb, vbb, ki4)
        xp = _merge_call(xp.reshape(B * S, D), oa.reshape(B * S, D_A), ob.reshape(B * S, D_B),
                         om.reshape(B * S, D_M), w_o, g_final, last).reshape(B, S, D)
        for lst, t in zip(outs_p, (ak.reshape(B, keep, H_A, D_HEAD), av.reshape(B, keep, H_A, D_HEAD),
                                   kb.reshape(B, S, H_B, D_HEAD), vb.reshape(B, S, H_B, D_HEAD), ki,
                                   mk.reshape(B, n_mem, H_M, D_HEAD), mv.reshape(B, n_mem, H_M, D_HEAD))):
            lst.append(t)

        (qa, ka, va, ga, qb, kb, vb, gb, qm, gm, qi, kiwi, ki4, ki, kbb, vbb, ak, av) = _proj_call(
            xs.reshape(1, Bs * T, D), g_mix, w, rope_s, Bs * T)
        per_b = lambda t: t.reshape(Bs, T, t.shape[-1])
        oa, om = _band_sample_call(
            per_b(qa), cache_a_k[l].reshape(Bs, Pa, D_A), cache_a_v[l].reshape(Bs, Pa, D_A),
            per_b(ka), per_b(va), per_b(ga), per_b(qm), per_b(gm),
            cache_mem_k[l].reshape(Bs, n_mem, D_M), cache_mem_v[l].reshape(Bs, n_mem, D_M), bias_vec)
        rep = lambda t: jnp.tile(per_b(t), (1, LANES // T, 1))
        kic = jnp.tile(cache_b_kidx[l], (1, 1, LANES // D_IDX)).astype(jnp.bfloat16)
        ob = _dsa_sample_call(rep(qb), rep(qi), rep(kiwi), rep(gb),
                              cache_b_k[l].reshape(Bs, P, D_B), cache_b_v[l].reshape(Bs, P, D_B), kic,
                              per_b(kb), per_b(vb), per_b(ki4))[:, :T]
        xs = _merge_call(xs.reshape(Bs * T, D), oa.reshape(Bs * T, D_A), ob.reshape(Bs * T, D_B),
                         om.reshape(Bs * T, D_M), w_o, g_final, last).reshape(Bs, T, D)
        for lst, t in zip(outs_s, (ak.reshape(Bs, T, H_A, D_HEAD), av.reshape(Bs, T, H_A, D_HEAD),
                                   kb.reshape(Bs, T, H_B, D_HEAD), vb.reshape(Bs, T, H_B, D_HEAD),
                                   ki.reshape(Bs, T, D_IDX))):
            lst.append(t)

    st = lambda ts: jnp.stack(ts, axis=0)
    return (xp, xs) + tuple(st(t) for t in outs_p) + tuple(st(t) for t in outs_s)
```

```python
import functools
import math

import jax
import jax.numpy as jnp
from jax import lax
from jax.experimental import pallas as pl
from jax.experimental.pallas import tpu as pltpu

CHUNK = 64
D_HEAD = 64
H_A = 6
H_B = 6
H_M = 4
D_A = H_A * D_HEAD
D_B = H_B * D_HEAD
D_M = H_M * D_HEAD
A_LEFT_CHUNKS = 8
A_WINDOW = A_LEFT_CHUNKS * CHUNK
REL_CLIP = 256
H_IDX = 8
D_IDX = 32
TOPK_MAX = 256
ROPE_THETA = 10000.0
EPS = 1e-6

LANES = 128
VMEM_LIMIT_BYTES = 56 * 1024 * 1024

C_QA, C_KA, C_VA, C_GA = 0, D_A, 2 * D_A, 3 * D_A
C_QB = 4 * D_A
C_KB, C_VB, C_GB = C_QB + D_B, C_QB + 2 * D_B, C_QB + 3 * D_B
C_QM = C_QB + 4 * D_B
C_GM = C_QM + D_M
C_QI = C_GM + D_M
C_KIWI = C_QI + H_IDX * D_IDX
C_KI4 = C_KIWI + LANES
W_COLS = C_KI4 + LANES

PROJ_TM = 512
MERGE_TM = 1024
BAND_TQ = 256
BAND_WIN = A_WINDOW + BAND_TQ
DSA_TQ = 256

LOG2E = math.log2(math.e)
QK_SCALE = (D_HEAD ** -0.5) * LOG2E
BIAS_VEC = 8 * LANES

_NT = (((1,), (1,)), ((), ()))
_INT_MIN = -2 ** 31
_NEG_INF = float("-inf")


def _cparams(n_axes):
    return pltpu.CompilerParams(
        dimension_semantics=("arbitrary",) * n_axes,
        vmem_limit_bytes=VMEM_LIMIT_BYTES)


def _silu(g):
    return g * (1.0 / (1.0 + jnp.exp(-g)))


def _rms_scale(x, g):
    ms = jnp.mean(x * x, axis=-1, keepdims=True)
    return (x * lax.rsqrt(ms + EPS)) * g


def _rope_slab(z, cos, sin, half):
    lane = lax.broadcasted_iota(jnp.int32, z.shape, 1)
    first = (lane & (2 * half - 1)) < half
    partner = jnp.where(first, pltpu.roll(z, LANES - half, 1), pltpu.roll(z, half, 1))
    return z * cos + partner * sin


def _proj_kernel(x_ref, g_ref, w_ref, rope_ref,
                 qa_ref, ka_ref, va_ref, ga_ref, qb_ref, kb_ref, vb_ref, gb_ref,
                 qm_ref, gm_ref, qi_ref, kiwi_ref, ki4_ref, ki_ref, kbb_ref, vbb_ref, ak_ref, av_ref,
                 *, n_tiles, keep_tiles):
    i = pl.program_id(1)
    scale = QK_SCALE
    tm = x_ref.shape[1]
    n_sub = 2 if tm % 32 == 0 else 1
    kept = []
    for u in range(n_sub):
        rows = slice(u * tm // n_sub, (u + 1) * tm // n_sub)
        xn = _rms_scale(x_ref[0, rows, :], g_ref[...]).astype(jnp.bfloat16)

        def proj(c0, n, xn=xn):
            return jnp.dot(xn, w_ref[:, c0:c0 + n], preferred_element_type=jnp.float32)

        z = proj(C_QA, 2 * D_A)
        qa_ref[0, rows, :] = (z[:, :D_A] * scale).astype(jnp.bfloat16)
        ka = z[:, D_A:]
        ka_ref[0, rows, :] = ka.astype(jnp.bfloat16)
        z = proj(C_VA, 2 * D_A)
        va = z[:, :D_A]
        va_ref[0, rows, :] = va.astype(jnp.bfloat16)
        ga_ref[0, rows, :] = _silu(z[:, D_A:]).astype(jnp.bfloat16)
        kept.append((rows, ka, va))

        cos64, sin64 = rope_ref[0, rows, :], rope_ref[1, rows, :]
        cos32, sin32 = rope_ref[2, rows, :], rope_ref[3, rows, :]
        cos_kw, sin_kw = rope_ref[4, rows, :], rope_ref[5, rows, :]
        z = proj(C_QB, 2 * D_B)
        for s in range(D_B // LANES):
            sl = slice(s * LANES, (s + 1) * LANES)
            qb = _rope_slab(z[:, sl], cos64, sin64, D_HEAD // 2)
            qb_ref[0, rows, sl] = (qb * scale).astype(jnp.bfloat16)
            kb = _rope_slab(z[:, D_B + s * LANES:D_B + (s + 1) * LANES], cos64, sin64, D_HEAD // 2)
            kb_ref[0, rows, sl] = kb
            kbb_ref[0, rows, sl] = kb.astype(jnp.bfloat16)
        z = proj(C_VB, 2 * D_B)
        vb_ref[0, rows, :] = z[:, :D_B]
        vbb_ref[0, rows, :] = z[:, :D_B].astype(jnp.bfloat16)
        gb_ref[0, rows, :] = _silu(z[:, D_B:]).astype(jnp.bfloat16)
        qm_ref[0, rows, :] = (proj(C_QM, D_M) * scale).astype(jnp.bfloat16)
        gm_ref[0, rows, :] = _silu(proj(C_GM, D_M)).astype(jnp.bfloat16)
        z = proj(C_QI, H_IDX * D_IDX)
        for s in range(H_IDX * D_IDX // LANES):
            sl = slice(s * LANES, (s + 1) * LANES)
            qi_ref[0, rows, sl] = _rope_slab(z[:, sl], cos32, sin32, D_IDX // 2).astype(jnp.bfloat16)
        z = proj(C_KIWI, 2 * LANES)
        kiwi = _rope_slab(z[:, :LANES], cos_kw, sin_kw, D_IDX // 2)
        kiwi_ref[0, rows, :] = kiwi
        ki_ref[0, rows, :] = kiwi[:, :D_IDX]
        ki4_ref[0, rows, :] = _rope_slab(z[:, LANES:], cos32, sin32, D_IDX // 2).astype(jnp.bfloat16)

    @pl.when(i >= n_tiles - keep_tiles)
    def _():
        for rows, ka, va in kept:
            ak_ref[0, rows, :] = ka
            av_ref[0, rows, :] = va


def _proj_call(x, g, w, rope, keep_rows):
    B, S, D = x.shape
    tm = min(PROJ_TM, S)
    n_tiles = S // tm
    keep_tiles = keep_rows // tm
    assert n_tiles * tm == S and keep_tiles * tm == keep_rows

    def tile(n, dtype):
        return (jax.ShapeDtypeStruct((B, S, n), dtype),
                pl.BlockSpec((1, tm, n), lambda b, i: (b, i, 0)))

    f32, bf16 = jnp.float32, jnp.bfloat16
    outs = [tile(D_A, bf16), tile(D_A, bf16), tile(D_A, bf16), tile(D_A, bf16),
            tile(D_B, bf16), tile(D_B, f32), tile(D_B, f32), tile(D_B, bf16),
            tile(D_M, bf16), tile(D_M, bf16), tile(H_IDX * D_IDX, bf16),
            tile(LANES, f32), tile(LANES, bf16), tile(D_IDX, f32), tile(D_B, bf16), tile(D_B, bf16)]
    keep_spec = pl.BlockSpec(
        (1, tm, D_A), lambda b, i: (b, jnp.maximum(i - (n_tiles - keep_tiles), 0), 0))
    outs += [(jax.ShapeDtypeStruct((B, keep_rows, D_A), f32), keep_spec)] * 2
    return pl.pallas_call(
        functools.partial(_proj_kernel, n_tiles=n_tiles, keep_tiles=keep_tiles),
        grid=(B, n_tiles),
        in_specs=[pl.BlockSpec((1, tm, D), lambda b, i: (b, i, 0)),
                  pl.BlockSpec((1, D), lambda b, i: (0, 0)),
                  pl.BlockSpec((D, W_COLS), lambda b, i: (0, 0)),
                  pl.BlockSpec((6, tm, LANES), lambda b, i: (0, i, 0))],
        out_specs=[o[1] for o in outs],
        out_shape=[o[0] for o in outs],
        compiler_params=_cparams(2),
        name="proj",
    )(x, g, w, rope)


def _memkv_kernel(m_ref, g_ref, w_ref, mk_ref, mv_ref):
    xn = _rms_scale(m_ref[...], g_ref[...]).astype(jnp.bfloat16)
    mk_ref[...] = jnp.dot(xn, w_ref[:, :D_M], preferred_element_type=jnp.float32)
    mv_ref[...] = jnp.dot(xn, w_ref[:, D_M:], preferred_element_type=jnp.float32)


def _memkv_call(mem2d, g, w):
    R, D = mem2d.shape
    tm = min(PROJ_TM, R)
    assert R % tm == 0
    row = lambda n: pl.BlockSpec((tm, n), lambda i: (i, 0))
    return pl.pallas_call(
        _memkv_kernel,
        grid=(R // tm,),
        in_specs=[row(D), pl.BlockSpec((1, D), lambda i: (0, 0)),
                  pl.BlockSpec((D, 2 * D_M), lambda i: (0, 0))],
        out_specs=[row(D_M), row(D_M)],
        out_shape=[jax.ShapeDtypeStruct((R, D_M), jnp.float32)] * 2,
        compiler_params=_cparams(1),
        name="memkv",
    )(mem2d, g, w)


def _softmax_rows_pv(s, v):
    m = jnp.max(s, axis=1, keepdims=True)
    p = jnp.exp2(s - m)
    l = jnp.sum(p, axis=1, keepdims=True)
    o = jnp.dot(p.astype(jnp.bfloat16), v, preferred_element_type=jnp.float32)
    return o * (1.0 / l)


def _keep_lanes(slab, lo, hi):
    x = slab.astype(jnp.float32)
    lane = lax.broadcasted_iota(jnp.int32, x.shape, 1)
    return jnp.where(lane >= lo, jnp.where(lane < hi, x, 0.0), 0.0).astype(jnp.bfloat16)


def _pair_attend(q_slab, k_slab, v_slab, bias_fn):
    lane = lax.broadcasted_iota(jnp.int32, q_slab.shape, 1)
    out = None
    for hh in range(2):
        s = lax.dot_general(_keep_lanes(q_slab, hh * D_HEAD, (hh + 1) * D_HEAD), k_slab, _NT,
                            preferred_element_type=jnp.float32)
        o = _softmax_rows_pv(bias_fn(s, hh), v_slab)
        out = o if hh == 0 else jnp.where(lane < D_HEAD, out, o)
    return out


def _mem_attend(qm_ref, gm_ref, mk_ref, mv_ref, om_ref):
    for p in range(D_M // LANES):
        sl = slice(p * LANES, (p + 1) * LANES)
        o = _pair_attend(qm_ref[0, :, sl], mk_ref[0, :, sl].astype(jnp.bfloat16),
                         mv_ref[0, :, sl].astype(jnp.bfloat16), lambda s, hh: s)
        om_ref[0, :, sl] = (o * gm_ref[0, :, sl].astype(jnp.float32)).astype(jnp.bfloat16)


def _build_band_bias(vec_ref, bias_ref, n_q, n_valid, chunked):
    n_keys = bias_ref.shape[2]
    i = lax.broadcasted_iota(jnp.int32, (n_q, n_keys), 0)
    r = lax.broadcasted_iota(jnp.int32, (n_q, n_keys), 1)
    lo = ((i >> 6) << 6) if chunked else jnp.zeros_like(i)
    for h in range(bias_ref.shape[0]):
        base = jnp.broadcast_to(vec_ref[h:h + 1, :], (n_q, BIAS_VEC))
        rolled = pltpu.roll(base, 0, 1, stride=1, stride_axis=0)
        b = rolled[:, :n_keys] * LOG2E
        bias_ref[h] = jnp.where(r >= lo, jnp.where(r < lo + n_valid, b, _NEG_INF), _NEG_INF)


def _build_band_bias_t(rvec_ref, bias_ref):
    n_keys, n_q = bias_ref.shape[1], bias_ref.shape[2]
    r = lax.broadcasted_iota(jnp.int32, (n_keys, n_q), 0)
    i = lax.broadcasted_iota(jnp.int32, (n_keys, n_q), 1)
    lo = (i >> 6) << 6
    for h in range(bias_ref.shape[0]):
        base = jnp.broadcast_to(rvec_ref[h:h + 1, :], (n_keys, BIAS_VEC))
        rolled = pltpu.roll(base, 0, 1, stride=1, stride_axis=0)
        b = rolled[:, :n_q] * LOG2E
        bias_ref[h] = jnp.where(r >= lo, jnp.where(r < lo + A_WINDOW + CHUNK, b, _NEG_INF), _NEG_INF)


def _band_prompt_kernel(qa_ref, ka_ref, va_ref, ga_ref, qm_ref, gm_ref, mk_ref, mv_ref, rvec_ref,
                        oa_ref, om_ref, kpad, vt_blk, mk_bf, mv_t, bias_ref):
    j = pl.program_id(1)
    n_front = A_WINDOW // BAND_TQ
    n_win = BAND_WIN // BAND_TQ

    @pl.when(jnp.logical_and(pl.program_id(0) == 0, j == 0))
    def _():
        _build_band_bias_t(rvec_ref, bias_ref)

    @pl.when(j == 0)
    def _():
        kpad[:A_WINDOW, :] = jnp.zeros((A_WINDOW, D_A), jnp.bfloat16)
        kpad[A_WINDOW:, :] = ka_ref[0]
        for t in range(vt_blk.shape[0]):
            if t < n_front:
                vt_blk[t] = jnp.zeros((D_A, BAND_TQ), jnp.bfloat16)
            else:
                rows = slice((t - n_front) * BAND_TQ, (t - n_front + 1) * BAND_TQ)
                vt_blk[t] = va_ref[0, rows, :].astype(jnp.float32).T.astype(jnp.bfloat16)
        mk_bf[...] = mk_ref[0].astype(jnp.bfloat16)
        mv_t[...] = mv_ref[0].T.astype(jnp.bfloat16)

    start = pl.multiple_of(j * BAND_TQ, BAND_TQ)

    def store_a(sl, o):
        oa_ref[0, :, sl] = o

    def store_m(sl, o):
        om_ref[0, :, sl] = o

    def run(mask_front):
        if mask_front:
            pos = lax.broadcasted_iota(jnp.int32, (BAND_WIN, BAND_TQ), 0) + j * BAND_TQ
            front = jnp.where(pos >= A_WINDOW, 0.0, _NEG_INF)
        _kq_attend(lambda h: bias_ref[h] + front if mask_front else bias_ref[h],
                   qa_ref[0], ga_ref[0],
                   lambda sl: kpad[pl.ds(start, BAND_WIN), sl],
                   lambda sl: jnp.concatenate([vt_blk[j + u, sl, :] for u in range(n_win)], axis=1),
                   store_a)
        _kq_attend(lambda h: None, qm_ref[0], gm_ref[0], lambda sl: mk_bf[:, sl], lambda sl: mv_t[sl, :],
                   store_m)

    pl.when(j < n_front)(lambda: run(True))
    pl.when(j >= n_front)(lambda: run(False))


def _band_prompt_call(qa, ka, va, ga, qm, gm, mk, mv, bias_rvec):
    B, S, _ = qa.shape
    nq = S // BAND_TQ
    n_mem = mk.shape[1]
    assert nq * BAND_TQ == S and A_WINDOW % BAND_TQ == 0
    qblk = lambda n: pl.BlockSpec((1, BAND_TQ, n), lambda b, j: (b, j, 0))
    full = lambda r, n: pl.BlockSpec((1, r, n), lambda b, j: (b, 0, 0))
    return pl.pallas_call(
        _band_prompt_kernel,
        grid=(B, nq),
        in_specs=[qblk(D_A), full(S, D_A), full(S, D_A), qblk(D_A), qblk(D_M), qblk(D_M),
                  full(n_mem, D_M), full(n_mem, D_M),
                  pl.BlockSpec((H_A, BIAS_VEC), lambda b, j: (0, 0))],
        out_specs=[qblk(D_A), qblk(D_M)],
        out_shape=[jax.ShapeDtypeStruct((B, S, D_A), jnp.bfloat16),
                   jax.ShapeDtypeStruct((B, S, D_M), jnp.bfloat16)],
        scratch_shapes=[pltpu.VMEM((S + A_WINDOW, D_A), jnp.bfloat16),
                        pltpu.VMEM((nq + A_WINDOW // BAND_TQ, D_A, BAND_TQ), jnp.bfloat16),
                        pltpu.VMEM((n_mem, D_M), jnp.bfloat16),
                        pltpu.VMEM((D_M, n_mem), jnp.bfloat16),
                        pltpu.VMEM((H_A, BAND_WIN, BAND_TQ), jnp.float32)],
        compiler_params=_cparams(2),
        name="band_prompt",
    )(qa, ka, va, ga, qm, gm, mk, mv, bias_rvec)


def _band_sample_kernel(qa_ref, kc_ref, vc_ref, kn_ref, vn_ref, ga_ref, qm_ref, gm_ref,
                        mk_ref, mv_ref, vec_ref, oa_ref, om_ref, kcat, vcat, bias_ref):
    P = kc_ref.shape[1]
    T = kn_ref.shape[1]
    pad = kcat.shape[0] - P - T

    @pl.when(pl.program_id(0) == 0)
    def _():
        _build_band_bias(vec_ref, bias_ref, T, P + T, False)

    kcat[:P, :] = kc_ref[0].astype(jnp.bfloat16)
    vcat[:P, :] = vc_ref[0].astype(jnp.bfloat16)
    kcat[P:P + T, :] = kn_ref[0]
    vcat[P:P + T, :] = vn_ref[0]
    zeros = jnp.zeros((pad, D_A), jnp.bfloat16)
    kcat[P + T:, :] = zeros
    vcat[P + T:, :] = zeros
    for p in range(D_A // LANES):
        sl = slice(p * LANES, (p + 1) * LANES)
        o = _pair_attend(qa_ref[0, :, sl], kcat[:, sl], vcat[:, sl],
                         lambda s, hh, p=p: s + bias_ref[2 * p + hh])
        oa_ref[0, :, sl] = (o * ga_ref[0, :, sl].astype(jnp.float32)).astype(jnp.bfloat16)
    _mem_attend(qm_ref, gm_ref, mk_ref, mv_ref, om_ref)


def _band_sample_call(qa, kc, vc, kn, vn, ga, qm, gm, mk, mv, bias_vec):
    B, T, _ = qa.shape
    P = kc.shape[1]
    n_keys = -(-(P + T) // LANES) * LANES
    n_mem = mk.shape[1]
    blk = lambda r, n: pl.BlockSpec((1, r, n), lambda b: (b, 0, 0))
    return pl.pallas_call(
        _band_sample_kernel,
        grid=(B,),
        in_specs=[blk(T, D_A), blk(P, D_A), blk(P, D_A), blk(T, D_A), blk(T, D_A), blk(T, D_A),
                  blk(T, D_M), blk(T, D_M), blk(n_mem, D_M), blk(n_mem, D_M),
                  pl.BlockSpec((H_A, BIAS_VEC), lambda b: (0, 0))],
        out_specs=[blk(T, D_A), blk(T, D_M)],
        out_shape=[jax.ShapeDtypeStruct((B, T, D_A), jnp.bfloat16),
                   jax.ShapeDtypeStruct((B, T, D_M), jnp.bfloat16)],
        scratch_shapes=[pltpu.VMEM((n_keys, D_A), jnp.bfloat16)] * 2
        + [pltpu.VMEM((H_A, T, n_keys), jnp.float32)],
        compiler_params=_cparams(1),
        name="band_sample",
    )(qa, kc, vc, kn, vn, ga, qm, gm, mk, mv, bias_vec)


def _key_to_f32(k):
    bits = k ^ ((k >> 31) & jnp.int32(0x7FFFFFFF))
    return lax.bitcast_convert_type(bits, jnp.float32)


def _count(pred_f32):
    ones = jnp.ones((8, pred_f32.shape[0]), jnp.bfloat16)
    c = jnp.dot(ones, pred_f32.astype(jnp.bfloat16), preferred_element_type=jnp.float32)
    return c[0:1, :]


def _dsa_select(sc_ref, probs, tq, topk):
    one, zero = jnp.float32(1.0), jnp.float32(0.0)
    live = [(off, tk) for off, tk in probs if tk > topk]
    n_live = len(live)

    def sc_of(n):
        off, tk = live[n]
        return sc_ref[off:off + tk, :]

    if live:
        def thr_step(i, ts):
            bit = lax.shift_left(jnp.int32(1), 31 - i)
            out = []
            for n in range(n_live):
                cand = ts[n] + bit
                c = _count(jnp.where(sc_of(n) >= _key_to_f32(cand), one, zero))
                out.append(jnp.where(c >= topk, cand, ts[n]))
            return tuple(out)

        start = tuple(jnp.full((1, tq), _INT_MIN, jnp.int32) for _ in range(n_live))
        t_keys = lax.fori_loop(0, 32, thr_step, start)
        thrs = [jnp.where(t == _INT_MIN, _NEG_INF, _key_to_f32(t)) for t in t_keys]

        for n in range(n_live):
            off, tk = live[n]
            c_ge = _count(jnp.where(sc_of(n) >= thrs[n], one, zero))

            def tie_search(n=n, off=off, tk=tk):
                need = topk - _count(jnp.where(sc_of(n) > thrs[n], one, zero))
                tile = 2 * LANES

                def step(first, size, state):
                    before, jmax = state
                    r = lax.broadcasted_iota(jnp.int32, (size, size), 0)
                    c = lax.broadcasted_iota(jnp.int32, (size, size), 1)
                    lower = jnp.where(c <= r, one, zero).astype(jnp.bfloat16)
                    tied = jnp.where(sc_ref[pl.ds(off + first, size), :] == thrs[n], one, zero)
                    rank = before + jnp.dot(lower, tied.astype(jnp.bfloat16),
                                            preferred_element_type=jnp.float32)
                    kidx = (lax.broadcasted_iota(jnp.int32, (size, tq), 0) + first).astype(jnp.float32)
                    taken = jnp.where(tied > 0, jnp.where(rank <= need, kidx, -one), -one)
                    jmax = jnp.maximum(jmax, jnp.max(taken, axis=0, keepdims=True))
                    return rank[size - 1:size, :], jmax

                state = (jnp.zeros((1, tq), jnp.float32), jnp.full((1, tq), -1.0, jnp.float32))
                n_full, rest = divmod(tk, tile)
                state = lax.fori_loop(
                    0, n_full, lambda t, st: step(pl.multiple_of(t * tile, tile), tile, st), state)
                if rest:
                    state = step(n_full * tile, rest, state)
                return state[1].astype(jnp.int32)

            def keep_with_ties(n=n, off=off, tk=tk, tie_search=tie_search):
                jmax = tie_search()
                sc = sc_of(n)
                kidx = lax.broadcasted_iota(jnp.int32, (tk, tq), 0)
                fin = jnp.where(jnp.abs(sc) < jnp.float32(jnp.inf), zero, _NEG_INF)
                sc_ref[off:off + tk, :] = jnp.where(
                    sc > thrs[n], fin,
                    jnp.where(sc == thrs[n], jnp.where(kidx <= jmax, fin, _NEG_INF), _NEG_INF))

            def keep_plain(n=n, off=off, tk=tk):
                sc = sc_of(n)
                fin = jnp.where(jnp.abs(sc) < jnp.float32(jnp.inf), zero, _NEG_INF)
                sc_ref[off:off + tk, :] = jnp.where(sc >= thrs[n], fin, _NEG_INF)

            has_excess = jnp.max(jnp.where(c_ge > topk, one, zero)) > 0
            lax.cond(has_excess, keep_with_ties, keep_plain)

    for off, tk in probs:
        if tk <= topk:
            sc = sc_ref[off:off + tk, :]
            sc_ref[off:off + tk, :] = jnp.where(jnp.abs(sc) < jnp.float32(jnp.inf), zero, _NEG_INF)


def _dsa_scores(limit, qi, kiwi, ki4):
    tk, tq = ki4.shape[0], qi.shape[0]
    kiwi_t = kiwi.T
    acc = jnp.zeros((tk, tq), jnp.float32)
    heads_per_slab = LANES // D_IDX

    def head_q(h):
        slab = qi[:, (h // heads_per_slab) * LANES:(h // heads_per_slab + 1) * LANES]
        lo = (h % heads_per_slab) * D_IDX
        return _keep_lanes(slab, lo, lo + D_IDX)

    for h in range(0, H_IDX, 2):
        d = lax.dot_general(ki4, jnp.concatenate([head_q(h), head_q(h + 1)], axis=0), _NT,
                            preferred_element_type=jnp.float32)
        for e in range(2):
            w = kiwi_t[D_IDX + h + e:D_IDX + h + e + 1, :] * ((D_IDX ** -0.5) * (H_IDX ** -0.5))
            acc = acc + w * jnp.maximum(d[:, e * tq:(e + 1) * tq], 0.0)
    if limit is not None:
        kidx = lax.broadcasted_iota(jnp.int32, (tk, tq), 0)
        acc = jnp.where(kidx < limit, acc, _NEG_INF)
    return acc


def _kq_attend(bias_of, q, g, k_slab_of, vt_slab_of, store):
    tq = q.shape[0]
    for p in range(q.shape[1] // LANES):
        sl = slice(p * LANES, (p + 1) * LANES)
        q_slab = q[:, sl]
        q_pair = jnp.concatenate([_keep_lanes(q_slab, 0, D_HEAD), _keep_lanes(q_slab, D_HEAD, LANES)],
                                 axis=0)
        s_pair = lax.dot_general(k_slab_of(sl), q_pair, _NT, preferred_element_type=jnp.float32)
        probs, inv_l = [], []
        for hh in range(2):
            s = s_pair[:, hh * tq:(hh + 1) * tq]
            bias = bias_of(2 * p + hh)
            if bias is not None:
                s = s + bias
            m = jnp.max(s, axis=0, keepdims=True)
            pr = jnp.exp2(s - m)
            inv_l.append(1.0 / jnp.sum(pr, axis=0, keepdims=True))
            probs.append(pr.astype(jnp.bfloat16))
        o_pair = jnp.dot(vt_slab_of(sl), jnp.concatenate(probs, axis=1),
                         preferred_element_type=jnp.float32)
        ot = jnp.concatenate([o_pair[:D_HEAD, :tq] * inv_l[0], o_pair[D_HEAD:, tq:] * inv_l[1]], axis=0)
        store(sl, (ot.T * g[:, sl].astype(jnp.float32)).astype(jnp.bfloat16))


def _dsa_blocks(S, tq):
    return [(tq * j * (j + 1) // 2, (j + 1) * tq) for j in range(S // tq)]


def _own_region():
    return pl.when(pl.program_id(0) >= 0)


def _dsa_select_kernel(qi_ref, kiwi_ref, ki4_ref, keep_ref, sc_ref, *, topk, tq):
    own_region = _own_region()
    probs = _dsa_blocks(ki4_ref.shape[1], tq)
    for j, (off, tk) in enumerate(probs):
        rows = slice(j * tq, (j + 1) * tq)

        @own_region
        def _(j=j, tk=tk, off=off, rows=rows):
            qpos = j * tq + lax.broadcasted_iota(jnp.int32, (1, tq), 1)
            limit = ((qpos >> 6) + 1) << 6
            sc_ref[off:off + tk, :] = _dsa_scores(limit, qi_ref[0, rows, :], kiwi_ref[0, rows, :],
                                                  ki4_ref[0, :tk, :])

    _dsa_select(sc_ref, probs, tq, topk)
    keep_ref[0] = sc_ref[...].astype(jnp.bfloat16)


def _dsa_attend_kernel(q_ref, g_ref, k_ref, v_ref, keep_ref, o_ref, vt, *, tq):
    own_region = _own_region()
    S = k_ref.shape[1]
    for t in range(S // tq):
        vt[t] = v_ref[0, t * tq:(t + 1) * tq, :].astype(jnp.float32).T.astype(jnp.bfloat16)
    for j, (off, tk) in enumerate(_dsa_blocks(S, tq)):
        rows = slice(j * tq, (j + 1) * tq)

        @own_region
        def _(j=j, tk=tk, off=off, rows=rows):
            def store(sl, o):
                o_ref[0, rows, sl] = o

            bias = keep_ref[0, off:off + tk, :].astype(jnp.float32)
            _kq_attend(lambda h: bias, q_ref[0, rows, :], g_ref[0, rows, :],
                       lambda sl: k_ref[0, :tk, sl],
                       lambda sl: jnp.concatenate([vt[t, sl, :] for t in range(j + 1)], axis=1), store)


def _dsa_prompt_call(qb, qi, kiwi, gb, kb, vb, ki4):
    B, S, _ = qb.shape
    tq = DSA_TQ
    nq = S // tq
    assert CHUNK == 64 and nq * tq == S
    topk = min(TOPK_MAX, S // 4)
    n_rows = tq * nq * (nq + 1) // 2
    full = lambda n: pl.BlockSpec((1, S, n), lambda b: (b, 0, 0))
    keep_spec = pl.BlockSpec((1, n_rows, tq), lambda b: (b, 0, 0))
    keep = pl.pallas_call(
        functools.partial(_dsa_select_kernel, topk=topk, tq=tq),
        grid=(B,),
        in_specs=[full(H_IDX * D_IDX), full(LANES), full(LANES)],
        out_specs=keep_spec,
        out_shape=jax.ShapeDtypeStruct((B, n_rows, tq), jnp.bfloat16),
        scratch_shapes=[pltpu.VMEM((n_rows, tq), jnp.float32)],
        compiler_params=_cparams(1),
        name="dsa_select",
    )(qi, kiwi, ki4)
    return pl.pallas_call(
        functools.partial(_dsa_attend_kernel, tq=tq),
        grid=(B,),
        in_specs=[full(D_B), full(D_B), full(D_B), full(D_B), keep_spec],
        out_specs=full(D_B),
        out_shape=jax.ShapeDtypeStruct((B, S, D_B), jnp.bfloat16),
        scratch_shapes=[pltpu.VMEM((nq, D_B, tq), jnp.bfloat16)],
        compiler_params=_cparams(1),
        name="dsa_attend",
    )(qb, gb, kb, vb, keep)


def _dsa_sample_kernel(q_ref, qi_ref, kiwi_ref, g_ref, kc_ref, vc_ref, kic_ref,
                       kn_ref, vn_ref, kin_ref, o_ref, kbf, vt, ki4s, sc_ref, *, topk):
    P = kc_ref.shape[1]
    T = kn_ref.shape[1]
    tq = q_ref.shape[1]
    kbf[:P, :] = kc_ref[0].astype(jnp.bfloat16)
    kbf[P:, :] = kn_ref[0].astype(jnp.bfloat16)
    vt[:, :P] = vc_ref[0].T.astype(jnp.bfloat16)
    vn_tile = jnp.concatenate([vn_ref[0], jnp.zeros((LANES - T, D_B), jnp.float32)], axis=0)
    vt[:, P:] = vn_tile.T[:, :T].astype(jnp.bfloat16)
    ki4s[:P, :] = kic_ref[0]
    ki4s[P:, :] = kin_ref[0]
    sc_ref[...] = _dsa_scores(None, qi_ref[0], kiwi_ref[0], ki4s[...])
    _dsa_select(sc_ref, [(0, P + T)], tq, topk)

    def store(sl, o):
        o_ref[0, :, sl] = o

    _kq_attend(lambda h: sc_ref[...], q_ref[0], g_ref[0], lambda sl: kbf[:, sl], lambda sl: vt[sl, :],
               store)


def _dsa_sample_call(qb, qi, kiwi, gb, kc, vc, kic, kn, vn, kin):
    B, tq, _ = qb.shape
    P, T = kc.shape[1], kn.shape[1]
    tk = P + T
    topk = min(TOPK_MAX, tk // 4)
    blk = lambda r, n: pl.BlockSpec((1, r, n), lambda b: (b, 0, 0))
    return pl.pallas_call(
        functools.partial(_dsa_sample_kernel, topk=topk),
        grid=(B,),
        in_specs=[blk(tq, D_B), blk(tq, H_IDX * D_IDX), blk(tq, LANES), blk(tq, D_B),
                  blk(P, D_B), blk(P, D_B), blk(P, LANES),
                  blk(T, D_B), blk(T, D_B), blk(T, LANES)],
        out_specs=blk(tq, D_B),
        out_shape=jax.ShapeDtypeStruct((B, tq, D_B), jnp.bfloat16),
        scratch_shapes=[pltpu.VMEM((tk, D_B), jnp.bfloat16),
                        pltpu.VMEM((D_B, tk), jnp.bfloat16),
                        pltpu.VMEM((tk, LANES), jnp.bfloat16),
                        pltpu.VMEM((tk, tq), jnp.float32)],
        compiler_params=_cparams(1),
        name="dsa_sample",
    )(qb, qi, kiwi, gb, kc, vc, kic, kn, vn, kin)


def _merge_kernel(x_ref, oa_ref, ob_ref, om_ref, w_ref, g_ref, y_ref, *, last):
    acc = x_ref[...]
    acc = acc + jnp.dot(oa_ref[...], w_ref[:D_A, :], preferred_element_type=jnp.float32)
    acc = acc + jnp.dot(ob_ref[...], w_ref[D_A:D_A + D_B, :], preferred_element_type=jnp.float32)
    acc = acc + jnp.dot(om_ref[...], w_ref[D_A + D_B:, :], preferred_element_type=jnp.float32)
    y_ref[...] = _rms_scale(acc, g_ref[...]) if last else acc


def _merge_call(x2d, oa, ob, om, w, g, last):
    R, D = x2d.shape
    tm = min(MERGE_TM, R)
    assert R % tm == 0
    row = lambda n: pl.BlockSpec((tm, n), lambda i: (i, 0))
    return pl.pallas_call(
        functools.partial(_merge_kernel, last=last),
        grid=(R // tm,),
        in_specs=[row(D), row(D_A), row(D_B), row(D_M),
                  pl.BlockSpec((D_A + D_B + D_M, D), lambda i: (0, 0)),
                  pl.BlockSpec((1, D), lambda i: (0, 0))],
        out_specs=row(D),
        out_shape=jax.ShapeDtypeStruct((R, D), jnp.float32),
        compiler_params=_cparams(1),
        name="merge",
    )(x2d, oa, ob, om, w, g)


def _pack_w_in(w):
    D = w.shape[0]
    ki = w[:, C_KIWI:C_KIWI + D_IDX]
    wi = w[:, C_KIWI + D_IDX:C_KIWI + D_IDX + H_IDX]
    pad = jnp.zeros((D, LANES - D_IDX - H_IDX), w.dtype)
    packed = jnp.concatenate([w[:, :C_KIWI], ki, wi, pad, jnp.tile(ki, (1, LANES // D_IDX))], axis=1)
    return packed.astype(jnp.bfloat16)


def _rope_tables(pos):
    posf = pos.astype(jnp.float32)

    def tables(d):
        half = d // 2
        inv_freq = ROPE_THETA ** (-jnp.arange(half, dtype=jnp.float32) * 2.0 / d)
        ang = posf[:, None] * inv_freq[None, :]
        cos, sin = jnp.cos(ang), jnp.sin(ang)
        return jnp.concatenate([cos, cos], axis=1), jnp.concatenate([-sin, sin], axis=1)

    cos64, sin64 = tables(D_HEAD)
    cos32, sin32 = tables(D_IDX)
    n = pos.shape[0]
    rest = LANES - D_IDX
    cos_kw = jnp.concatenate([cos32, jnp.ones((n, rest), jnp.float32)], axis=1)
    sin_kw = jnp.concatenate([sin32, jnp.zeros((n, rest), jnp.float32)], axis=1)
    rep = lambda t, d: jnp.tile(t, (1, LANES // d))
    return jnp.stack([rep(cos64, D_HEAD), rep(sin64, D_HEAD), rep(cos32, D_IDX), rep(sin32, D_IDX),
                      cos_kw, sin_kw])


def _band_bias_vec(table):
    n_far = A_WINDOW - REL_CLIP + 1
    lo_idx = A_WINDOW + REL_CLIP - (BAND_WIN - 1)
    assert lo_idx >= 0 and BIAS_VEC >= BAND_WIN + BAND_TQ - 1
    far = table[:, 2 * REL_CLIP:2 * REL_CLIP + 1].astype(jnp.float32)
    mid = table[:, lo_idx:2 * REL_CLIP][:, ::-1].astype(jnp.float32)
    vec = jnp.concatenate([jnp.tile(far, (1, n_far)), mid, jnp.tile(far, (1, BIAS_VEC - BAND_WIN))], axis=1)
    rvec = jnp.concatenate([vec[:, :1], vec[:, 1:][:, ::-1]], axis=1)
    return vec, rvec


def kernel(x_prompt, x_sample, mem_prompt, cache_a_k, cache_a_v, cache_b_k, cache_b_v, cache_b_kidx,
           cache_mem_k, cache_mem_v, norm_mix_g, w_in, rel_bias_a, norm_mem_g, w_mem_kv, w_out,
           norm_final_g):
    B, S, D = x_prompt.shape
    Bs, T, _ = x_sample.shape
    depth = w_in.shape[0]
    P = cache_b_k.shape[2]
    Pa = cache_a_k.shape[2]
    n_mem = mem_prompt.shape[1]
    keep = min(A_WINDOW, S)
    assert Pa == A_WINDOW and T <= CHUNK and LANES % T == 0

    rope_p = _rope_tables(jnp.arange(S))
    rope_s = jnp.tile(_rope_tables(P + jnp.arange(T)), (1, Bs, 1))
    g_final = norm_final_g.reshape(1, D)

    xp, xs = x_prompt, x_sample
    outs_p = [[] for _ in range(7)]
    outs_s = [[] for _ in range(5)]
    for l in range(depth):
        last = l == depth - 1
        w = _pack_w_in(w_in[l])
        g_mix = norm_mix_g[l].reshape(1, D)
        w_o = w_out[l].astype(jnp.bfloat16)
        bias_vec, bias_rvec = _band_bias_vec(rel_bias_a[l])

        (qa, ka, va, ga, qb, kb, vb, gb, qm, gm, qi, kiwi, ki4, ki, kbb, vbb, ak, av) = _proj_call(
            xp, g_mix, w, rope_p, keep)
        mk, mv = _memkv_call(mem_prompt.reshape(B * n_mem, D), norm_mem_g[l].reshape(1, D),
                             w_mem_kv[l].astype(jnp.bfloat16))
        mk = mk.reshape(B, n_mem, D_M)
        mv = mv.reshape(B, n_mem, D_M)
        oa, om = _band_prompt_call(qa, ka, va, ga, qm, gm, mk, mv, bias_rvec)
        ob = _dsa_prompt_call(qb, qi, kiwi, gb, kbb, vbb, ki4)
        xp = _merge_call(xp.reshape(B * S, D), oa.reshape(B * S, D_A), ob.reshape(B * S, D_B),
                         om.reshape(B * S, D_M), w_o, g_final, last).reshape(B, S, D)
        for lst, t in zip(outs_p, (ak.reshape(B, keep, H_A, D_HEAD), av.reshape(B, keep, H_A, D_HEAD),
                                   kb.reshape(B, S, H_B, D_HEAD), vb.reshape(B, S, H_B, D_HEAD), ki,
                                   mk.reshape(B, n_mem, H_M, D_HEAD), mv.reshape(B, n_mem, H_M, D_HEAD))):
            lst.append(t)

        (qa, ka, va, ga, qb, kb, vb, gb, qm, gm, qi, kiwi, ki4, ki, kbb, vbb, ak, av) = _proj_call(
            xs.reshape(1, Bs * T, D), g_mix, w, rope_s, Bs * T)
        per_b = lambda t: t.reshape(Bs, T, t.shape[-1])
        oa, om = _band_sample_call(
            per_b(qa), cache_a_k[l].reshape(Bs, Pa, D_A), cache_a_v[l].reshape(Bs, Pa, D_A),
            per_b(ka), per_b(va), per_b(ga), per_b(qm), per_b(gm),
            cache_mem_k[l].reshape(Bs, n_mem, D_M), cache_mem_v[l].reshape(Bs, n_mem, D_M), bias_vec)
        rep = lambda t: jnp.tile(per_b(t), (1, LANES // T, 1))
        kic = jnp.tile(cache_b_kidx[l], (1, 1, LANES // D_IDX)).astype(jnp.bfloat16)
        ob = _dsa_sample_call(rep(qb), rep(qi), rep(kiwi), rep(gb),
                              cache_b_k[l].reshape(Bs, P, D_B), cache_b_v[l].reshape(Bs, P, D_B), kic,
                              per_b(kb), per_b(vb), per_b(ki4))[:, :T]
        xs = _merge_call(xs.reshape(Bs * T, D), oa.reshape(Bs * T, D_A), ob.reshape(Bs * T, D_B),
                         om.reshape(Bs * T, D_M), w_o, g_final, last).reshape(Bs, T, D)
        for lst, t in zip(outs_s, (ak.reshape(Bs, T, H_A, D_HEAD), av.reshape(Bs, T, H_A, D_HEAD),
                                   kb.reshape(Bs, T, H_B, D_HEAD), vb.reshape(Bs, T, H_B, D_HEAD),
                                   ki.reshape(Bs, T, D_IDX))):
            lst.append(t)

    st = lambda ts: jnp.stack(ts, axis=0)
    return (xp, xs) + tuple(st(t) for t in outs_p) + tuple(st(t) for t in outs_s)
```

```python
import functools
import math

import jax
import jax.numpy as jnp
from jax import lax
from jax.experimental import pallas as pl
from jax.experimental.pallas import tpu as pltpu

CHUNK = 64
D_HEAD = 64
H_A = 6
H_B = 6
H_M = 4
D_A = H_A * D_HEAD
D_B = H_B * D_HEAD
D_M = H_M * D_HEAD
A_LEFT_CHUNKS = 8
A_WINDOW = A_LEFT_CHUNKS * CHUNK
REL_CLIP = 256
H_IDX = 8
D_IDX = 32
TOPK_MAX = 256
ROPE_THETA = 10000.0
EPS = 1e-6

LANES = 128
VMEM_LIMIT_BYTES = 56 * 1024 * 1024

C_QA, C_KA, C_VA, C_GA = 0, D_A, 2 * D_A, 3 * D_A
C_QB = 4 * D_A
C_KB, C_VB, C_GB = C_QB + D_B, C_QB + 2 * D_B, C_QB + 3 * D_B
C_QM = C_QB + 4 * D_B
C_GM = C_QM + D_M
C_QI = C_GM + D_M
C_KIWI = C_QI + H_IDX * D_IDX
C_KI4 = C_KIWI + LANES
W_COLS = C_KI4 + LANES

PROJ_TM = 512
MERGE_TM = 1024
BAND_TQ = 256
BAND_WIN = A_WINDOW + BAND_TQ
DSA_TQ = 256

LOG2E = math.log2(math.e)
QK_SCALE = (D_HEAD ** -0.5) * LOG2E
BIAS_VEC = 8 * LANES

_NT = (((1,), (1,)), ((), ()))
_INT_MIN = -2 ** 31
_NEG_INF = float("-inf")


def _cparams(n_axes):
    return pltpu.CompilerParams(
        dimension_semantics=("arbitrary",) * n_axes,
        vmem_limit_bytes=VMEM_LIMIT_BYTES)


def _silu(g):
    return g * (1.0 / (1.0 + jnp.exp(-g)))


def _rms_scale(x, g):
    ms = jnp.mean(x * x, axis=-1, keepdims=True)
    return (x * lax.rsqrt(ms + EPS)) * g


def _rope_slab(z, cos, sin, half):
    lane = lax.broadcasted_iota(jnp.int32, z.shape, 1)
    first = (lane & (2 * half - 1)) < half
    partner = jnp.where(first, pltpu.roll(z, LANES - half, 1), pltpu.roll(z, half, 1))
    return z * cos + partner * sin


def _proj_kernel(x_ref, g_ref, w_ref, rope_ref,
                 qa_ref, ka_ref, va_ref, ga_ref, qb_ref, kb_ref, vb_ref, gb_ref,
                 qm_ref, gm_ref, qi_ref, kiwi_ref, ki4_ref, ki_ref, kbb_ref, vbb_ref, ak_ref, av_ref,
                 *, n_tiles, keep_tiles):
    i = pl.program_id(1)
    scale = QK_SCALE
    tm = x_ref.shape[1]
    n_sub = 2 if tm % 32 == 0 else 1
    kept = []
    for u in range(n_sub):
        rows = slice(u * tm // n_sub, (u + 1) * tm // n_sub)
        xn = _rms_scale(x_ref[0, rows, :], g_ref[...]).astype(jnp.bfloat16)

        def proj(c0, n, xn=xn):
            return jnp.dot(xn, w_ref[:, c0:c0 + n], preferred_element_type=jnp.float32)

        z = proj(C_QA, 2 * D_A)
        qa_ref[0, rows, :] = (z[:, :D_A] * scale).astype(jnp.bfloat16)
        ka = z[:, D_A:]
        ka_ref[0, rows, :] = ka.astype(jnp.bfloat16)
        z = proj(C_VA, 2 * D_A)
        va = z[:, :D_A]
        va_ref[0, rows, :] = va.astype(jnp.bfloat16)
        ga_ref[0, rows, :] = _silu(z[:, D_A:]).astype(jnp.bfloat16)
        kept.append((rows, ka, va))

        cos64, sin64 = rope_ref[0, rows, :], rope_ref[1, rows, :]
        cos32, sin32 = rope_ref[2, rows, :], rope_ref[3, rows, :]
        cos_kw, sin_kw = rope_ref[4, rows, :], rope_ref[5, rows, :]
        z = proj(C_QB, 2 * D_B)
        for s in range(D_B // LANES):
            sl = slice(s * LANES, (s + 1) * LANES)
            qb = _rope_slab(z[:, sl], cos64, sin64, D_HEAD // 2)
            qb_ref[0, rows, sl] = (qb * scale).astype(jnp.bfloat16)
            kb = _rope_slab(z[:, D_B + s * LANES:D_B + (s + 1) * LANES], cos64, sin64, D_HEAD // 2)
            kb_ref[0, rows, sl] = kb
            kbb_ref[0, rows, sl] = kb.astype(jnp.bfloat16)
        z = proj(C_VB, 2 * D_B)
        vb_ref[0, rows, :] = z[:, :D_B]
        vbb_ref[0, rows, :] = z[:, :D_B].astype(jnp.bfloat16)
        gb_ref[0, rows, :] = _silu(z[:, D_B:]).astype(jnp.bfloat16)
        qm_ref[0, rows, :] = (proj(C_QM, D_M) * scale).astype(jnp.bfloat16)
        gm_ref[0, rows, :] = _silu(proj(C_GM, D_M)).astype(jnp.bfloat16)
        z = proj(C_QI, H_IDX * D_IDX)
        for s in range(H_IDX * D_IDX // LANES):
            sl = slice(s * LANES, (s + 1) * LANES)
            qi_ref[0, rows, sl] = _rope_slab(z[:, sl], cos32, sin32, D_IDX // 2).astype(jnp.bfloat16)
        z = proj(C_KIWI, 2 * LANES)
        kiwi = _rope_slab(z[:, :LANES], cos_kw, sin_kw, D_IDX // 2)
        kiwi_ref[0, rows, :] = kiwi
        ki_ref[0, rows, :] = kiwi[:, :D_IDX]
        ki4_ref[0, rows, :] = _rope_slab(z[:, LANES:], cos32, sin32, D_IDX // 2).astype(jnp.bfloat16)

    @pl.when(i >= n_tiles - keep_tiles)
    def _():
        for rows, ka, va in kept:
            ak_ref[0, rows, :] = ka
            av_ref[0, rows, :] = va


def _proj_call(x, g, w, rope, keep_rows):
    B, S, D = x.shape
    tm = min(PROJ_TM, S)
    n_tiles = S // tm
    keep_tiles = keep_rows // tm
    assert n_tiles * tm == S and keep_tiles * tm == keep_rows

    def tile(n, dtype):
        return (jax.ShapeDtypeStruct((B, S, n), dtype),
                pl.BlockSpec((1, tm, n), lambda b, i: (b, i, 0)))

    f32, bf16 = jnp.float32, jnp.bfloat16
    outs = [tile(D_A, bf16), tile(D_A, bf16), tile(D_A, bf16), tile(D_A, bf16),
            tile(D_B, bf16), tile(D_B, f32), tile(D_B, f32), tile(D_B, bf16),
            tile(D_M, bf16), tile(D_M, bf16), tile(H_IDX * D_IDX, bf16),
            tile(LANES, f32), tile(LANES, bf16), tile(D_IDX, f32), tile(D_B, bf16), tile(D_B, bf16)]
    keep_spec = pl.BlockSpec(
        (1, tm, D_A), lambda b, i: (b, jnp.maximum(i - (n_tiles - keep_tiles), 0), 0))
    outs += [(jax.ShapeDtypeStruct((B, keep_rows, D_A), f32), keep_spec)] * 2
    return pl.pallas_call(
        functools.partial(_proj_kernel, n_tiles=n_tiles, keep_tiles=keep_tiles),
        grid=(B, n_tiles),
        in_specs=[pl.BlockSpec((1, tm, D), lambda b, i: (b, i, 0)),
                  pl.BlockSpec((1, D), lambda b, i: (0, 0)),
                  pl.BlockSpec((D, W_COLS), lambda b, i: (0, 0)),
                  pl.BlockSpec((6, tm, LANES), lambda b, i: (0, i, 0))],
        out_specs=[o[1] for o in outs],
        out_shape=[o[0] for o in outs],
        compiler_params=_cparams(2),
        name="proj",
    )(x, g, w, rope)


def _memkv_kernel(m_ref, g_ref, w_ref, mk_ref, mv_ref):
    xn = _rms_scale(m_ref[...], g_ref[...]).astype(jnp.bfloat16)
    mk_ref[...] = jnp.dot(xn, w_ref[:, :D_M], preferred_element_type=jnp.float32)
    mv_ref[...] = jnp.dot(xn, w_ref[:, D_M:], preferred_element_type=jnp.float32)


def _memkv_call(mem2d, g, w):
    R, D = mem2d.shape
    tm = min(PROJ_TM, R)
    assert R % tm == 0
    row = lambda n: pl.BlockSpec((tm, n), lambda i: (i, 0))
    return pl.pallas_call(
        _memkv_kernel,
        grid=(R // tm,),
        in_specs=[row(D), pl.BlockSpec((1, D), lambda i: (0, 0)),
                  pl.BlockSpec((D, 2 * D_M), lambda i: (0, 0))],
        out_specs=[row(D_M), row(D_M)],
        out_shape=[jax.ShapeDtypeStruct((R, D_M), jnp.float32)] * 2,
        compiler_params=_cparams(1),
        name="memkv",
    )(mem2d, g, w)


def _softmax_rows_pv(s, v):
    m = jnp.max(s, axis=1, keepdims=True)
    p = jnp.exp2(s - m)
    l = jnp.sum(p, axis=1, keepdims=True)
    o = jnp.dot(p.astype(jnp.bfloat16), v, preferred_element_type=jnp.float32)
    return o * (1.0 / l)


def _keep_lanes(slab, lo, hi):
    x = slab.astype(jnp.float32)
    lane = lax.broadcasted_iota(jnp.int32, x.shape, 1)
    return jnp.where(lane >= lo, jnp.where(lane < hi, x, 0.0), 0.0).astype(jnp.bfloat16)


def _pair_attend(q_slab, k_slab, v_slab, bias_fn):
    lane = lax.broadcasted_iota(jnp.int32, q_slab.shape, 1)
    out = None
    for hh in range(2):
        s = lax.dot_general(_keep_lanes(q_slab, hh * D_HEAD, (hh + 1) * D_HEAD), k_slab, _NT,
                            preferred_element_type=jnp.float32)
        o = _softmax_rows_pv(bias_fn(s, hh), v_slab)
        out = o if hh == 0 else jnp.where(lane < D_HEAD, out, o)
    return out


def _mem_attend(qm_ref, gm_ref, mk_ref, mv_ref, om_ref):
    for p in range(D_M // LANES):
        sl = slice(p * LANES, (p + 1) * LANES)
        o = _pair_attend(qm_ref[0, :, sl], mk_ref[0, :, sl].astype(jnp.bfloat16),
                         mv_ref[0, :, sl].astype(jnp.bfloat16), lambda s, hh: s)
        om_ref[0, :, sl] = (o * gm_ref[0, :, sl].astype(jnp.float32)).astype(jnp.bfloat16)


def _build_band_bias(vec_ref, bias_ref, n_q, n_valid, chunked):
    n_keys = bias_ref.shape[2]
    i = lax.broadcasted_iota(jnp.int32, (n_q, n_keys), 0)
    r = lax.broadcasted_iota(jnp.int32, (n_q, n_keys), 1)
    lo = ((i >> 6) << 6) if chunked else jnp.zeros_like(i)
    for h in range(bias_ref.shape[0]):
        base = jnp.broadcast_to(vec_ref[h:h + 1, :], (n_q, BIAS_VEC))
        rolled = pltpu.roll(base, 0, 1, stride=1, stride_axis=0)
        b = rolled[:, :n_keys] * LOG2E
        bias_ref[h] = jnp.where(r >= lo, jnp.where(r < lo + n_valid, b, _NEG_INF), _NEG_INF)


def _build_band_bias_t(rvec_ref, bias_ref):
    n_keys, n_q = bias_ref.shape[1], bias_ref.shape[2]
    r = lax.broadcasted_iota(jnp.int32, (n_keys, n_q), 0)
    i = lax.broadcasted_iota(jnp.int32, (n_keys, n_q), 1)
    lo = (i >> 6) << 6
    for h in range(bias_ref.shape[0]):
        base = jnp.broadcast_to(rvec_ref[h:h + 1, :], (n_keys, BIAS_VEC))
        rolled = pltpu.roll(base, 0, 1, stride=1, stride_axis=0)
        b = rolled[:, :n_q] * LOG2E
        bias_ref[h] = jnp.where(r >= lo, jnp.where(r < lo + A_WINDOW + CHUNK, b, _NEG_INF), _NEG_INF)


def _band_prompt_kernel(qa_ref, ka_ref, va_ref, ga_ref, qm_ref, gm_ref, mk_ref, mv_ref, rvec_ref,
                        oa_ref, om_ref, kpad, vt_blk, mk_bf, mv_t, bias_ref, s_scr, p_scr):
    j = pl.program_id(1)
    n_front = A_WINDOW // BAND_TQ
    n_win = BAND_WIN // BAND_TQ

    @pl.when(jnp.logical_and(pl.program_id(0) == 0, j == 0))
    def _():
        _build_band_bias_t(rvec_ref, bias_ref)

    @pl.when(j == 0)
    def _():
        kpad[:A_WINDOW, :] = jnp.zeros((A_WINDOW, D_A), jnp.bfloat16)
        kpad[A_WINDOW:, :] = ka_ref[0]
        for t in range(vt_blk.shape[0]):
            if t < n_front:
                vt_blk[t] = jnp.zeros((D_A, BAND_TQ), jnp.bfloat16)
            else:
                rows = slice((t - n_front) * BAND_TQ, (t - n_front + 1) * BAND_TQ)
                vt_blk[t] = va_ref[0, rows, :].astype(jnp.float32).T.astype(jnp.bfloat16)
        mk_bf[...] = mk_ref[0].astype(jnp.bfloat16)
        mv_t[...] = mv_ref[0].T.astype(jnp.bfloat16)

    start = pl.multiple_of(j * BAND_TQ, BAND_TQ)

    def store_a(sl, o):
        oa_ref[0, :, sl] = o

    def store_m(sl, o):
        om_ref[0, :, sl] = o

    def run(mask_front):
        def band_bias(h, rows):
            b = bias_ref[h, rows, :]
            if mask_front:
                n = rows.stop - rows.start
                pos = lax.broadcasted_iota(jnp.int32, (n, BAND_TQ), 0) + (rows.start + j * BAND_TQ)
                b = jnp.where(pos >= A_WINDOW, b, _NEG_INF)
            return b

        _kq_attend(band_bias, qa_ref[0], ga_ref[0],
                   lambda sl: kpad[pl.ds(start, BAND_WIN), sl],
                   lambda sl: jnp.concatenate([vt_blk[j + u, sl, :] for u in range(n_win)], axis=1),
                   store_a, s_scr, p_scr)
        _kq_attend(lambda h, rows: None, qm_ref[0], gm_ref[0], lambda sl: mk_bf[:, sl],
                   lambda sl: mv_t[sl, :], store_m, s_scr, p_scr, buf0=D_A // LANES)

    pl.when(j < n_front)(lambda: run(True))
    pl.when(j >= n_front)(lambda: run(False))


def _band_prompt_call(qa, ka, va, ga, qm, gm, mk, mv, bias_rvec):
    B, S, _ = qa.shape
    nq = S // BAND_TQ
    n_mem = mk.shape[1]
    assert nq * BAND_TQ == S and A_WINDOW % BAND_TQ == 0
    qblk = lambda n: pl.BlockSpec((1, BAND_TQ, n), lambda b, j: (b, j, 0))
    full = lambda r, n: pl.BlockSpec((1, r, n), lambda b, j: (b, 0, 0))
    return pl.pallas_call(
        _band_prompt_kernel,
        grid=(B, nq),
        in_specs=[qblk(D_A), full(S, D_A), full(S, D_A), qblk(D_A), qblk(D_M), qblk(D_M),
                  full(n_mem, D_M), full(n_mem, D_M),
                  pl.BlockSpec((H_A, BIAS_VEC), lambda b, j: (0, 0))],
        out_specs=[qblk(D_A), qblk(D_M)],
        out_shape=[jax.ShapeDtypeStruct((B, S, D_A), jnp.bfloat16),
                   jax.ShapeDtypeStruct((B, S, D_M), jnp.bfloat16)],
        scratch_shapes=[pltpu.VMEM((S + A_WINDOW, D_A), jnp.bfloat16),
                        pltpu.VMEM((nq + A_WINDOW // BAND_TQ, D_A, BAND_TQ), jnp.bfloat16),
                        pltpu.VMEM((n_mem, D_M), jnp.bfloat16),
                        pltpu.VMEM((D_M, n_mem), jnp.bfloat16),
                        pltpu.VMEM((H_A, BAND_WIN, BAND_TQ), jnp.float32),
                        pltpu.VMEM(((D_A + D_M) // LANES, BAND_WIN, 2 * BAND_TQ), jnp.float32),
                        pltpu.VMEM(((D_A + D_M) // LANES, BAND_WIN, 2 * BAND_TQ), jnp.bfloat16)],
        compiler_params=_cparams(2),
        name="band_prompt",
    )(qa, ka, va, ga, qm, gm, mk, mv, bias_rvec)


def _band_sample_kernel(qa_ref, kc_ref, vc_ref, kn_ref, vn_ref, ga_ref, qm_ref, gm_ref,
                        mk_ref, mv_ref, vec_ref, oa_ref, om_ref, kcat, vcat, bias_ref):
    P = kc_ref.shape[1]
    T = kn_ref.shape[1]
    pad = kcat.shape[0] - P - T

    @pl.when(pl.program_id(0) == 0)
    def _():
        _build_band_bias(vec_ref, bias_ref, T, P + T, False)

    kcat[:P, :] = kc_ref[0].astype(jnp.bfloat16)
    vcat[:P, :] = vc_ref[0].astype(jnp.bfloat16)
    kcat[P:P + T, :] = kn_ref[0]
    vcat[P:P + T, :] = vn_ref[0]
    zeros = jnp.zeros((pad, D_A), jnp.bfloat16)
    kcat[P + T:, :] = zeros
    vcat[P + T:, :] = zeros
    for p in range(D_A // LANES):
        sl = slice(p * LANES, (p + 1) * LANES)
        o = _pair_attend(qa_ref[0, :, sl], kcat[:, sl], vcat[:, sl],
                         lambda s, hh, p=p: s + bias_ref[2 * p + hh])
        oa_ref[0, :, sl] = (o * ga_ref[0, :, sl].astype(jnp.float32)).astype(jnp.bfloat16)
    _mem_attend(qm_ref, gm_ref, mk_ref, mv_ref, om_ref)


def _band_sample_call(qa, kc, vc, kn, vn, ga, qm, gm, mk, mv, bias_vec):
    B, T, _ = qa.shape
    P = kc.shape[1]
    n_keys = -(-(P + T) // LANES) * LANES
    n_mem = mk.shape[1]
    blk = lambda r, n: pl.BlockSpec((1, r, n), lambda b: (b, 0, 0))
    return pl.pallas_call(
        _band_sample_kernel,
        grid=(B,),
        in_specs=[blk(T, D_A), blk(P, D_A), blk(P, D_A), blk(T, D_A), blk(T, D_A), blk(T, D_A),
                  blk(T, D_M), blk(T, D_M), blk(n_mem, D_M), blk(n_mem, D_M),
                  pl.BlockSpec((H_A, BIAS_VEC), lambda b: (0, 0))],
        out_specs=[blk(T, D_A), blk(T, D_M)],
        out_shape=[jax.ShapeDtypeStruct((B, T, D_A), jnp.bfloat16),
                   jax.ShapeDtypeStruct((B, T, D_M), jnp.bfloat16)],
        scratch_shapes=[pltpu.VMEM((n_keys, D_A), jnp.bfloat16)] * 2
        + [pltpu.VMEM((H_A, T, n_keys), jnp.float32)],
        compiler_params=_cparams(1),
        name="band_sample",
    )(qa, kc, vc, kn, vn, ga, qm, gm, mk, mv, bias_vec)


def _key_to_f32(k):
    bits = k ^ ((k >> 31) & jnp.int32(0x7FFFFFFF))
    return lax.bitcast_convert_type(bits, jnp.float32)


def _count(pred_f32):
    ones = jnp.ones((8, pred_f32.shape[0]), jnp.bfloat16)
    c = jnp.dot(ones, pred_f32.astype(jnp.bfloat16), preferred_element_type=jnp.float32)
    return c[0:1, :]


def _dsa_select(sc_ref, probs, tq, topk):
    one, zero = jnp.float32(1.0), jnp.float32(0.0)
    live = [(off, tk) for off, tk in probs if tk > topk]
    n_live = len(live)

    def sc_of(n):
        off, tk = live[n]
        return sc_ref[off:off + tk, :]

    if live:
        def thr_step(i, ts):
            bit = lax.shift_left(jnp.int32(1), 31 - i)
            out = []
            for n in range(n_live):
                cand = ts[n] + bit
                c = _count(jnp.where(sc_of(n) >= _key_to_f32(cand), one, zero))
                out.append(jnp.where(c >= topk, cand, ts[n]))
            return tuple(out)

        start = tuple(jnp.full((1, tq), _INT_MIN, jnp.int32) for _ in range(n_live))
        t_keys = lax.fori_loop(0, 32, thr_step, start)
        thrs = [jnp.where(t == _INT_MIN, _NEG_INF, _key_to_f32(t)) for t in t_keys]

        for n in range(n_live):
            off, tk = live[n]
            c_ge = _count(jnp.where(sc_of(n) >= thrs[n], one, zero))

            def tie_search(n=n, off=off, tk=tk):
                need = topk - _count(jnp.where(sc_of(n) > thrs[n], one, zero))
                tile = 2 * LANES

                def step(first, size, state):
                    before, jmax = state
                    r = lax.broadcasted_iota(jnp.int32, (size, size), 0)
                    c = lax.broadcasted_iota(jnp.int32, (size, size), 1)
                    lower = jnp.where(c <= r, one, zero).astype(jnp.bfloat16)
                    tied = jnp.where(sc_ref[pl.ds(off + first, size), :] == thrs[n], one, zero)
                    rank = before + jnp.dot(lower, tied.astype(jnp.bfloat16),
                                            preferred_element_type=jnp.float32)
                    kidx = (lax.broadcasted_iota(jnp.int32, (size, tq), 0) + first).astype(jnp.float32)
                    taken = jnp.where(tied > 0, jnp.where(rank <= need, kidx, -one), -one)
                    jmax = jnp.maximum(jmax, jnp.max(taken, axis=0, keepdims=True))
                    return rank[size - 1:size, :], jmax

                state = (jnp.zeros((1, tq), jnp.float32), jnp.full((1, tq), -1.0, jnp.float32))
                n_full, rest = divmod(tk, tile)
                state = lax.fori_loop(
                    0, n_full, lambda t, st: step(pl.multiple_of(t * tile, tile), tile, st), state)
                if rest:
                    state = step(n_full * tile, rest, state)
                return state[1].astype(jnp.int32)

            def keep_with_ties(n=n, off=off, tk=tk, tie_search=tie_search):
                jmax = tie_search()
                sc = sc_of(n)
                kidx = lax.broadcasted_iota(jnp.int32, (tk, tq), 0)
                fin = jnp.where(jnp.abs(sc) < jnp.float32(jnp.inf), zero, _NEG_INF)
                sc_ref[off:off + tk, :] = jnp.where(
                    sc > thrs[n], fin,
                    jnp.where(sc == thrs[n], jnp.where(kidx <= jmax, fin, _NEG_INF), _NEG_INF))

            def keep_plain(n=n, off=off, tk=tk):
                sc = sc_of(n)
                fin = jnp.where(jnp.abs(sc) < jnp.float32(jnp.inf), zero, _NEG_INF)
                sc_ref[off:off + tk, :] = jnp.where(sc >= thrs[n], fin, _NEG_INF)

            has_excess = jnp.max(jnp.where(c_ge > topk, one, zero)) > 0
            lax.cond(has_excess, keep_with_ties, keep_plain)

    for off, tk in probs:
        if tk <= topk:
            sc = sc_ref[off:off + tk, :]
            sc_ref[off:off + tk, :] = jnp.where(jnp.abs(sc) < jnp.float32(jnp.inf), zero, _NEG_INF)


def _dsa_scores(limit, qi, kiwi, ki4):
    tk, tq = ki4.shape[0], qi.shape[0]
    kiwi_t = kiwi.T
    acc = jnp.zeros((tk, tq), jnp.float32)
    heads_per_slab = LANES // D_IDX

    def head_q(h):
        slab = qi[:, (h // heads_per_slab) * LANES:(h // heads_per_slab + 1) * LANES]
        lo = (h % heads_per_slab) * D_IDX
        return _keep_lanes(slab, lo, lo + D_IDX)

    for h in range(0, H_IDX, 2):
        d = lax.dot_general(ki4, jnp.concatenate([head_q(h), head_q(h + 1)], axis=0), _NT,
                            preferred_element_type=jnp.float32)
        for e in range(2):
            w = kiwi_t[D_IDX + h + e:D_IDX + h + e + 1, :] * ((D_IDX ** -0.5) * (H_IDX ** -0.5))
            acc = acc + w * jnp.maximum(d[:, e * tq:(e + 1) * tq], 0.0)
    if limit is not None:
        kidx = lax.broadcasted_iota(jnp.int32, (tk, tq), 0)
        acc = jnp.where(kidx < limit, acc, _NEG_INF)
    return acc


ATT_TILE = 64


def _kq_attend(bias_rows, q, g, k_slab_of, vt_slab_of, store, s_scr, p_scr, buf0=0):
    tq = q.shape[0]
    for p in range(q.shape[1] // LANES):
        sl = slice(p * LANES, (p + 1) * LANES)
        buf = (buf0 + p) % s_scr.shape[0]
        q_slab = q[:, sl]
        q_pair = jnp.concatenate([_keep_lanes(q_slab, 0, D_HEAD), _keep_lanes(q_slab, D_HEAD, LANES)],
                                 axis=0)
        k_slab = k_slab_of(sl)
        tk = k_slab.shape[0]
        s_scr[buf, :tk, :] = lax.dot_general(k_slab, q_pair, _NT, preferred_element_type=jnp.float32)
        tiles = [slice(r, min(r + ATT_TILE, tk)) for r in range(0, tk, ATT_TILE)]

        def biased(rows, hh):
            s = s_scr[buf, rows, hh * tq:(hh + 1) * tq]
            bias = bias_rows(2 * p + hh, rows)
            return s if bias is None else s + bias

        def fold(x, op):
            n = x.shape[0]
            return op(x.reshape(n // 8, 8, tq), axis=0)

        m8 = [jnp.full((8, tq), _NEG_INF, jnp.float32) for _ in range(2)]
        for rows in tiles:
            for hh in range(2):
                m8[hh] = jnp.maximum(m8[hh], fold(biased(rows, hh), jnp.max))
        m = [jnp.max(x, axis=0, keepdims=True) for x in m8]
        l8 = [jnp.zeros((8, tq), jnp.float32) for _ in range(2)]
        for rows in tiles:
            for hh in range(2):
                pr = jnp.exp2(biased(rows, hh) - m[hh])
                l8[hh] = l8[hh] + fold(pr, jnp.sum)
                p_scr[buf, rows, hh * tq:(hh + 1) * tq] = pr.astype(jnp.bfloat16)
        inv_l = [1.0 / jnp.sum(x, axis=0, keepdims=True) for x in l8]
        o_pair = jnp.dot(vt_slab_of(sl), p_scr[buf, :tk, :],
                         preferred_element_type=jnp.float32)
        ot = jnp.concatenate([o_pair[:D_HEAD, :tq] * inv_l[0], o_pair[D_HEAD:, tq:] * inv_l[1]], axis=0)
        store(sl, (ot.T * g[:, sl].astype(jnp.float32)).astype(jnp.bfloat16))


def _dsa_blocks(S, tq):
    return [(tq * j * (j + 1) // 2, (j + 1) * tq) for j in range(S // tq)]


def _own_region():
    return pl.when(pl.program_id(0) >= 0)


def _dsa_select_kernel(qi_ref, kiwi_ref, ki4_ref, keep_ref, sc_ref, *, topk, tq):
    own_region = _own_region()
    probs = _dsa_blocks(ki4_ref.shape[1], tq)
    for j, (off, tk) in enumerate(probs):
        rows = slice(j * tq, (j + 1) * tq)

        @own_region
        def _(j=j, tk=tk, off=off, rows=rows):
            qpos = j * tq + lax.broadcasted_iota(jnp.int32, (1, tq), 1)
            limit = ((qpos >> 6) + 1) << 6
            sc_ref[off:off + tk, :] = _dsa_scores(limit, qi_ref[0, rows, :], kiwi_ref[0, rows, :],
                                                  ki4_ref[0, :tk, :])

    _dsa_select(sc_ref, probs, tq, topk)
    keep_ref[0] = sc_ref[...].astype(jnp.bfloat16)


def _dsa_attend_kernel(q_ref, g_ref, k_ref, v_ref, keep_ref, o_ref, vt, bias_scr, s_scr, p_scr, *, tq):
    own_region = _own_region()
    S = k_ref.shape[1]
    for t in range(S // tq):
        vt[t] = v_ref[0, t * tq:(t + 1) * tq, :].astype(jnp.float32).T.astype(jnp.bfloat16)
    for j, (off, tk) in enumerate(_dsa_blocks(S, tq)):
        rows = slice(j * tq, (j + 1) * tq)

        @own_region
        def _(j=j, tk=tk, off=off, rows=rows):
            def store(sl, o):
                o_ref[0, rows, sl] = o

            bias_scr[:tk, :] = keep_ref[0, off:off + tk, :].astype(jnp.float32)
            _kq_attend(lambda h, r: bias_scr[r, :], q_ref[0, rows, :], g_ref[0, rows, :],
                       lambda sl: k_ref[0, :tk, sl],
                       lambda sl: jnp.concatenate([vt[t, sl, :] for t in range(j + 1)], axis=1), store,
                       s_scr, p_scr)


def _dsa_prompt_call(qb, qi, kiwi, gb, kb, vb, ki4):
    B, S, _ = qb.shape
    tq = DSA_TQ
    nq = S // tq
    assert CHUNK == 64 and nq * tq == S
    topk = min(TOPK_MAX, S // 4)
    n_rows = tq * nq * (nq + 1) // 2
    full = lambda n: pl.BlockSpec((1, S, n), lambda b: (b, 0, 0))
    keep_spec = pl.BlockSpec((1, n_rows, tq), lambda b: (b, 0, 0))
    keep = pl.pallas_call(
        functools.partial(_dsa_select_kernel, topk=topk, tq=tq),
        grid=(B,),
        in_specs=[full(H_IDX * D_IDX), full(LANES), full(LANES)],
        out_specs=keep_spec,
        out_shape=jax.ShapeDtypeStruct((B, n_rows, tq), jnp.bfloat16),
        scratch_shapes=[pltpu.VMEM((n_rows, tq), jnp.float32)],
        compiler_params=_cparams(1),
        name="dsa_select",
    )(qi, kiwi, ki4)
    return pl.pallas_call(
        functools.partial(_dsa_attend_kernel, tq=tq),
        grid=(B,),
        in_specs=[full(D_B), full(D_B), full(D_B), full(D_B), keep_spec],
        out_specs=full(D_B),
        out_shape=jax.ShapeDtypeStruct((B, S, D_B), jnp.bfloat16),
        scratch_shapes=[pltpu.VMEM((nq, D_B, tq), jnp.bfloat16),
                        pltpu.VMEM((S, tq), jnp.float32),
                        pltpu.VMEM((D_B // LANES, S, 2 * tq), jnp.float32),
                        pltpu.VMEM((D_B // LANES, S, 2 * tq), jnp.bfloat16)],
        compiler_params=_cparams(1),
        name="dsa_attend",
    )(qb, gb, kb, vb, keep)


def _dsa_sample_kernel(q_ref, qi_ref, kiwi_ref, g_ref, kc_ref, vc_ref, kic_ref,
                       kn_ref, vn_ref, kin_ref, o_ref, kbf, vt, ki4s, sc_ref, s_scr, p_scr, *, topk):
    P = kc_ref.shape[1]
    T = kn_ref.shape[1]
    tq = q_ref.shape[1]
    kbf[:P, :] = kc_ref[0].astype(jnp.bfloat16)
    kbf[P:, :] = kn_ref[0].astype(jnp.bfloat16)
    vt[:, :P] = vc_ref[0].T.astype(jnp.bfloat16)
    vn_tile = jnp.concatenate([vn_ref[0], jnp.zeros((LANES - T, D_B), jnp.float32)], axis=0)
    vt[:, P:] = vn_tile.T[:, :T].astype(jnp.bfloat16)
    ki4s[:P, :] = kic_ref[0]
    ki4s[P:, :] = kin_ref[0]
    sc_ref[...] = _dsa_scores(None, qi_ref[0], kiwi_ref[0], ki4s[...])
    _dsa_select(sc_ref, [(0, P + T)], tq, topk)

    def store(sl, o):
        o_ref[0, :, sl] = o

    _kq_attend(lambda h, r: sc_ref[r, :], q_ref[0], g_ref[0], lambda sl: kbf[:, sl], lambda sl: vt[sl, :],
               store, s_scr, p_scr)


def _dsa_sample_call(qb, qi, kiwi, gb, kc, vc, kic, kn, vn, kin):
    B, tq, _ = qb.shape
    P, T = kc.shape[1], kn.shape[1]
    tk = P + T
    topk = min(TOPK_MAX, tk // 4)
    blk = lambda r, n: pl.BlockSpec((1, r, n), lambda b: (b, 0, 0))
    return pl.pallas_call(
        functools.partial(_dsa_sample_kernel, topk=topk),
        grid=(B,),
        in_specs=[blk(tq, D_B), blk(tq, H_IDX * D_IDX), blk(tq, LANES), blk(tq, D_B),
                  blk(P, D_B), blk(P, D_B), blk(P, LANES),
                  blk(T, D_B), blk(T, D_B), blk(T, LANES)],
        out_specs=blk(tq, D_B),
        out_shape=jax.ShapeDtypeStruct((B, tq, D_B), jnp.bfloat16),
        scratch_shapes=[pltpu.VMEM((tk, D_B), jnp.bfloat16),
                        pltpu.VMEM((D_B, tk), jnp.bfloat16),
                        pltpu.VMEM((tk, LANES), jnp.bfloat16),
                        pltpu.VMEM((tk, tq), jnp.float32),
                        pltpu.VMEM((1, tk, 2 * tq), jnp.float32),
                        pltpu.VMEM((1, tk, 2 * tq), jnp.bfloat16)],
        compiler_params=_cparams(1),
        name="dsa_sample",
    )(qb, qi, kiwi, gb, kc, vc, kic, kn, vn, kin)


def _merge_kernel(x_ref, oa_ref, ob_ref, om_ref, w_ref, g_ref, y_ref, *, last):
    acc = x_ref[...]
    acc = acc + jnp.dot(oa_ref[...], w_ref[:D_A, :], preferred_element_type=jnp.float32)
    acc = acc + jnp.dot(ob_ref[...], w_ref[D_A:D_A + D_B, :], preferred_element_type=jnp.float32)
    acc = acc + jnp.dot(om_ref[...], w_ref[D_A + D_B:, :], preferred_element_type=jnp.float32)
    y_ref[...] = _rms_scale(acc, g_ref[...]) if last else acc


def _merge_call(x2d, oa, ob, om, w, g, last):
    R, D = x2d.shape
    tm = min(MERGE_TM, R)
    assert R % tm == 0
    row = lambda n: pl.BlockSpec((tm, n), lambda i: (i, 0))
    return pl.pallas_call(
        functools.partial(_merge_kernel, last=last),
        grid=(R // tm,),
        in_specs=[row(D), row(D_A), row(D_B), row(D_M),
                  pl.BlockSpec((D_A + D_B + D_M, D), lambda i: (0, 0)),
                  pl.BlockSpec((1, D), lambda i: (0, 0))],
        out_specs=row(D),
        out_shape=jax.ShapeDtypeStruct((R, D), jnp.float32),
        compiler_params=_cparams(1),
        name="merge",
    )(x2d, oa, ob, om, w, g)


def _pack_w_in(w):
    D = w.shape[0]
    ki = w[:, C_KIWI:C_KIWI + D_IDX]
    wi = w[:, C_KIWI + D_IDX:C_KIWI + D_IDX + H_IDX]
    pad = jnp.zeros((D, LANES - D_IDX - H_IDX), w.dtype)
    packed = jnp.concatenate([w[:, :C_KIWI], ki, wi, pad, jnp.tile(ki, (1, LANES // D_IDX))], axis=1)
    return packed.astype(jnp.bfloat16)


def _rope_tables(pos):
    posf = pos.astype(jnp.float32)

    def tables(d):
        half = d // 2
        inv_freq = ROPE_THETA ** (-jnp.arange(half, dtype=jnp.float32) * 2.0 / d)
        ang = posf[:, None] * inv_freq[None, :]
        cos, sin = jnp.cos(ang), jnp.sin(ang)
        return jnp.concatenate([cos, cos], axis=1), jnp.concatenate([-sin, sin], axis=1)

    cos64, sin64 = tables(D_HEAD)
    cos32, sin32 = tables(D_IDX)
    n = pos.shape[0]
    rest = LANES - D_IDX
    cos_kw = jnp.concatenate([cos32, jnp.ones((n, rest), jnp.float32)], axis=1)
    sin_kw = jnp.concatenate([sin32, jnp.zeros((n, rest), jnp.float32)], axis=1)
    rep = lambda t, d: jnp.tile(t, (1, LANES // d))
    return jnp.stack([rep(cos64, D_HEAD), rep(sin64, D_HEAD), rep(cos32, D_IDX), rep(sin32, D_IDX),
                      cos_kw, sin_kw])


def _band_bias_vec(table):
    n_far = A_WINDOW - REL_CLIP + 1
    lo_idx = A_WINDOW + REL_CLIP - (BAND_WIN - 1)
    assert lo_idx >= 0 and BIAS_VEC >= BAND_WIN + BAND_TQ - 1
    far = table[:, 2 * REL_CLIP:2 * REL_CLIP + 1].astype(jnp.float32)
    mid = table[:, lo_idx:2 * REL_CLIP][:, ::-1].astype(jnp.float32)
    vec = jnp.concatenate([jnp.tile(far, (1, n_far)), mid, jnp.tile(far, (1, BIAS_VEC - BAND_WIN))], axis=1)
    rvec = jnp.concatenate([vec[:, :1], vec[:, 1:][:, ::-1]], axis=1)
    return vec, rvec


def kernel(x_prompt, x_sample, mem_prompt, cache_a_k, cache_a_v, cache_b_k, cache_b_v, cache_b_kidx,
           cache_mem_k, cache_mem_v, norm_mix_g, w_in, rel_bias_a, norm_mem_g, w_mem_kv, w_out,
           norm_final_g):
    B, S, D = x_prompt.shape
    Bs, T, _ = x_sample.shape
    depth = w_in.shape[0]
    P = cache_b_k.shape[2]
    Pa = cache_a_k.shape[2]
    n_mem = mem_prompt.shape[1]
    keep = min(A_WINDOW, S)
    assert Pa == A_WINDOW and T <= CHUNK and LANES % T == 0

    rope_p = _rope_tables(jnp.arange(S))
    rope_s = jnp.tile(_rope_tables(P + jnp.arange(T)), (1, Bs, 1))
    g_final = norm_final_g.reshape(1, D)

    xp, xs = x_prompt, x_sample
    outs_p = [[] for _ in range(7)]
    outs_s = [[] for _ in range(5)]
    for l in range(depth):
        last = l == depth - 1
        w = _pack_w_in(w_in[l])
        g_mix = norm_mix_g[l].reshape(1, D)
        w_o = w_out[l].astype(jnp.bfloat16)
        bias_vec, bias_rvec = _band_bias_vec(rel_bias_a[l])

        (qa, ka, va, ga, qb, kb, vb, gb, qm, gm, qi, kiwi, ki4, ki, kbb, vbb, ak, av) = _proj_call(
            xp, g_mix, w, rope_p, keep)
        mk, mv = _memkv_call(mem_prompt.reshape(B * n_mem, D), norm_mem_g[l].reshape(1, D),
                             w_mem_kv[l].astype(jnp.bfloat16))
        mk = mk.reshape(B, n_mem, D_M)
        mv = mv.reshape(B, n_mem, D_M)
        oa, om = _band_prompt_call(qa, ka, va, ga, qm, gm, mk, mv, bias_rvec)
        ob = _dsa_prompt_call(qb, qi, kiwi, gb, kbb, vbb, ki4)
        xp = _merge_call(xp.reshape(B * S, D), oa.reshape(B * S, D_A), ob.reshape(B * S, D_B),
                         om.reshape(B * S, D_M), w_o, g_final, last).reshape(B, S, D)
        for lst, t in zip(outs_p, (ak.reshape(B, keep, H_A, D_HEAD), av.reshape(B, keep, H_A, D_HEAD),
                                   kb.reshape(B, S, H_B, D_HEAD), vb.reshape(B, S, H_B, D_HEAD), ki,
                                   mk.reshape(B, n_mem, H_M, D_HEAD), mv.reshape(B, n_mem, H_M, D_HEAD))):
            lst.append(t)

        (qa, ka, va, ga, qb, kb, vb, gb, qm, gm, qi, kiwi, ki4, ki, kbb, vbb, ak, av) = _proj_call(
            xs.reshape(1, Bs * T, D), g_mix, w, rope_s, Bs * T)
        per_b = lambda t: t.reshape(Bs, T, t.shape[-1])
        oa, om = _band_sample_call(
            per_b(qa), cache_a_k[l].reshape(Bs, Pa, D_A), cache_a_v[l].reshape(Bs, Pa, D_A),
            per_b(ka), per_b(va), per_b(ga), per_b(qm), per_b(gm),
            cache_mem_k[l].reshape(Bs, n_mem, D_M), cache_mem_v[l].reshape(Bs, n_mem, D_M), bias_vec)
        rep = lambda t: jnp.tile(per_b(t), (1, LANES // T, 1))
        kic = jnp.tile(cache_b_kidx[l], (1, 1, LANES // D_IDX)).astype(jnp.bfloat16)
        ob = _dsa_sample_call(rep(qb), rep(qi), rep(kiwi), rep(gb),
                              cache_b_k[l].reshape(Bs, P, D_B), cache_b_v[l].reshape(Bs, P, D_B), kic,
                              per_b(kb), per_b(vb), per_b(ki4))[:, :T]
        xs = _merge_call(xs.reshape(Bs * T, D), oa.reshape(Bs * T, D_A), ob.reshape(Bs * T, D_B),
                         om.reshape(Bs * T, D_M), w_o, g_final, last).reshape(Bs, T, D)
        for lst, t in zip(outs_s, (ak.reshape(Bs, T, H_A, D_HEAD), av.reshape(Bs, T, H_A, D_HEAD),
                                   kb.reshape(Bs, T, H_B, D_HEAD), vb.reshape(Bs, T, H_B, D_HEAD),
                                   ki.reshape(Bs, T, D_IDX))):
            lst.append(t)

    st = lambda ts: jnp.stack(ts, axis=0)
    return (xp, xs) + tuple(st(t) for t in outs_p) + tuple(st(t) for t in outs_s)
```

```python
import functools
import math

import jax
import jax.numpy as jnp
from jax import lax
from jax.experimental import pallas as pl
from jax.experimental.pallas import tpu as pltpu

CHUNK = 64
D_HEAD = 64
H_A = 6
H_B = 6
H_M = 4
D_A = H_A * D_HEAD
D_B = H_B * D_HEAD
D_M = H_M * D_HEAD
A_LEFT_CHUNKS = 8
A_WINDOW = A_LEFT_CHUNKS * CHUNK
REL_CLIP = 256
H_IDX = 8
D_IDX = 32
TOPK_MAX = 256
ROPE_THETA = 10000.0
EPS = 1e-6

LANES = 128
VMEM_LIMIT_BYTES = 56 * 1024 * 1024

C_QA, C_KA, C_VA, C_GA = 0, D_A, 2 * D_A, 3 * D_A
C_QB = 4 * D_A
C_KB, C_VB, C_GB = C_QB + D_B, C_QB + 2 * D_B, C_QB + 3 * D_B
C_QM = C_QB + 4 * D_B
C_GM = C_QM + D_M
C_QI = C_GM + D_M
C_KIWI = C_QI + H_IDX * D_IDX
C_KI4 = C_KIWI + LANES
W_COLS = C_KI4 + LANES

PROJ_TM = 512
MERGE_TM = 1024
BAND_TQ = 256
BAND_WIN = A_WINDOW + BAND_TQ
DSA_TQ = 256

LOG2E = math.log2(math.e)
QK_SCALE = (D_HEAD ** -0.5) * LOG2E
BIAS_VEC = 8 * LANES

_NT = (((1,), (1,)), ((), ()))
_INT_MIN = -2 ** 31
_NEG_INF = float("-inf")


def _cparams(n_axes):
    return pltpu.CompilerParams(
        dimension_semantics=("arbitrary",) * n_axes,
        vmem_limit_bytes=VMEM_LIMIT_BYTES)


def _silu(g):
    return g * (1.0 / (1.0 + jnp.exp(-g)))


def _rms_scale(x, g):
    ms = jnp.mean(x * x, axis=-1, keepdims=True)
    return (x * lax.rsqrt(ms + EPS)) * g


def _rope_slab(z, cos, sin, half):
    lane = lax.broadcasted_iota(jnp.int32, z.shape, 1)
    first = (lane & (2 * half - 1)) < half
    partner = jnp.where(first, pltpu.roll(z, LANES - half, 1), pltpu.roll(z, half, 1))
    return z * cos + partner * sin


def _proj_kernel(x_ref, g_ref, w_ref, rope_ref,
                 qa_ref, ka_ref, va_ref, ga_ref, qb_ref, kb_ref, vb_ref, gb_ref,
                 qm_ref, gm_ref, qi_ref, kiwi_ref, ki4_ref, ki_ref, kbb_ref, vbb_ref, ak_ref, av_ref,
                 *, n_tiles, keep_tiles):
    i = pl.program_id(1)
    scale = QK_SCALE
    tm = x_ref.shape[1]
    n_sub = 2 if tm % 32 == 0 else 1
    kept = []
    for u in range(n_sub):
        rows = slice(u * tm // n_sub, (u + 1) * tm // n_sub)
        xn = _rms_scale(x_ref[0, rows, :], g_ref[...]).astype(jnp.bfloat16)

        def proj(c0, n, xn=xn):
            return jnp.dot(xn, w_ref[:, c0:c0 + n], preferred_element_type=jnp.float32)

        z = proj(C_QA, 2 * D_A)
        qa_ref[0, rows, :] = (z[:, :D_A] * scale).astype(jnp.bfloat16)
        ka = z[:, D_A:]
        ka_ref[0, rows, :] = ka.astype(jnp.bfloat16)
        z = proj(C_VA, 2 * D_A)
        va = z[:, :D_A]
        va_ref[0, rows, :] = va.astype(jnp.bfloat16)
        ga_ref[0, rows, :] = _silu(z[:, D_A:]).astype(jnp.bfloat16)
        kept.append((rows, ka, va))

        cos64, sin64 = rope_ref[0, rows, :], rope_ref[1, rows, :]
        cos32, sin32 = rope_ref[2, rows, :], rope_ref[3, rows, :]
        cos_kw, sin_kw = rope_ref[4, rows, :], rope_ref[5, rows, :]
        z = proj(C_QB, 2 * D_B)
        for s in range(D_B // LANES):
            sl = slice(s * LANES, (s + 1) * LANES)
            qb = _rope_slab(z[:, sl], cos64, sin64, D_HEAD // 2)
            qb_ref[0, rows, sl] = (qb * scale).astype(jnp.bfloat16)
            kb = _rope_slab(z[:, D_B + s * LANES:D_B + (s + 1) * LANES], cos64, sin64, D_HEAD // 2)
            kb_ref[0, rows, sl] = kb
            kbb_ref[0, rows, sl] = kb.astype(jnp.bfloat16)
        z = proj(C_VB, 2 * D_B)
        vb_ref[0, rows, :] = z[:, :D_B]
        vbb_ref[0, rows, :] = z[:, :D_B].astype(jnp.bfloat16)
        gb_ref[0, rows, :] = _silu(z[:, D_B:]).astype(jnp.bfloat16)
        qm_ref[0, rows, :] = (proj(C_QM, D_M) * scale).astype(jnp.bfloat16)
        gm_ref[0, rows, :] = _silu(proj(C_GM, D_M)).astype(jnp.bfloat16)
        z = proj(C_QI, H_IDX * D_IDX)
        for s in range(H_IDX * D_IDX // LANES):
            sl = slice(s * LANES, (s + 1) * LANES)
            qi_ref[0, rows, sl] = _rope_slab(z[:, sl], cos32, sin32, D_IDX // 2).astype(jnp.bfloat16)
        z = proj(C_KIWI, 2 * LANES)
        kiwi = _rope_slab(z[:, :LANES], cos_kw, sin_kw, D_IDX // 2)
        kiwi_ref[0, rows, :] = kiwi
        ki_ref[0, rows, :] = kiwi[:, :D_IDX]
        ki4_ref[0, rows, :] = _rope_slab(z[:, LANES:], cos32, sin32, D_IDX // 2).astype(jnp.bfloat16)

    @pl.when(i >= n_tiles - keep_tiles)
    def _():
        for rows, ka, va in kept:
            ak_ref[0, rows, :] = ka
            av_ref[0, rows, :] = va


def _proj_call(x, g, w, rope, keep_rows):
    B, S, D = x.shape
    tm = min(PROJ_TM, S)
    n_tiles = S // tm
    keep_tiles = keep_rows // tm
    assert n_tiles * tm == S and keep_tiles * tm == keep_rows

    def tile(n, dtype):
        return (jax.ShapeDtypeStruct((B, S, n), dtype),
                pl.BlockSpec((1, tm, n), lambda b, i: (b, i, 0)))

    f32, bf16 = jnp.float32, jnp.bfloat16
    outs = [tile(D_A, bf16), tile(D_A, bf16), tile(D_A, bf16), tile(D_A, bf16),
            tile(D_B, bf16), tile(D_B, f32), tile(D_B, f32), tile(D_B, bf16),
            tile(D_M, bf16), tile(D_M, bf16), tile(H_IDX * D_IDX, bf16),
            tile(LANES, f32), tile(LANES, bf16), tile(D_IDX, f32), tile(D_B, bf16), tile(D_B, bf16)]
    keep_spec = pl.BlockSpec(
        (1, tm, D_A), lambda b, i: (b, jnp.maximum(i - (n_tiles - keep_tiles), 0), 0))
    outs += [(jax.ShapeDtypeStruct((B, keep_rows, D_A), f32), keep_spec)] * 2
    return pl.pallas_call(
        functools.partial(_proj_kernel, n_tiles=n_tiles, keep_tiles=keep_tiles),
        grid=(B, n_tiles),
        in_specs=[pl.BlockSpec((1, tm, D), lambda b, i: (b, i, 0)),
                  pl.BlockSpec((1, D), lambda b, i: (0, 0)),
                  pl.BlockSpec((D, W_COLS), lambda b, i: (0, 0)),
                  pl.BlockSpec((6, tm, LANES), lambda b, i: (0, i, 0))],
        out_specs=[o[1] for o in outs],
        out_shape=[o[0] for o in outs],
        compiler_params=_cparams(2),
        name="proj",
    )(x, g, w, rope)


def _memkv_kernel(m_ref, g_ref, w_ref, mk_ref, mv_ref):
    xn = _rms_scale(m_ref[...], g_ref[...]).astype(jnp.bfloat16)
    mk_ref[...] = jnp.dot(xn, w_ref[:, :D_M], preferred_element_type=jnp.float32)
    mv_ref[...] = jnp.dot(xn, w_ref[:, D_M:], preferred_element_type=jnp.float32)


def _memkv_call(mem2d, g, w):
    R, D = mem2d.shape
    tm = min(PROJ_TM, R)
    assert R % tm == 0
    row = lambda n: pl.BlockSpec((tm, n), lambda i: (i, 0))
    return pl.pallas_call(
        _memkv_kernel,
        grid=(R // tm,),
        in_specs=[row(D), pl.BlockSpec((1, D), lambda i: (0, 0)),
                  pl.BlockSpec((D, 2 * D_M), lambda i: (0, 0))],
        out_specs=[row(D_M), row(D_M)],
        out_shape=[jax.ShapeDtypeStruct((R, D_M), jnp.float32)] * 2,
        compiler_params=_cparams(1),
        name="memkv",
    )(mem2d, g, w)


def _softmax_rows_pv(s, v):
    m = jnp.max(s, axis=1, keepdims=True)
    p = jnp.exp2(s - m)
    l = jnp.sum(p, axis=1, keepdims=True)
    o = jnp.dot(p.astype(jnp.bfloat16), v, preferred_element_type=jnp.float32)
    return o * (1.0 / l)


def _keep_lanes(slab, lo, hi):
    x = slab.astype(jnp.float32)
    lane = lax.broadcasted_iota(jnp.int32, x.shape, 1)
    return jnp.where(lane >= lo, jnp.where(lane < hi, x, 0.0), 0.0).astype(jnp.bfloat16)


def _pair_attend(q_slab, k_slab, v_slab, bias_fn):
    lane = lax.broadcasted_iota(jnp.int32, q_slab.shape, 1)
    out = None
    for hh in range(2):
        s = lax.dot_general(_keep_lanes(q_slab, hh * D_HEAD, (hh + 1) * D_HEAD), k_slab, _NT,
                            preferred_element_type=jnp.float32)
        o = _softmax_rows_pv(bias_fn(s, hh), v_slab)
        out = o if hh == 0 else jnp.where(lane < D_HEAD, out, o)
    return out


def _mem_attend(qm_ref, gm_ref, mk_ref, mv_ref, om_ref):
    for p in range(D_M // LANES):
        sl = slice(p * LANES, (p + 1) * LANES)
        o = _pair_attend(qm_ref[0, :, sl], mk_ref[0, :, sl].astype(jnp.bfloat16),
                         mv_ref[0, :, sl].astype(jnp.bfloat16), lambda s, hh: s)
        om_ref[0, :, sl] = (o * gm_ref[0, :, sl].astype(jnp.float32)).astype(jnp.bfloat16)


def _build_band_bias(vec_ref, bias_ref, n_q, n_valid, chunked):
    n_keys = bias_ref.shape[2]
    i = lax.broadcasted_iota(jnp.int32, (n_q, n_keys), 0)
    r = lax.broadcasted_iota(jnp.int32, (n_q, n_keys), 1)
    lo = ((i >> 6) << 6) if chunked else jnp.zeros_like(i)
    for h in range(bias_ref.shape[0]):
        base = jnp.broadcast_to(vec_ref[h:h + 1, :], (n_q, BIAS_VEC))
        rolled = pltpu.roll(base, 0, 1, stride=1, stride_axis=0)
        b = rolled[:, :n_keys] * LOG2E
        bias_ref[h] = jnp.where(r >= lo, jnp.where(r < lo + n_valid, b, _NEG_INF), _NEG_INF)


def _build_band_bias_t(rvec_ref, bias_ref):
    n_keys, n_q = bias_ref.shape[1], bias_ref.shape[2]
    r = lax.broadcasted_iota(jnp.int32, (n_keys, n_q), 0)
    i = lax.broadcasted_iota(jnp.int32, (n_keys, n_q), 1)
    lo = (i >> 6) << 6
    for h in range(bias_ref.shape[0]):
        base = jnp.broadcast_to(rvec_ref[h:h + 1, :], (n_keys, BIAS_VEC))
        rolled = pltpu.roll(base, 0, 1, stride=1, stride_axis=0)
        b = rolled[:, :n_q] * LOG2E
        bias_ref[h] = jnp.where(r >= lo, jnp.where(r < lo + A_WINDOW + CHUNK, b, _NEG_INF), _NEG_INF)


def _band_prompt_kernel(qa_ref, ka_ref, va_ref, ga_ref, qm_ref, gm_ref, mk_ref, mv_ref, rvec_ref,
                        oa_ref, om_ref, kpad, vt_blk, mk_bf, mv_t, bias_ref):
    j = pl.program_id(1)
    n_front = A_WINDOW // BAND_TQ
    n_win = BAND_WIN // BAND_TQ

    @pl.when(jnp.logical_and(pl.program_id(0) == 0, j == 0))
    def _():
        _build_band_bias_t(rvec_ref, bias_ref)

    @pl.when(j == 0)
    def _():
        kpad[:A_WINDOW, :] = jnp.zeros((A_WINDOW, D_A), jnp.bfloat16)
        kpad[A_WINDOW:, :] = ka_ref[0]
        for t in range(vt_blk.shape[0]):
            if t < n_front:
                vt_blk[t] = jnp.zeros((D_A, BAND_TQ), jnp.bfloat16)
            else:
                rows = slice((t - n_front) * BAND_TQ, (t - n_front + 1) * BAND_TQ)
                vt_blk[t] = va_ref[0, rows, :].astype(jnp.float32).T.astype(jnp.bfloat16)
        mk_bf[...] = mk_ref[0].astype(jnp.bfloat16)
        mv_t[...] = mv_ref[0].T.astype(jnp.bfloat16)

    start = pl.multiple_of(j * BAND_TQ, BAND_TQ)

    def store_a(sl, o):
        oa_ref[0, :, sl] = o

    def store_m(sl, o):
        om_ref[0, :, sl] = o

    def run(mask_front):
        if mask_front:
            pos = lax.broadcasted_iota(jnp.int32, (BAND_WIN, BAND_TQ), 0) + j * BAND_TQ
            front = jnp.where(pos >= A_WINDOW, 0.0, _NEG_INF)
        _kq_attend(lambda h: bias_ref[h] + front if mask_front else bias_ref[h],
                   qa_ref[0], ga_ref[0],
                   lambda sl: kpad[pl.ds(start, BAND_WIN), sl],
                   lambda sl: jnp.concatenate([vt_blk[j + u, sl, :] for u in range(n_win)], axis=1),
                   store_a)
        _kq_attend(lambda h: None, qm_ref[0], gm_ref[0], lambda sl: mk_bf[:, sl], lambda sl: mv_t[sl, :],
                   store_m)

    pl.when(j < n_front)(lambda: run(True))
    pl.when(j >= n_front)(lambda: run(False))


def _band_prompt_call(qa, ka, va, ga, qm, gm, mk, mv, bias_rvec):
    B, S, _ = qa.shape
    nq = S // BAND_TQ
    n_mem = mk.shape[1]
    assert nq * BAND_TQ == S and A_WINDOW % BAND_TQ == 0
    qblk = lambda n: pl.BlockSpec((1, BAND_TQ, n), lambda b, j: (b, j, 0))
    full = lambda r, n: pl.BlockSpec((1, r, n), lambda b, j: (b, 0, 0))
    return pl.pallas_call(
        _band_prompt_kernel,
        grid=(B, nq),
        in_specs=[qblk(D_A), full(S, D_A), full(S, D_A), qblk(D_A), qblk(D_M), qblk(D_M),
                  full(n_mem, D_M), full(n_mem, D_M),
                  pl.BlockSpec((H_A, BIAS_VEC), lambda b, j: (0, 0))],
        out_specs=[qblk(D_A), qblk(D_M)],
        out_shape=[jax.ShapeDtypeStruct((B, S, D_A), jnp.bfloat16),
                   jax.ShapeDtypeStruct((B, S, D_M), jnp.bfloat16)],
        scratch_shapes=[pltpu.VMEM((S + A_WINDOW, D_A), jnp.bfloat16),
                        pltpu.VMEM((nq + A_WINDOW // BAND_TQ, D_A, BAND_TQ), jnp.bfloat16),
                        pltpu.VMEM((n_mem, D_M), jnp.bfloat16),
                        pltpu.VMEM((D_M, n_mem), jnp.bfloat16),
                        pltpu.VMEM((H_A, BAND_WIN, BAND_TQ), jnp.float32)],
        compiler_params=_cparams(2),
        name="band_prompt",
    )(qa, ka, va, ga, qm, gm, mk, mv, bias_rvec)


def _band_sample_kernel(qa_ref, kc_ref, vc_ref, kn_ref, vn_ref, ga_ref, qm_ref, gm_ref,
                        mk_ref, mv_ref, vec_ref, oa_ref, om_ref, kcat, vcat, bias_ref):
    P = kc_ref.shape[1]
    T = kn_ref.shape[1]
    pad = kcat.shape[0] - P - T

    @pl.when(pl.program_id(0) == 0)
    def _():
        _build_band_bias(vec_ref, bias_ref, T, P + T, False)

    kcat[:P, :] = kc_ref[0].astype(jnp.bfloat16)
    vcat[:P, :] = vc_ref[0].astype(jnp.bfloat16)
    kcat[P:P + T, :] = kn_ref[0]
    vcat[P:P + T, :] = vn_ref[0]
    zeros = jnp.zeros((pad, D_A), jnp.bfloat16)
    kcat[P + T:, :] = zeros
    vcat[P + T:, :] = zeros
    for p in range(D_A // LANES):
        sl = slice(p * LANES, (p + 1) * LANES)
        o = _pair_attend(qa_ref[0, :, sl], kcat[:, sl], vcat[:, sl],
                         lambda s, hh, p=p: s + bias_ref[2 * p + hh])
        oa_ref[0, :, sl] = (o * ga_ref[0, :, sl].astype(jnp.float32)).astype(jnp.bfloat16)
    _mem_attend(qm_ref, gm_ref, mk_ref, mv_ref, om_ref)


def _band_sample_call(qa, kc, vc, kn, vn, ga, qm, gm, mk, mv, bias_vec):
    B, T, _ = qa.shape
    P = kc.shape[1]
    n_keys = -(-(P + T) // LANES) * LANES
    n_mem = mk.shape[1]
    blk = lambda r, n: pl.BlockSpec((1, r, n), lambda b: (b, 0, 0))
    return pl.pallas_call(
        _band_sample_kernel,
        grid=(B,),
        in_specs=[blk(T, D_A), blk(P, D_A), blk(P, D_A), blk(T, D_A), blk(T, D_A), blk(T, D_A),
                  blk(T, D_M), blk(T, D_M), blk(n_mem, D_M), blk(n_mem, D_M),
                  pl.BlockSpec((H_A, BIAS_VEC), lambda b: (0, 0))],
        out_specs=[blk(T, D_A), blk(T, D_M)],
        out_shape=[jax.ShapeDtypeStruct((B, T, D_A), jnp.bfloat16),
                   jax.ShapeDtypeStruct((B, T, D_M), jnp.bfloat16)],
        scratch_shapes=[pltpu.VMEM((n_keys, D_A), jnp.bfloat16)] * 2
        + [pltpu.VMEM((H_A, T, n_keys), jnp.float32)],
        compiler_params=_cparams(1),
        name="band_sample",
    )(qa, kc, vc, kn, vn, ga, qm, gm, mk, mv, bias_vec)


def _key_to_f32(k):
    bits = k ^ ((k >> 31) & jnp.int32(0x7FFFFFFF))
    return lax.bitcast_convert_type(bits, jnp.float32)


def _count(pred_f32):
    ones = jnp.ones((8, pred_f32.shape[0]), jnp.bfloat16)
    c = jnp.dot(ones, pred_f32.astype(jnp.bfloat16), preferred_element_type=jnp.float32)
    return c[0:1, :]


def _dsa_select(sc_ref, probs, tq, topk):
    one, zero = jnp.float32(1.0), jnp.float32(0.0)
    live = [(off, tk) for off, tk in probs if tk > topk]
    n_live = len(live)

    def sc_of(n):
        off, tk = live[n]
        return sc_ref[off:off + tk, :]

    if live:
        def thr_step(i, ts):
            bit = lax.shift_left(jnp.int32(1), 31 - i)
            out = []
            for n in range(n_live):
                cand = ts[n] + bit
                thr = _key_to_f32(cand)
                off, tk = live[n]
                n_vec = (tk * 7 // 32) // 64 * 64
                n_mxu = tk - n_vec
                c = _count(jnp.where(sc_ref[off:off + n_mxu, :] >= thr, one, zero))
                if n_vec:
                    c = c + jnp.sum(jnp.where(sc_ref[off + n_mxu:off + tk, :] >= thr, one, zero),
                                    axis=0, keepdims=True)
                out.append(jnp.where(c >= topk, cand, ts[n]))
            return tuple(out)

        start = tuple(jnp.full((1, tq), _INT_MIN, jnp.int32) for _ in range(n_live))
        t_keys = lax.fori_loop(0, 32, thr_step, start)
        thrs = [jnp.where(t == _INT_MIN, _NEG_INF, _key_to_f32(t)) for t in t_keys]

        for n in range(n_live):
            off, tk = live[n]
            c_ge = _count(jnp.where(sc_of(n) >= thrs[n], one, zero))

            def tie_search(n=n, off=off, tk=tk):
                need = topk - _count(jnp.where(sc_of(n) > thrs[n], one, zero))
                tile = 2 * LANES

                def step(first, size, state):
                    before, jmax = state
                    r = lax.broadcasted_iota(jnp.int32, (size, size), 0)
                    c = lax.broadcasted_iota(jnp.int32, (size, size), 1)
                    lower = jnp.where(c <= r, one, zero).astype(jnp.bfloat16)
                    tied = jnp.where(sc_ref[pl.ds(off + first, size), :] == thrs[n], one, zero)
                    rank = before + jnp.dot(lower, tied.astype(jnp.bfloat16),
                                            preferred_element_type=jnp.float32)
                    kidx = (lax.broadcasted_iota(jnp.int32, (size, tq), 0) + first).astype(jnp.float32)
                    taken = jnp.where(tied > 0, jnp.where(rank <= need, kidx, -one), -one)
                    jmax = jnp.maximum(jmax, jnp.max(taken, axis=0, keepdims=True))
                    return rank[size - 1:size, :], jmax

                state = (jnp.zeros((1, tq), jnp.float32), jnp.full((1, tq), -1.0, jnp.float32))
                n_full, rest = divmod(tk, tile)
                state = lax.fori_loop(
                    0, n_full, lambda t, st: step(pl.multiple_of(t * tile, tile), tile, st), state)
                if rest:
                    state = step(n_full * tile, rest, state)
                return state[1].astype(jnp.int32)

            def keep_with_ties(n=n, off=off, tk=tk, tie_search=tie_search):
                jmax = tie_search()
                sc = sc_of(n)
                kidx = lax.broadcasted_iota(jnp.int32, (tk, tq), 0)
                fin = jnp.where(jnp.abs(sc) < jnp.float32(jnp.inf), zero, _NEG_INF)
                sc_ref[off:off + tk, :] = jnp.where(
                    sc > thrs[n], fin,
                    jnp.where(sc == thrs[n], jnp.where(kidx <= jmax, fin, _NEG_INF), _NEG_INF))

            def keep_plain(n=n, off=off, tk=tk):
                sc = sc_of(n)
                fin = jnp.where(jnp.abs(sc) < jnp.float32(jnp.inf), zero, _NEG_INF)
                sc_ref[off:off + tk, :] = jnp.where(sc >= thrs[n], fin, _NEG_INF)

            has_excess = jnp.max(jnp.where(c_ge > topk, one, zero)) > 0
            lax.cond(has_excess, keep_with_ties, keep_plain)

    for off, tk in probs:
        if tk <= topk:
            sc = sc_ref[off:off + tk, :]
            sc_ref[off:off + tk, :] = jnp.where(jnp.abs(sc) < jnp.float32(jnp.inf), zero, _NEG_INF)


def _dsa_scores(limit, qi, kiwi, ki4):
    tk, tq = ki4.shape[0], qi.shape[0]
    kiwi_t = kiwi.T
    acc = jnp.zeros((tk, tq), jnp.float32)
    heads_per_slab = LANES // D_IDX

    def head_q(h):
        slab = qi[:, (h // heads_per_slab) * LANES:(h // heads_per_slab + 1) * LANES]
        lo = (h % heads_per_slab) * D_IDX
        return _keep_lanes(slab, lo, lo + D_IDX)

    for h in range(0, H_IDX, 2):
        d = lax.dot_general(ki4, jnp.concatenate([head_q(h), head_q(h + 1)], axis=0), _NT,
                            preferred_element_type=jnp.float32)
        for e in range(2):
            w = kiwi_t[D_IDX + h + e:D_IDX + h + e + 1, :] * ((D_IDX ** -0.5) * (H_IDX ** -0.5))
            acc = acc + w * jnp.maximum(d[:, e * tq:(e + 1) * tq], 0.0)
    if limit is not None:
        kidx = lax.broadcasted_iota(jnp.int32, (tk, tq), 0)
        acc = jnp.where(kidx < limit, acc, _NEG_INF)
    return acc


def _kq_attend(bias_of, q, g, k_slab_of, vt_slab_of, store):
    tq = q.shape[0]
    for p in range(q.shape[1] // LANES):
        sl = slice(p * LANES, (p + 1) * LANES)
        q_slab = q[:, sl]
        q_pair = jnp.concatenate([_keep_lanes(q_slab, 0, D_HEAD), _keep_lanes(q_slab, D_HEAD, LANES)],
                                 axis=0)
        s_pair = lax.dot_general(k_slab_of(sl), q_pair, _NT, preferred_element_type=jnp.float32)
        probs, inv_l = [], []
        for hh in range(2):
            s = s_pair[:, hh * tq:(hh + 1) * tq]
            bias = bias_of(2 * p + hh)
            if bias is not None:
                s = s + bias
            m = jnp.max(s, axis=0, keepdims=True)
            pr = jnp.exp2(s - m)
            inv_l.append(1.0 / jnp.sum(pr, axis=0, keepdims=True))
            probs.append(pr.astype(jnp.bfloat16))
        o_pair = jnp.dot(vt_slab_of(sl), jnp.concatenate(probs, axis=1),
                         preferred_element_type=jnp.float32)
        ot = jnp.concatenate([o_pair[:D_HEAD, :tq] * inv_l[0], o_pair[D_HEAD:, tq:] * inv_l[1]], axis=0)
        store(sl, (ot.T * g[:, sl].astype(jnp.float32)).astype(jnp.bfloat16))


def _dsa_blocks(S, tq):
    return [(tq * j * (j + 1) // 2, (j + 1) * tq) for j in range(S // tq)]


def _own_region():
    return pl.when(pl.program_id(0) >= 0)


def _dsa_select_kernel(qi_ref, kiwi_ref, ki4_ref, keep_ref, sc_ref, *, topk, tq):
    own_region = _own_region()
    probs = _dsa_blocks(ki4_ref.shape[1], tq)
    for j, (off, tk) in enumerate(probs):
        rows = slice(j * tq, (j + 1) * tq)

        @own_region
        def _(j=j, tk=tk, off=off, rows=rows):
            qpos = j * tq + lax.broadcasted_iota(jnp.int32, (1, tq), 1)
            limit = ((qpos >> 6) + 1) << 6
            sc_ref[off:off + tk, :] = _dsa_scores(limit, qi_ref[0, rows, :], kiwi_ref[0, rows, :],
                                                  ki4_ref[0, :tk, :])

    _dsa_select(sc_ref, probs, tq, topk)
    keep_ref[0] = sc_ref[...].astype(jnp.bfloat16)


def _dsa_attend_kernel(q_ref, g_ref, k_ref, v_ref, keep_ref, o_ref, vt, *, tq):
    own_region = _own_region()
    S = k_ref.shape[1]
    for t in range(S // tq):
        vt[t] = v_ref[0, t * tq:(t + 1) * tq, :].astype(jnp.float32).T.astype(jnp.bfloat16)
    for j, (off, tk) in enumerate(_dsa_blocks(S, tq)):
        rows = slice(j * tq, (j + 1) * tq)

        @own_region
        def _(j=j, tk=tk, off=off, rows=rows):
            def store(sl, o):
                o_ref[0, rows, sl] = o

            bias = keep_ref[0, off:off + tk, :].astype(jnp.float32)
            _kq_attend(lambda h: bias, q_ref[0, rows, :], g_ref[0, rows, :],
                       lambda sl: k_ref[0, :tk, sl],
                       lambda sl: jnp.concatenate([vt[t, sl, :] for t in range(j + 1)], axis=1), store)


def _dsa_prompt_call(qb, qi, kiwi, gb, kb, vb, ki4):
    B, S, _ = qb.shape
    tq = DSA_TQ
    nq = S // tq
    assert CHUNK == 64 and nq * tq == S
    topk = min(TOPK_MAX, S // 4)
    n_rows = tq * nq * (nq + 1) // 2
    full = lambda n: pl.BlockSpec((1, S, n), lambda b: (b, 0, 0))
    keep_spec = pl.BlockSpec((1, n_rows, tq), lambda b: (b, 0, 0))
    keep = pl.pallas_call(
        functools.partial(_dsa_select_kernel, topk=topk, tq=tq),
        grid=(B,),
        in_specs=[full(H_IDX * D_IDX), full(LANES), full(LANES)],
        out_specs=keep_spec,
        out_shape=jax.ShapeDtypeStruct((B, n_rows, tq), jnp.bfloat16),
        scratch_shapes=[pltpu.VMEM((n_rows, tq), jnp.float32)],
        compiler_params=_cparams(1),
        name="dsa_select",
    )(qi, kiwi, ki4)
    return pl.pallas_call(
        functools.partial(_dsa_attend_kernel, tq=tq),
        grid=(B,),
        in_specs=[full(D_B), full(D_B), full(D_B), full(D_B), keep_spec],
        out_specs=full(D_B),
        out_shape=jax.ShapeDtypeStruct((B, S, D_B), jnp.bfloat16),
        scratch_shapes=[pltpu.VMEM((nq, D_B, tq), jnp.bfloat16)],
        compiler_params=_cparams(1),
        name="dsa_attend",
    )(qb, gb, kb, vb, keep)


def _dsa_sample_kernel(q_ref, qi_ref, kiwi_ref, g_ref, kc_ref, vc_ref, kic_ref,
                       kn_ref, vn_ref, kin_ref, o_ref, kbf, vt, ki4s, sc_ref, *, topk):
    P = kc_ref.shape[1]
    T = kn_ref.shape[1]
    tq = q_ref.shape[1]
    kbf[:P, :] = kc_ref[0].astype(jnp.bfloat16)
    kbf[P:, :] = kn_ref[0].astype(jnp.bfloat16)
    vt[:, :P] = vc_ref[0].T.astype(jnp.bfloat16)
    vn_tile = jnp.concatenate([vn_ref[0], jnp.zeros((LANES - T, D_B), jnp.float32)], axis=0)
    vt[:, P:] = vn_tile.T[:, :T].astype(jnp.bfloat16)
    ki4s[:P, :] = kic_ref[0]
    ki4s[P:, :] = kin_ref[0]
    sc_ref[...] = _dsa_scores(None, qi_ref[0], kiwi_ref[0], ki4s[...])
    _dsa_select(sc_ref, [(0, P + T)], tq, topk)

    def store(sl, o):
        o_ref[0, :, sl] = o

    _kq_attend(lambda h: sc_ref[...], q_ref[0], g_ref[0], lambda sl: kbf[:, sl], lambda sl: vt[sl, :],
               store)


def _dsa_sample_call(qb, qi, kiwi, gb, kc, vc, kic, kn, vn, kin):
    B, tq, _ = qb.shape
    P, T = kc.shape[1], kn.shape[1]
    tk = P + T
    topk = min(TOPK_MAX, tk // 4)
    blk = lambda r, n: pl.BlockSpec((1, r, n), lambda b: (b, 0, 0))
    return pl.pallas_call(
        functools.partial(_dsa_sample_kernel, topk=topk),
        grid=(B,),
        in_specs=[blk(tq, D_B), blk(tq, H_IDX * D_IDX), blk(tq, LANES), blk(tq, D_B),
                  blk(P, D_B), blk(P, D_B), blk(P, LANES),
                  blk(T, D_B), blk(T, D_B), blk(T, LANES)],
        out_specs=blk(tq, D_B),
        out_shape=jax.ShapeDtypeStruct((B, tq, D_B), jnp.bfloat16),
        scratch_shapes=[pltpu.VMEM((tk, D_B), jnp.bfloat16),
                        pltpu.VMEM((D_B, tk), jnp.bfloat16),
                        pltpu.VMEM((tk, LANES), jnp.bfloat16),
                        pltpu.VMEM((tk, tq), jnp.float32)],
        compiler_params=_cparams(1),
        name="dsa_sample",
    )(qb, qi, kiwi, gb, kc, vc, kic, kn, vn, kin)


def _merge_kernel(x_ref, oa_ref, ob_ref, om_ref, w_ref, g_ref, y_ref, *, last):
    acc = x_ref[...]
    acc = acc + jnp.dot(oa_ref[...], w_ref[:D_A, :], preferred_element_type=jnp.float32)
    acc = acc + jnp.dot(ob_ref[...], w_ref[D_A:D_A + D_B, :], preferred_element_type=jnp.float32)
    acc = acc + jnp.dot(om_ref[...], w_ref[D_A + D_B:, :], preferred_element_type=jnp.float32)
    y_ref[...] = _rms_scale(acc, g_ref[...]) if last else acc


def _merge_call(x2d, oa, ob, om, w, g, last):
    R, D = x2d.shape
    tm = min(MERGE_TM, R)
    assert R % tm == 0
    row = lambda n: pl.BlockSpec((tm, n), lambda i: (i, 0))
    return pl.pallas_call(
        functools.partial(_merge_kernel, last=last),
        grid=(R // tm,),
        in_specs=[row(D), row(D_A), row(D_B), row(D_M),
                  pl.BlockSpec((D_A + D_B + D_M, D), lambda i: (0, 0)),
                  pl.BlockSpec((1, D), lambda i: (0, 0))],
        out_specs=row(D),
        out_shape=jax.ShapeDtypeStruct((R, D), jnp.float32),
        compiler_params=_cparams(1),
        name="merge",
    )(x2d, oa, ob, om, w, g)


def _pack_w_in(w):
    D = w.shape[0]
    ki = w[:, C_KIWI:C_KIWI + D_IDX]
    wi = w[:, C_KIWI + D_IDX:C_KIWI + D_IDX + H_IDX]
    pad = jnp.zeros((D, LANES - D_IDX - H_IDX), w.dtype)
    packed = jnp.concatenate([w[:, :C_KIWI], ki, wi, pad, jnp.tile(ki, (1, LANES // D_IDX))], axis=1)
    return packed.astype(jnp.bfloat16)


def _rope_tables(pos):
    posf = pos.astype(jnp.float32)

    def tables(d):
        half = d // 2
        inv_freq = ROPE_THETA ** (-jnp.arange(half, dtype=jnp.float32) * 2.0 / d)
        ang = posf[:, None] * inv_freq[None, :]
        cos, sin = jnp.cos(ang), jnp.sin(ang)
        return jnp.concatenate([cos, cos], axis=1), jnp.concatenate([-sin, sin], axis=1)

    cos64, sin64 = tables(D_HEAD)
    cos32, sin32 = tables(D_IDX)
    n = pos.shape[0]
    rest = LANES - D_IDX
    cos_kw = jnp.concatenate([cos32, jnp.ones((n, rest), jnp.float32)], axis=1)
    sin_kw = jnp.concatenate([sin32, jnp.zeros((n, rest), jnp.float32)], axis=1)
    rep = lambda t, d: jnp.tile(t, (1, LANES // d))
    return jnp.stack([rep(cos64, D_HEAD), rep(sin64, D_HEAD), rep(cos32, D_IDX), rep(sin32, D_IDX),
                      cos_kw, sin_kw])


def _band_bias_vec(table):
    n_far = A_WINDOW - REL_CLIP + 1
    lo_idx = A_WINDOW + REL_CLIP - (BAND_WIN - 1)
    assert lo_idx >= 0 and BIAS_VEC >= BAND_WIN + BAND_TQ - 1
    far = table[:, 2 * REL_CLIP:2 * REL_CLIP + 1].astype(jnp.float32)
    mid = table[:, lo_idx:2 * REL_CLIP][:, ::-1].astype(jnp.float32)
    vec = jnp.concatenate([jnp.tile(far, (1, n_far)), mid, jnp.tile(far, (1, BIAS_VEC - BAND_WIN))], axis=1)
    rvec = jnp.concatenate([vec[:, :1], vec[:, 1:][:, ::-1]], axis=1)
    return vec, rvec


def kernel(x_prompt, x_sample, mem_prompt, cache_a_k, cache_a_v, cache_b_k, cache_b_v, cache_b_kidx,
           cache_mem_k, cache_mem_v, norm_mix_g, w_in, rel_bias_a, norm_mem_g, w_mem_kv, w_out,
           norm_final_g):
    B, S, D = x_prompt.shape
    Bs, T, _ = x_sample.shape
    depth = w_in.shape[0]
    P = cache_b_k.shape[2]
    Pa = cache_a_k.shape[2]
    n_mem = mem_prompt.shape[1]
    keep = min(A_WINDOW, S)
    assert Pa == A_WINDOW and T <= CHUNK and LANES % T == 0

    rope_p = _rope_tables(jnp.arange(S))
    rope_s = jnp.tile(_rope_tables(P + jnp.arange(T)), (1, Bs, 1))
    g_final = norm_final_g.reshape(1, D)

    xp, xs = x_prompt, x_sample
    outs_p = [[] for _ in range(7)]
    outs_s = [[] for _ in range(5)]
    for l in range(depth):
        last = l == depth - 1
        w = _pack_w_in(w_in[l])
        g_mix = norm_mix_g[l].reshape(1, D)
        w_o = w_out[l].astype(jnp.bfloat16)
        bias_vec, bias_rvec = _band_bias_vec(rel_bias_a[l])

        (qa, ka, va, ga, qb, kb, vb, gb, qm, gm, qi, kiwi, ki4, ki, kbb, vbb, ak, av) = _proj_call(
            xp, g_mix, w, rope_p, keep)
        mk, mv = _memkv_call(mem_prompt.reshape(B * n_mem, D), norm_mem_g[l].reshape(1, D),
                             w_mem_kv[l].astype(jnp.bfloat16))
        mk = mk.reshape(B, n_mem, D_M)
        mv = mv.reshape(B, n_mem, D_M)
        oa, om = _band_prompt_call(qa, ka, va, ga, qm, gm, mk, mv, bias_rvec)
        ob = _dsa_prompt_call(qb, qi, kiwi, gb, kbb, vbb, ki4)
        xp = _merge_call(xp.reshape(B * S, D), oa.reshape(B * S, D_A), ob.reshape(B * S, D_B),
                         om.reshape(B * S, D_M), w_o, g_final, last).reshape(B, S, D)
        for lst, t in zip(outs_p, (ak.reshape(B, keep, H_A, D_HEAD), av.reshape(B, keep, H_A, D_HEAD),
                                   kb.reshape(B, S, H_B, D_HEAD), vb.reshape(B, S, H_B, D_HEAD), ki,
                                   mk.reshape(B, n_mem, H_M, D_HEAD), mv.reshape(B, n_mem, H_M, D_HEAD))):
            lst.append(t)

        (qa, ka, va, ga, qb, kb, vb, gb, qm, gm, qi, kiwi, ki4, ki, kbb, vbb, ak, av) = _proj_call(
            xs.reshape(1, Bs * T, D), g_mix, w, rope_s, Bs * T)
        per_b = lambda t: t.reshape(Bs, T, t.shape[-1])
        oa, om = _band_sample_call(
            per_b(qa), cache_a_k[l].reshape(Bs, Pa, D_A), cache_a_v[l].reshape(Bs, Pa, D_A),
            per_b(ka), per_b(va), per_b(ga), per_b(qm), per_b(gm),
            cache_mem_k[l].reshape(Bs, n_mem, D_M), cache_mem_v[l].reshape(Bs, n_mem, D_M), bias_vec)
        rep = lambda t: jnp.tile(per_b(t), (1, LANES // T, 1))
        kic = jnp.tile(cache_b_kidx[l], (1, 1, LANES // D_IDX)).astype(jnp.bfloat16)
        ob = _dsa_sample_call(rep(qb), rep(qi), rep(kiwi), rep(gb),
                              cache_b_k[l].reshape(Bs, P, D_B), cache_b_v[l].reshape(Bs, P, D_B), kic,
                              per_b(kb), per_b(vb), per_b(ki4))[:, :T]
        xs = _merge_call(xs.reshape(Bs * T, D), oa.reshape(Bs * T, D_A), ob.reshape(Bs * T, D_B),
                         om.reshape(Bs * T, D_M), w_o, g_final, last).reshape(Bs, T, D)
        for lst, t in zip(outs_s, (ak.reshape(Bs, T, H_A, D_HEAD), av.reshape(Bs, T, H_A, D_HEAD),
                                   kb.reshape(Bs, T, H_B, D_HEAD), vb.reshape(Bs, T, H_B, D_HEAD),
                                   ki.reshape(Bs, T, D_IDX))):
            lst.append(t)

    st = lambda ts: jnp.stack(ts, axis=0)
    return (xp, xs) + tuple(st(t) for t in outs_p) + tuple(st(t) for t in outs_s)
```

```python
import functools
import math

import jax
import jax.numpy as jnp
from jax import lax
from jax.experimental import pallas as pl
from jax.experimental.pallas import tpu as pltpu

CHUNK = 64
D_HEAD = 64
H_A = 6
H_B = 6
H_M = 4
D_A = H_A * D_HEAD
D_B = H_B * D_HEAD
D_M = H_M * D_HEAD
A_LEFT_CHUNKS = 8
A_WINDOW = A_LEFT_CHUNKS * CHUNK
REL_CLIP = 256
H_IDX = 8
D_IDX = 32
TOPK_MAX = 256
ROPE_THETA = 10000.0
EPS = 1e-6

LANES = 128
VMEM_LIMIT_BYTES = 56 * 1024 * 1024

C_QA, C_KA, C_VA, C_GA = 0, D_A, 2 * D_A, 3 * D_A
C_QB = 4 * D_A
C_KB, C_VB, C_GB = C_QB + D_B, C_QB + 2 * D_B, C_QB + 3 * D_B
C_QM = C_QB + 4 * D_B
C_GM = C_QM + D_M
C_QI = C_GM + D_M
C_KIWI = C_QI + H_IDX * D_IDX
C_KI4 = C_KIWI + LANES
W_COLS = C_KI4 + LANES

PROJ_TM = 512
MERGE_TM = 1024
BAND_TQ = 256
BAND_WIN = A_WINDOW + BAND_TQ
DSA_TQ = 256

LOG2E = math.log2(math.e)
QK_SCALE = (D_HEAD ** -0.5) * LOG2E
BIAS_VEC = 8 * LANES

_NT = (((1,), (1,)), ((), ()))
_INT_MIN = -2 ** 31
_NEG_INF = float("-inf")


def _cparams(n_axes):
    return pltpu.CompilerParams(
        dimension_semantics=("arbitrary",) * n_axes,
        vmem_limit_bytes=VMEM_LIMIT_BYTES)


def _silu(g):
    return g * (1.0 / (1.0 + jnp.exp(-g)))


def _rms_scale(x, g):
    ms = jnp.mean(x * x, axis=-1, keepdims=True)
    return (x * lax.rsqrt(ms + EPS)) * g


def _rope_slab(z, cos, sin, half):
    lane = lax.broadcasted_iota(jnp.int32, z.shape, 1)
    first = (lane & (2 * half - 1)) < half
    partner = jnp.where(first, pltpu.roll(z, LANES - half, 1), pltpu.roll(z, half, 1))
    return z * cos + partner * sin


def _proj_kernel(x_ref, g_ref, w_ref, rope_ref,
                 qa_ref, ka_ref, va_ref, ga_ref, qb_ref, kb_ref, vb_ref, gb_ref,
                 qm_ref, gm_ref, qi_ref, kiwi_ref, ki4_ref, ki_ref, kbb_ref, vbb_ref, ak_ref, av_ref,
                 *, n_tiles, keep_tiles):
    i = pl.program_id(1)
    scale = QK_SCALE
    tm = x_ref.shape[1]
    n_sub = 2 if tm % 32 == 0 else 1
    kept = []
    for u in range(n_sub):
        rows = slice(u * tm // n_sub, (u + 1) * tm // n_sub)
        xn = _rms_scale(x_ref[0, rows, :], g_ref[...]).astype(jnp.bfloat16)

        def proj(c0, n, xn=xn):
            return jnp.dot(xn, w_ref[:, c0:c0 + n], preferred_element_type=jnp.float32)

        z = proj(C_QA, 2 * D_A)
        qa_ref[0, rows, :] = (z[:, :D_A] * scale).astype(jnp.bfloat16)
        ka = z[:, D_A:]
        ka_ref[0, rows, :] = ka.astype(jnp.bfloat16)
        z = proj(C_VA, 2 * D_A)
        va = z[:, :D_A]
        va_ref[0, rows, :] = va.astype(jnp.bfloat16)
        ga_ref[0, rows, :] = _silu(z[:, D_A:]).astype(jnp.bfloat16)
        kept.append((rows, ka, va))

        cos64, sin64 = rope_ref[0, rows, :], rope_ref[1, rows, :]
        cos32, sin32 = rope_ref[2, rows, :], rope_ref[3, rows, :]
        cos_kw, sin_kw = rope_ref[4, rows, :], rope_ref[5, rows, :]
        z = proj(C_QB, 2 * D_B)
        for s in range(D_B // LANES):
            sl = slice(s * LANES, (s + 1) * LANES)
            qb = _rope_slab(z[:, sl], cos64, sin64, D_HEAD // 2)
            qb_ref[0, rows, sl] = (qb * scale).astype(jnp.bfloat16)
            kb = _rope_slab(z[:, D_B + s * LANES:D_B + (s + 1) * LANES], cos64, sin64, D_HEAD // 2)
            kb_ref[0, rows, sl] = kb
            kbb_ref[0, rows, sl] = kb.astype(jnp.bfloat16)
        z = proj(C_VB, 2 * D_B)
        vb_ref[0, rows, :] = z[:, :D_B]
        vbb_ref[0, rows, :] = z[:, :D_B].astype(jnp.bfloat16)
        gb_ref[0, rows, :] = _silu(z[:, D_B:]).astype(jnp.bfloat16)
        qm_ref[0, rows, :] = (proj(C_QM, D_M) * scale).astype(jnp.bfloat16)
        gm_ref[0, rows, :] = _silu(proj(C_GM, D_M)).astype(jnp.bfloat16)
        z = proj(C_QI, H_IDX * D_IDX)
        for s in range(H_IDX * D_IDX // LANES):
            sl = slice(s * LANES, (s + 1) * LANES)
            qi_ref[0, rows, sl] = _rope_slab(z[:, sl], cos32, sin32, D_IDX // 2).astype(jnp.bfloat16)
        z = proj(C_KIWI, 2 * LANES)
        kiwi = _rope_slab(z[:, :LANES], cos_kw, sin_kw, D_IDX // 2)
        kiwi_ref[0, rows, :] = kiwi
        ki_ref[0, rows, :] = kiwi[:, :D_IDX]
        ki4_ref[0, rows, :] = _rope_slab(z[:, LANES:], cos32, sin32, D_IDX // 2).astype(jnp.bfloat16)

    @pl.when(i >= n_tiles - keep_tiles)
    def _():
        for rows, ka, va in kept:
            ak_ref[0, rows, :] = ka
            av_ref[0, rows, :] = va


def _proj_call(x, g, w, rope, keep_rows):
    B, S, D = x.shape
    tm = min(PROJ_TM, S)
    n_tiles = S // tm
    keep_tiles = keep_rows // tm
    assert n_tiles * tm == S and keep_tiles * tm == keep_rows

    def tile(n, dtype):
        return (jax.ShapeDtypeStruct((B, S, n), dtype),
                pl.BlockSpec((1, tm, n), lambda b, i: (b, i, 0)))

    f32, bf16 = jnp.float32, jnp.bfloat16
    outs = [tile(D_A, bf16), tile(D_A, bf16), tile(D_A, bf16), tile(D_A, bf16),
            tile(D_B, bf16), tile(D_B, f32), tile(D_B, f32), tile(D_B, bf16),
            tile(D_M, bf16), tile(D_M, bf16), tile(H_IDX * D_IDX, bf16),
            tile(LANES, f32), tile(LANES, bf16), tile(D_IDX, f32), tile(D_B, bf16), tile(D_B, bf16)]
    keep_spec = pl.BlockSpec(
        (1, tm, D_A), lambda b, i: (b, jnp.maximum(i - (n_tiles - keep_tiles), 0), 0))
    outs += [(jax.ShapeDtypeStruct((B, keep_rows, D_A), f32), keep_spec)] * 2
    return pl.pallas_call(
        functools.partial(_proj_kernel, n_tiles=n_tiles, keep_tiles=keep_tiles),
        grid=(B, n_tiles),
        in_specs=[pl.BlockSpec((1, tm, D), lambda b, i: (b, i, 0)),
                  pl.BlockSpec((1, D), lambda b, i: (0, 0)),
                  pl.BlockSpec((D, W_COLS), lambda b, i: (0, 0)),
                  pl.BlockSpec((6, tm, LANES), lambda b, i: (0, i, 0))],
        out_specs=[o[1] for o in outs],
        out_shape=[o[0] for o in outs],
        compiler_params=_cparams(2),
        name="proj",
    )(x, g, w, rope)


def _memkv_kernel(m_ref, g_ref, w_ref, mk_ref, mv_ref):
    xn = _rms_scale(m_ref[...], g_ref[...]).astype(jnp.bfloat16)
    mk_ref[...] = jnp.dot(xn, w_ref[:, :D_M], preferred_element_type=jnp.float32)
    mv_ref[...] = jnp.dot(xn, w_ref[:, D_M:], preferred_element_type=jnp.float32)


def _memkv_call(mem2d, g, w):
    R, D = mem2d.shape
    tm = min(PROJ_TM, R)
    assert R % tm == 0
    row = lambda n: pl.BlockSpec((tm, n), lambda i: (i, 0))
    return pl.pallas_call(
        _memkv_kernel,
        grid=(R // tm,),
        in_specs=[row(D), pl.BlockSpec((1, D), lambda i: (0, 0)),
                  pl.BlockSpec((D, 2 * D_M), lambda i: (0, 0))],
        out_specs=[row(D_M), row(D_M)],
        out_shape=[jax.ShapeDtypeStruct((R, D_M), jnp.float32)] * 2,
        compiler_params=_cparams(1),
        name="memkv",
    )(mem2d, g, w)


def _softmax_rows_pv(s, v):
    m = jnp.max(s, axis=1, keepdims=True)
    p = jnp.exp2(s - m)
    l = jnp.sum(p, axis=1, keepdims=True)
    o = jnp.dot(p.astype(jnp.bfloat16), v, preferred_element_type=jnp.float32)
    return o * (1.0 / l)


def _keep_lanes(slab, lo, hi):
    x = slab.astype(jnp.float32)
    lane = lax.broadcasted_iota(jnp.int32, x.shape, 1)
    return jnp.where(lane >= lo, jnp.where(lane < hi, x, 0.0), 0.0).astype(jnp.bfloat16)


def _pair_attend(q_slab, k_slab, v_slab, bias_fn):
    lane = lax.broadcasted_iota(jnp.int32, q_slab.shape, 1)
    out = None
    for hh in range(2):
        s = lax.dot_general(_keep_lanes(q_slab, hh * D_HEAD, (hh + 1) * D_HEAD), k_slab, _NT,
                            preferred_element_type=jnp.float32)
        o = _softmax_rows_pv(bias_fn(s, hh), v_slab)
        out = o if hh == 0 else jnp.where(lane < D_HEAD, out, o)
    return out


def _mem_attend(qm_ref, gm_ref, mk_ref, mv_ref, om_ref):
    for p in range(D_M // LANES):
        sl = slice(p * LANES, (p + 1) * LANES)
        o = _pair_attend(qm_ref[0, :, sl], mk_ref[0, :, sl].astype(jnp.bfloat16),
                         mv_ref[0, :, sl].astype(jnp.bfloat16), lambda s, hh: s)
        om_ref[0, :, sl] = (o * gm_ref[0, :, sl].astype(jnp.float32)).astype(jnp.bfloat16)


def _build_band_bias(vec_ref, bias_ref, n_q, n_valid, chunked):
    n_keys = bias_ref.shape[2]
    i = lax.broadcasted_iota(jnp.int32, (n_q, n_keys), 0)
    r = lax.broadcasted_iota(jnp.int32, (n_q, n_keys), 1)
    lo = ((i >> 6) << 6) if chunked else jnp.zeros_like(i)
    for h in range(bias_ref.shape[0]):
        base = jnp.broadcast_to(vec_ref[h:h + 1, :], (n_q, BIAS_VEC))
        rolled = pltpu.roll(base, 0, 1, stride=1, stride_axis=0)
        b = rolled[:, :n_keys] * LOG2E
        bias_ref[h] = jnp.where(r >= lo, jnp.where(r < lo + n_valid, b, _NEG_INF), _NEG_INF)


def _build_band_bias_t(rvec_ref, bias_ref):
    n_keys, n_q = bias_ref.shape[1], bias_ref.shape[2]
    r = lax.broadcasted_iota(jnp.int32, (n_keys, n_q), 0)
    i = lax.broadcasted_iota(jnp.int32, (n_keys, n_q), 1)
    lo = (i >> 6) << 6
    for h in range(bias_ref.shape[0]):
        base = jnp.broadcast_to(rvec_ref[h:h + 1, :], (n_keys, BIAS_VEC))
        rolled = pltpu.roll(base, 0, 1, stride=1, stride_axis=0)
        b = rolled[:, :n_q] * LOG2E
        bias_ref[h] = jnp.where(r >= lo, jnp.where(r < lo + A_WINDOW + CHUNK, b, _NEG_INF), _NEG_INF)


def _band_prompt_kernel(qa_ref, ka_ref, va_ref, ga_ref, qm_ref, gm_ref, mk_ref, mv_ref, rvec_ref,
                        oa_ref, om_ref, kpad, vt_blk, mk_bf, mv_t, bias_ref):
    j = pl.program_id(1)
    n_front = A_WINDOW // BAND_TQ
    n_win = BAND_WIN // BAND_TQ

    @pl.when(jnp.logical_and(pl.program_id(0) == 0, j == 0))
    def _():
        _build_band_bias_t(rvec_ref, bias_ref)

    @pl.when(j == 0)
    def _():
        kpad[:A_WINDOW, :] = jnp.zeros((A_WINDOW, D_A), jnp.bfloat16)
        kpad[A_WINDOW:, :] = ka_ref[0]
        for t in range(vt_blk.shape[0]):
            if t < n_front:
                vt_blk[t] = jnp.zeros((D_A, BAND_TQ), jnp.bfloat16)
            else:
                rows = slice((t - n_front) * BAND_TQ, (t - n_front + 1) * BAND_TQ)
                vt_blk[t] = va_ref[0, rows, :].astype(jnp.float32).T.astype(jnp.bfloat16)
        mk_bf[...] = mk_ref[0].astype(jnp.bfloat16)
        mv_t[...] = mv_ref[0].T.astype(jnp.bfloat16)

    start = pl.multiple_of(j * BAND_TQ, BAND_TQ)

    def store_a(sl, o):
        oa_ref[0, :, sl] = o

    def store_m(sl, o):
        om_ref[0, :, sl] = o

    def run(mask_front):
        if mask_front:
            pos = lax.broadcasted_iota(jnp.int32, (BAND_WIN, BAND_TQ), 0) + j * BAND_TQ
            front = jnp.where(pos >= A_WINDOW, 0.0, _NEG_INF)
        _kq_attend(lambda h: bias_ref[h] + front if mask_front else bias_ref[h],
                   qa_ref[0], ga_ref[0],
                   lambda sl: kpad[pl.ds(start, BAND_WIN), sl],
                   lambda sl: jnp.concatenate([vt_blk[j + u, sl, :] for u in range(n_win)], axis=1),
                   store_a)
        _kq_attend(lambda h: None, qm_ref[0], gm_ref[0], lambda sl: mk_bf[:, sl], lambda sl: mv_t[sl, :],
                   store_m)

    pl.when(j < n_front)(lambda: run(True))
    pl.when(j >= n_front)(lambda: run(False))


def _band_prompt_call(qa, ka, va, ga, qm, gm, mk, mv, bias_rvec):
    B, S, _ = qa.shape
    nq = S // BAND_TQ
    n_mem = mk.shape[1]
    assert nq * BAND_TQ == S and A_WINDOW % BAND_TQ == 0
    qblk = lambda n: pl.BlockSpec((1, BAND_TQ, n), lambda b, j: (b, j, 0))
    full = lambda r, n: pl.BlockSpec((1, r, n), lambda b, j: (b, 0, 0))
    return pl.pallas_call(
        _band_prompt_kernel,
        grid=(B, nq),
        in_specs=[qblk(D_A), full(S, D_A), full(S, D_A), qblk(D_A), qblk(D_M), qblk(D_M),
                  full(n_mem, D_M), full(n_mem, D_M),
                  pl.BlockSpec((H_A, BIAS_VEC), lambda b, j: (0, 0))],
        out_specs=[qblk(D_A), qblk(D_M)],
        out_shape=[jax.ShapeDtypeStruct((B, S, D_A), jnp.bfloat16),
                   jax.ShapeDtypeStruct((B, S, D_M), jnp.bfloat16)],
        scratch_shapes=[pltpu.VMEM((S + A_WINDOW, D_A), jnp.bfloat16),
                        pltpu.VMEM((nq + A_WINDOW // BAND_TQ, D_A, BAND_TQ), jnp.bfloat16),
                        pltpu.VMEM((n_mem, D_M), jnp.bfloat16),
                        pltpu.VMEM((D_M, n_mem), jnp.bfloat16),
                        pltpu.VMEM((H_A, BAND_WIN, BAND_TQ), jnp.float32)],
        compiler_params=_cparams(2),
        name="band_prompt",
    )(qa, ka, va, ga, qm, gm, mk, mv, bias_rvec)


def _band_sample_kernel(qa_ref, kc_ref, vc_ref, kn_ref, vn_ref, ga_ref, qm_ref, gm_ref,
                        mk_ref, mv_ref, vec_ref, oa_ref, om_ref, kcat, vcat, bias_ref):
    P = kc_ref.shape[1]
    T = kn_ref.shape[1]
    pad = kcat.shape[0] - P - T

    @pl.when(pl.program_id(0) == 0)
    def _():
        _build_band_bias(vec_ref, bias_ref, T, P + T, False)

    kcat[:P, :] = kc_ref[0].astype(jnp.bfloat16)
    vcat[:P, :] = vc_ref[0].astype(jnp.bfloat16)
    kcat[P:P + T, :] = kn_ref[0]
    vcat[P:P + T, :] = vn_ref[0]
    zeros = jnp.zeros((pad, D_A), jnp.bfloat16)
    kcat[P + T:, :] = zeros
    vcat[P + T:, :] = zeros
    for p in range(D_A // LANES):
        sl = slice(p * LANES, (p + 1) * LANES)
        o = _pair_attend(qa_ref[0, :, sl], kcat[:, sl], vcat[:, sl],
                         lambda s, hh, p=p: s + bias_ref[2 * p + hh])
        oa_ref[0, :, sl] = (o * ga_ref[0, :, sl].astype(jnp.float32)).astype(jnp.bfloat16)
    _mem_attend(qm_ref, gm_ref, mk_ref, mv_ref, om_ref)


def _band_sample_call(qa, kc, vc, kn, vn, ga, qm, gm, mk, mv, bias_vec):
    B, T, _ = qa.shape
    P = kc.shape[1]
    n_keys = -(-(P + T) // LANES) * LANES
    n_mem = mk.shape[1]
    blk = lambda r, n: pl.BlockSpec((1, r, n), lambda b: (b, 0, 0))
    return pl.pallas_call(
        _band_sample_kernel,
        grid=(B,),
        in_specs=[blk(T, D_A), blk(P, D_A), blk(P, D_A), blk(T, D_A), blk(T, D_A), blk(T, D_A),
                  blk(T, D_M), blk(T, D_M), blk(n_mem, D_M), blk(n_mem, D_M),
                  pl.BlockSpec((H_A, BIAS_VEC), lambda b: (0, 0))],
        out_specs=[blk(T, D_A), blk(T, D_M)],
        out_shape=[jax.ShapeDtypeStruct((B, T, D_A), jnp.bfloat16),
                   jax.ShapeDtypeStruct((B, T, D_M), jnp.bfloat16)],
        scratch_shapes=[pltpu.VMEM((n_keys, D_A), jnp.bfloat16)] * 2
        + [pltpu.VMEM((H_A, T, n_keys), jnp.float32)],
        compiler_params=_cparams(1),
        name="band_sample",
    )(qa, kc, vc, kn, vn, ga, qm, gm, mk, mv, bias_vec)


def _key_to_f32(k):
    bits = k ^ ((k >> 31) & jnp.int32(0x7FFFFFFF))
    return lax.bitcast_convert_type(bits, jnp.float32)


def _count(pred_f32):
    ones = jnp.ones((8, pred_f32.shape[0]), jnp.bfloat16)
    c = jnp.dot(ones, pred_f32.astype(jnp.bfloat16), preferred_element_type=jnp.float32)
    return c[0:1, :]


def _dsa_select(sc_ref, probs, tq, topk):
    one, zero = jnp.float32(1.0), jnp.float32(0.0)
    live = [(off, tk) for off, tk in probs if tk > topk]
    n_live = len(live)

    def sc_of(n):
        off, tk = live[n]
        return sc_ref[off:off + tk, :]

    if live:
        def thr_step(i, ts):
            bit = lax.shift_left(jnp.int32(1), 31 - i)
            out = []
            for n in range(n_live):
                cand = ts[n] + bit
                thr = _key_to_f32(cand)
                off, tk = live[n]
                n_vec = (tk * 7 // 32) // 64 * 64
                n_mxu = tk - n_vec
                c = _count(jnp.where(sc_ref[off:off + n_mxu, :] >= thr, one, zero))
                if n_vec:
                    c = c + jnp.sum(jnp.where(sc_ref[off + n_mxu:off + tk, :] >= thr, one, zero),
                                    axis=0, keepdims=True)
                out.append(jnp.where(c >= topk, cand, ts[n]))
            return tuple(out)

        start = tuple(jnp.full((1, tq), _INT_MIN, jnp.int32) for _ in range(n_live))
        t_keys = lax.fori_loop(0, 32, thr_step, start)
        thrs = [jnp.where(t == _INT_MIN, _NEG_INF, _key_to_f32(t)) for t in t_keys]

        for n in range(n_live):
            off, tk = live[n]
            c_ge = _count(jnp.where(sc_of(n) >= thrs[n], one, zero))

            def tie_search(n=n, off=off, tk=tk):
                need = topk - _count(jnp.where(sc_of(n) > thrs[n], one, zero))
                tile = 2 * LANES

                def step(first, size, state):
                    before, jmax = state
                    r = lax.broadcasted_iota(jnp.int32, (size, size), 0)
                    c = lax.broadcasted_iota(jnp.int32, (size, size), 1)
                    lower = jnp.where(c <= r, one, zero).astype(jnp.bfloat16)
                    tied = jnp.where(sc_ref[pl.ds(off + first, size), :] == thrs[n], one, zero)
                    rank = before + jnp.dot(lower, tied.astype(jnp.bfloat16),
                                            preferred_element_type=jnp.float32)
                    kidx = (lax.broadcasted_iota(jnp.int32, (size, tq), 0) + first).astype(jnp.float32)
                    taken = jnp.where(tied > 0, jnp.where(rank <= need, kidx, -one), -one)
                    jmax = jnp.maximum(jmax, jnp.max(taken, axis=0, keepdims=True))
                    return rank[size - 1:size, :], jmax

                state = (jnp.zeros((1, tq), jnp.float32), jnp.full((1, tq), -1.0, jnp.float32))
                n_full, rest = divmod(tk, tile)
                state = lax.fori_loop(
                    0, n_full, lambda t, st: step(pl.multiple_of(t * tile, tile), tile, st), state)
                if rest:
                    state = step(n_full * tile, rest, state)
                return state[1].astype(jnp.int32)

            def keep_with_ties(n=n, off=off, tk=tk, tie_search=tie_search):
                jmax = tie_search()
                sc = sc_of(n)
                kidx = lax.broadcasted_iota(jnp.int32, (tk, tq), 0)
                fin = jnp.where(jnp.abs(sc) < jnp.float32(jnp.inf), zero, _NEG_INF)
                sc_ref[off:off + tk, :] = jnp.where(
                    sc > thrs[n], fin,
                    jnp.where(sc == thrs[n], jnp.where(kidx <= jmax, fin, _NEG_INF), _NEG_INF))

            def keep_plain(n=n, off=off, tk=tk):
                sc = sc_of(n)
                fin = jnp.where(jnp.abs(sc) < jnp.float32(jnp.inf), zero, _NEG_INF)
                sc_ref[off:off + tk, :] = jnp.where(sc >= thrs[n], fin, _NEG_INF)

            has_excess = jnp.max(jnp.where(c_ge > topk, one, zero)) > 0
            lax.cond(has_excess, keep_with_ties, keep_plain)

    for off, tk in probs:
        if tk <= topk:
            sc = sc_ref[off:off + tk, :]
            sc_ref[off:off + tk, :] = jnp.where(jnp.abs(sc) < jnp.float32(jnp.inf), zero, _NEG_INF)


def _dsa_scores(limit, qi, kiwi, ki4):
    tk, tq = ki4.shape[0], qi.shape[0]
    kiwi_t = kiwi.T
    acc = jnp.zeros((tk, tq), jnp.float32)
    heads_per_slab = LANES // D_IDX

    def head_q(h):
        slab = qi[:, (h // heads_per_slab) * LANES:(h // heads_per_slab + 1) * LANES]
        lo = (h % heads_per_slab) * D_IDX
        return _keep_lanes(slab, lo, lo + D_IDX)

    for h in range(0, H_IDX, 2):
        d = lax.dot_general(ki4, jnp.concatenate([head_q(h), head_q(h + 1)], axis=0), _NT,
                            preferred_element_type=jnp.float32)
        for e in range(2):
            w = kiwi_t[D_IDX + h + e:D_IDX + h + e + 1, :] * ((D_IDX ** -0.5) * (H_IDX ** -0.5))
            acc = acc + w * jnp.maximum(d[:, e * tq:(e + 1) * tq], 0.0)
    if limit is not None:
        kidx = lax.broadcasted_iota(jnp.int32, (tk, tq), 0)
        acc = jnp.where(kidx < limit, acc, _NEG_INF)
    return acc


def _kq_attend(bias_of, q, g, k_slab_of, vt_slab_of, store):
    tq = q.shape[0]
    for p in range(q.shape[1] // LANES):
        sl = slice(p * LANES, (p + 1) * LANES)
        q_slab = q[:, sl]
        q_pair = jnp.concatenate([_keep_lanes(q_slab, 0, D_HEAD), _keep_lanes(q_slab, D_HEAD, LANES)],
                                 axis=0)
        s_pair = lax.dot_general(k_slab_of(sl), q_pair, _NT, preferred_element_type=jnp.float32)
        probs, inv_l = [], []
        for hh in range(2):
            s = s_pair[:, hh * tq:(hh + 1) * tq]
            bias = bias_of(2 * p + hh)
            if bias is not None:
                s = s + bias
            m = jnp.max(s, axis=0, keepdims=True)
            pr = jnp.exp2(s - m)
            inv_l.append(1.0 / jnp.sum(pr, axis=0, keepdims=True))
            probs.append(pr.astype(jnp.bfloat16))
        o_pair = jnp.dot(vt_slab_of(sl), jnp.concatenate(probs, axis=1),
                         preferred_element_type=jnp.float32)
        ot = jnp.concatenate([o_pair[:D_HEAD, :tq] * inv_l[0], o_pair[D_HEAD:, tq:] * inv_l[1]], axis=0)
        store(sl, (ot.T * g[:, sl].astype(jnp.float32)).astype(jnp.bfloat16))


def _dsa_blocks(S, tq):
    return [(tq * j * (j + 1) // 2, (j + 1) * tq) for j in range(S // tq)]


def _own_region():
    return pl.when(pl.program_id(0) >= 0)


def _dsa_select_kernel(qi_ref, kiwi_ref, ki4_ref, keep_ref, sc_ref, *, topk, tq):
    own_region = _own_region()
    probs = _dsa_blocks(ki4_ref.shape[1], tq)
    for j, (off, tk) in enumerate(probs):
        rows = slice(j * tq, (j + 1) * tq)

        @own_region
        def _(j=j, tk=tk, off=off, rows=rows):
            qpos = j * tq + lax.broadcasted_iota(jnp.int32, (1, tq), 1)
            limit = ((qpos >> 6) + 1) << 6
            sc_ref[off:off + tk, :] = _dsa_scores(limit, qi_ref[0, rows, :], kiwi_ref[0, rows, :],
                                                  ki4_ref[0, :tk, :])

    _dsa_select(sc_ref, probs, tq, topk)
    keep_ref[0] = sc_ref[...].astype(jnp.bfloat16)


def _dsa_attend_kernel(q_ref, g_ref, k_ref, v_ref, keep_ref, o_ref, vt, *, tq):
    own_region = _own_region()
    S = k_ref.shape[1]
    for t in range(S // tq):
        vt[t] = v_ref[0, t * tq:(t + 1) * tq, :].astype(jnp.float32).T.astype(jnp.bfloat16)
    for j, (off, tk) in enumerate(_dsa_blocks(S, tq)):
        rows = slice(j * tq, (j + 1) * tq)

        @own_region
        def _(j=j, tk=tk, off=off, rows=rows):
            def store(sl, o):
                o_ref[0, rows, sl] = o

            bias = keep_ref[0, off:off + tk, :].astype(jnp.float32)
            _kq_attend(lambda h: bias, q_ref[0, rows, :], g_ref[0, rows, :],
                       lambda sl: k_ref[0, :tk, sl],
                       lambda sl: jnp.concatenate([vt[t, sl, :] for t in range(j + 1)], axis=1), store)


def _dsa_prompt_call(qb, qi, kiwi, gb, kb, vb, ki4):
    B, S, _ = qb.shape
    tq = DSA_TQ
    nq = S // tq
    assert CHUNK == 64 and nq * tq == S
    topk = min(TOPK_MAX, S // 4)
    n_rows = tq * nq * (nq + 1) // 2
    full = lambda n: pl.BlockSpec((1, S, n), lambda b: (b, 0, 0))
    keep_spec = pl.BlockSpec((1, n_rows, tq), lambda b: (b, 0, 0))
    keep = pl.pallas_call(
        functools.partial(_dsa_select_kernel, topk=topk, tq=tq),
        grid=(B,),
        in_specs=[full(H_IDX * D_IDX), full(LANES), full(LANES)],
        out_specs=keep_spec,
        out_shape=jax.ShapeDtypeStruct((B, n_rows, tq), jnp.bfloat16),
        scratch_shapes=[pltpu.VMEM((n_rows, tq), jnp.float32)],
        compiler_params=_cparams(1),
        name="dsa_select",
    )(qi, kiwi, ki4)
    return pl.pallas_call(
        functools.partial(_dsa_attend_kernel, tq=tq),
        grid=(B,),
        in_specs=[full(D_B), full(D_B), full(D_B), full(D_B), keep_spec],
        out_specs=full(D_B),
        out_shape=jax.ShapeDtypeStruct((B, S, D_B), jnp.bfloat16),
        scratch_shapes=[pltpu.VMEM((nq, D_B, tq), jnp.bfloat16)],
        compiler_params=_cparams(1),
        name="dsa_attend",
    )(qb, gb, kb, vb, keep)


def _dsa_sample_kernel(q_ref, qi_ref, kiwi_ref, g_ref, kc_ref, vc_ref, kic_ref,
                       kn_ref, vn_ref, kin_ref, o_ref, kbf, vt, ki4s, sc_ref, *, topk):
    P = kc_ref.shape[1]
    T = kn_ref.shape[1]
    tq = q_ref.shape[1]
    kbf[:P, :] = kc_ref[0].astype(jnp.bfloat16)
    kbf[P:, :] = kn_ref[0].astype(jnp.bfloat16)
    vt[:, :P] = vc_ref[0].T.astype(jnp.bfloat16)
    vn_tile = jnp.concatenate([vn_ref[0], jnp.zeros((LANES - T, D_B), jnp.float32)], axis=0)
    vt[:, P:] = vn_tile.T[:, :T].astype(jnp.bfloat16)
    ki4s[:P, :] = kic_ref[0]
    ki4s[P:, :] = kin_ref[0]
    sc_ref[...] = _dsa_scores(None, qi_ref[0], kiwi_ref[0], ki4s[...])
    _dsa_select(sc_ref, [(0, P + T)], tq, topk)

    def store(sl, o):
        o_ref[0, :, sl] = o

    _kq_attend(lambda h: sc_ref[...], q_ref[0], g_ref[0], lambda sl: kbf[:, sl], lambda sl: vt[sl, :],
               store)


def _dsa_sample_call(qb, qi, kiwi, gb, kc, vc, kic, kn, vn, kin):
    B, tq, _ = qb.shape
    P, T = kc.shape[1], kn.shape[1]
    tk = P + T
    topk = min(TOPK_MAX, tk // 4)
    blk = lambda r, n: pl.BlockSpec((1, r, n), lambda b: (b, 0, 0))
    return pl.pallas_call(
        functools.partial(_dsa_sample_kernel, topk=topk),
        grid=(B,),
        in_specs=[blk(tq, D_B), blk(tq, H_IDX * D_IDX), blk(tq, LANES), blk(tq, D_B),
                  blk(P, D_B), blk(P, D_B), blk(P, LANES),
                  blk(T, D_B), blk(T, D_B), blk(T, LANES)],
        out_specs=blk(tq, D_B),
        out_shape=jax.ShapeDtypeStruct((B, tq, D_B), jnp.bfloat16),
        scratch_shapes=[pltpu.VMEM((tk, D_B), jnp.bfloat16),
                        pltpu.VMEM((D_B, tk), jnp.bfloat16),
                        pltpu.VMEM((tk, LANES), jnp.bfloat16),
                        pltpu.VMEM((tk, tq), jnp.float32)],
        compiler_params=_cparams(1),
        name="dsa_sample",
    )(qb, qi, kiwi, gb, kc, vc, kic, kn, vn, kin)


def _merge_kernel(x_ref, oa_ref, ob_ref, om_ref, w_ref, g_ref, y_ref, *, last):
    acc = x_ref[...]
    acc = acc + jnp.dot(oa_ref[...], w_ref[:D_A, :], preferred_element_type=jnp.float32)
    acc = acc + jnp.dot(ob_ref[...], w_ref[D_A:D_A + D_B, :], preferred_element_type=jnp.float32)
    acc = acc + jnp.dot(om_ref[...], w_ref[D_A + D_B:, :], preferred_element_type=jnp.float32)
    y_ref[...] = _rms_scale(acc, g_ref[...]) if last else acc


def _merge_streamed_kernel(x_hbm, oa_hbm, ob_hbm, om_hbm, w_ref, g_ref, y_hbm, *, last, tm, n_steps):
    deep = lambda n: pl.BlockSpec((tm, n), lambda i: (i, 0), pipeline_mode=pl.Buffered(3))
    D = x_hbm.shape[1]

    def body(x_ref, oa_ref, ob_ref, om_ref, y_ref):
        _merge_kernel(x_ref, oa_ref, ob_ref, om_ref, w_ref, g_ref, y_ref, last=last)

    pltpu.emit_pipeline(
        body, grid=(n_steps,),
        in_specs=[deep(D), deep(D_A), deep(D_B), deep(D_M)],
        out_specs=[pl.BlockSpec((tm, D), lambda i: (i, 0))],
    )(x_hbm, oa_hbm, ob_hbm, om_hbm, y_hbm)


def _merge_call(x2d, oa, ob, om, w, g, last):
    R, D = x2d.shape
    tm = min(MERGE_TM, R)
    assert R % tm == 0
    n_steps = R // tm
    if n_steps >= 3:
        any_spec = pl.BlockSpec(memory_space=pl.ANY)
        vmem_spec = pl.BlockSpec(memory_space=pltpu.VMEM)
        return pl.pallas_call(
            functools.partial(_merge_streamed_kernel, last=last, tm=tm, n_steps=n_steps),
            in_specs=[any_spec] * 4 + [vmem_spec, vmem_spec],
            out_specs=any_spec,
            out_shape=jax.ShapeDtypeStruct((R, D), jnp.float32),
            compiler_params=pltpu.CompilerParams(vmem_limit_bytes=VMEM_LIMIT_BYTES),
            name="merge",
        )(x2d, oa, ob, om, w, g)
    row = lambda n: pl.BlockSpec((tm, n), lambda i: (i, 0))
    return pl.pallas_call(
        functools.partial(_merge_kernel, last=last),
        grid=(n_steps,),
        in_specs=[row(D), row(D_A), row(D_B), row(D_M),
                  pl.BlockSpec((D_A + D_B + D_M, D), lambda i: (0, 0)),
                  pl.BlockSpec((1, D), lambda i: (0, 0))],
        out_specs=row(D),
        out_shape=jax.ShapeDtypeStruct((R, D), jnp.float32),
        compiler_params=_cparams(1),
        name="merge",
    )(x2d, oa, ob, om, w, g)


def _pack_w_in(w):
    D = w.shape[0]
    ki = w[:, C_KIWI:C_KIWI + D_IDX]
    wi = w[:, C_KIWI + D_IDX:C_KIWI + D_IDX + H_IDX]
    pad = jnp.zeros((D, LANES - D_IDX - H_IDX), w.dtype)
    packed = jnp.concatenate([w[:, :C_KIWI], ki, wi, pad, jnp.tile(ki, (1, LANES // D_IDX))], axis=1)
    return packed.astype(jnp.bfloat16)


def _rope_tables(pos):
    posf = pos.astype(jnp.float32)

    def tables(d):
        half = d // 2
        inv_freq = ROPE_THETA ** (-jnp.arange(half, dtype=jnp.float32) * 2.0 / d)
        ang = posf[:, None] * inv_freq[None, :]
        cos, sin = jnp.cos(ang), jnp.sin(ang)
        return jnp.concatenate([cos, cos], axis=1), jnp.concatenate([-sin, sin], axis=1)

    cos64, sin64 = tables(D_HEAD)
    cos32, sin32 = tables(D_IDX)
    n = pos.shape[0]
    rest = LANES - D_IDX
    cos_kw = jnp.concatenate([cos32, jnp.ones((n, rest), jnp.float32)], axis=1)
    sin_kw = jnp.concatenate([sin32, jnp.zeros((n, rest), jnp.float32)], axis=1)
    rep = lambda t, d: jnp.tile(t, (1, LANES // d))
    return jnp.stack([rep(cos64, D_HEAD), rep(sin64, D_HEAD), rep(cos32, D_IDX), rep(sin32, D_IDX),
                      cos_kw, sin_kw])


def _band_bias_vec(table):
    n_far = A_WINDOW - REL_CLIP + 1
    lo_idx = A_WINDOW + REL_CLIP - (BAND_WIN - 1)
    assert lo_idx >= 0 and BIAS_VEC >= BAND_WIN + BAND_TQ - 1
    far = table[:, 2 * REL_CLIP:2 * REL_CLIP + 1].astype(jnp.float32)
    mid = table[:, lo_idx:2 * REL_CLIP][:, ::-1].astype(jnp.float32)
    vec = jnp.concatenate([jnp.tile(far, (1, n_far)), mid, jnp.tile(far, (1, BIAS_VEC - BAND_WIN))], axis=1)
    rvec = jnp.concatenate([vec[:, :1], vec[:, 1:][:, ::-1]], axis=1)
    return vec, rvec


def kernel(x_prompt, x_sample, mem_prompt, cache_a_k, cache_a_v, cache_b_k, cache_b_v, cache_b_kidx,
           cache_mem_k, cache_mem_v, norm_mix_g, w_in, rel_bias_a, norm_mem_g, w_mem_kv, w_out,
           norm_final_g):
    B, S, D = x_prompt.shape
    Bs, T, _ = x_sample.shape
    depth = w_in.shape[0]
    P = cache_b_k.shape[2]
    Pa = cache_a_k.shape[2]
    n_mem = mem_prompt.shape[1]
    keep = min(A_WINDOW, S)
    assert Pa == A_WINDOW and T <= CHUNK and LANES % T == 0

    rope_p = _rope_tables(jnp.arange(S))
    rope_s = jnp.tile(_rope_tables(P + jnp.arange(T)), (1, Bs, 1))
    g_final = norm_final_g.reshape(1, D)

    xp, xs = x_prompt, x_sample
    outs_p = [[] for _ in range(7)]
    outs_s = [[] for _ in range(5)]
    for l in range(depth):
        last = l == depth - 1
        w = _pack_w_in(w_in[l])
        g_mix = norm_mix_g[l].reshape(1, D)
        w_o = w_out[l].astype(jnp.bfloat16)
        bias_vec, bias_rvec = _band_bias_vec(rel_bias_a[l])

        (qa, ka, va, ga, qb, kb, vb, gb, qm, gm, qi, kiwi, ki4, ki, kbb, vbb, ak, av) = _proj_call(
            xp, g_mix, w, rope_p, keep)
        mk, mv = _memkv_call(mem_prompt.reshape(B * n_mem, D), norm_mem_g[l].reshape(1, D),
                             w_mem_kv[l].astype(jnp.bfloat16))
        mk = mk.reshape(B, n_mem, D_M)
        mv = mv.reshape(B, n_mem, D_M)
        oa, om = _band_prompt_call(qa, ka, va, ga, qm, gm, mk, mv, bias_rvec)
        ob = _dsa_prompt_call(qb, qi, kiwi, gb, kbb, vbb, ki4)
        xp = _merge_call(xp.reshape(B * S, D), oa.reshape(B * S, D_A), ob.reshape(B * S, D_B),
                         om.reshape(B * S, D_M), w_o, g_final, last).reshape(B, S, D)
        for lst, t in zip(outs_p, (ak.reshape(B, keep, H_A, D_HEAD), av.reshape(B, keep, H_A, D_HEAD),
                                   kb.reshape(B, S, H_B, D_HEAD), vb.reshape(B, S, H_B, D_HEAD), ki,
                                   mk.reshape(B, n_mem, H_M, D_HEAD), mv.reshape(B, n_mem, H_M, D_HEAD))):
            lst.append(t)

        (qa, ka, va, ga, qb, kb, vb, gb, qm, gm, qi, kiwi, ki4, ki, kbb, vbb, ak, av) = _proj_call(
            xs.reshape(1, Bs * T, D), g_mix, w, rope_s, Bs * T)
        per_b = lambda t: t.reshape(Bs, T, t.shape[-1])
        oa, om = _band_sample_call(
            per_b(qa), cache_a_k[l].reshape(Bs, Pa, D_A), cache_a_v[l].reshape(Bs, Pa, D_A),
            per_b(ka), per_b(va), per_b(ga), per_b(qm), per_b(gm),
            cache_mem_k[l].reshape(Bs, n_mem, D_M), cache_mem_v[l].reshape(Bs, n_mem, D_M), bias_vec)
        rep = lambda t: jnp.tile(per_b(t), (1, LANES // T, 1))
        kic = jnp.tile(cache_b_kidx[l], (1, 1, LANES // D_IDX)).astype(jnp.bfloat16)
        ob = _dsa_sample_call(rep(qb), rep(qi), rep(kiwi), rep(gb),
                              cache_b_k[l].reshape(Bs, P, D_B), cache_b_v[l].reshape(Bs, P, D_B), kic,
                              per_b(kb), per_b(vb), per_b(ki4))[:, :T]
        xs = _merge_call(xs.reshape(Bs * T, D), oa.reshape(Bs * T, D_A), ob.reshape(Bs * T, D_B),
                         om.reshape(Bs * T, D_M), w_o, g_final, last).reshape(Bs, T, D)
        for lst, t in zip(outs_s, (ak.reshape(Bs, T, H_A, D_HEAD), av.reshape(Bs, T, H_A, D_HEAD),
                                   kb.reshape(Bs, T, H_B, D_HEAD), vb.reshape(Bs, T, H_B, D_HEAD),
                                   ki.reshape(Bs, T, D_IDX))):
            lst.append(t)

    st = lambda ts: jnp.stack(ts, axis=0)
    return (xp, xs) + tuple(st(t) for t in outs_p) + tuple(st(t) for t in outs_s)
```

```python
import functools
import math

import jax
import jax.numpy as jnp
from jax import lax
from jax.experimental import pallas as pl
from jax.experimental.pallas import tpu as pltpu

CHUNK = 64
D_HEAD = 64
H_A = 6
H_B = 6
H_M = 4
D_A = H_A * D_HEAD
D_B = H_B * D_HEAD
D_M = H_M * D_HEAD
A_LEFT_CHUNKS = 8
A_WINDOW = A_LEFT_CHUNKS * CHUNK
REL_CLIP = 256
H_IDX = 8
D_IDX = 32
TOPK_MAX = 256
ROPE_THETA = 10000.0
EPS = 1e-6

LANES = 128
VMEM_LIMIT_BYTES = 56 * 1024 * 1024

C_QA, C_KA, C_VA, C_GA = 0, D_A, 2 * D_A, 3 * D_A
C_QB = 4 * D_A
C_KB, C_VB, C_GB = C_QB + D_B, C_QB + 2 * D_B, C_QB + 3 * D_B
C_QM = C_QB + 4 * D_B
C_GM = C_QM + D_M
C_QI = C_GM + D_M
C_KIWI = C_QI + H_IDX * D_IDX
C_KI4 = C_KIWI + LANES
W_COLS = C_KI4 + LANES

PROJ_TM = 512
MERGE_TM = 1024
BAND_TQ = 256
BAND_WIN = A_WINDOW + BAND_TQ
DSA_TQ = 256

LOG2E = math.log2(math.e)
QK_SCALE = (D_HEAD ** -0.5) * LOG2E
BIAS_VEC = 8 * LANES

_NT = (((1,), (1,)), ((), ()))
_INT_MIN = -2 ** 31
_NEG_INF = float("-inf")


def _cparams(n_axes):
    return pltpu.CompilerParams(
        dimension_semantics=("arbitrary",) * n_axes,
        vmem_limit_bytes=VMEM_LIMIT_BYTES)


def _silu(g):
    return g * (1.0 / (1.0 + jnp.exp(-g)))


def _rms_scale(x, g):
    ms = jnp.mean(x * x, axis=-1, keepdims=True)
    return (x * lax.rsqrt(ms + EPS)) * g


def _rope_slab(z, cos, sin, half):
    lane = lax.broadcasted_iota(jnp.int32, z.shape, 1)
    first = (lane & (2 * half - 1)) < half
    partner = jnp.where(first, pltpu.roll(z, LANES - half, 1), pltpu.roll(z, half, 1))
    return z * cos + partner * sin


def _proj_kernel(x_ref, g_ref, w_ref, rope_ref,
                 qa_ref, ka_ref, va_ref, ga_ref, qb_ref, kb_ref, vb_ref, gb_ref,
                 qm_ref, gm_ref, qi_ref, kiwi_ref, ki4_ref, ki_ref, kbb_ref, vbb_ref, ak_ref, av_ref,
                 *, n_tiles, keep_tiles):
    i = pl.program_id(1)
    scale = QK_SCALE
    tm = x_ref.shape[1]
    n_sub = 2 if tm % 32 == 0 else 1
    kept = []
    for u in range(n_sub):
        rows = slice(u * tm // n_sub, (u + 1) * tm // n_sub)
        xn = _rms_scale(x_ref[0, rows, :], g_ref[...]).astype(jnp.bfloat16)

        def proj(c0, n, xn=xn):
            return jnp.dot(xn, w_ref[:, c0:c0 + n], preferred_element_type=jnp.float32)

        z = proj(C_QA, 2 * D_A)
        qa_ref[0, rows, :] = (z[:, :D_A] * scale).astype(jnp.bfloat16)
        ka = z[:, D_A:]
        ka_ref[0, rows, :] = ka.astype(jnp.bfloat16)
        z = proj(C_VA, 2 * D_A)
        va = z[:, :D_A]
        va_ref[0, rows, :] = va.astype(jnp.bfloat16)
        ga_ref[0, rows, :] = _silu(z[:, D_A:]).astype(jnp.bfloat16)
        kept.append((rows, ka, va))

        cos64, sin64 = rope_ref[0, rows, :], rope_ref[1, rows, :]
        cos32, sin32 = rope_ref[2, rows, :], rope_ref[3, rows, :]
        cos_kw, sin_kw = rope_ref[4, rows, :], rope_ref[5, rows, :]
        z = proj(C_QB, 2 * D_B)
        for s in range(D_B // LANES):
            sl = slice(s * LANES, (s + 1) * LANES)
            qb = _rope_slab(z[:, sl], cos64, sin64, D_HEAD // 2)
            qb_ref[0, rows, sl] = (qb * scale).astype(jnp.bfloat16)
            kb = _rope_slab(z[:, D_B + s * LANES:D_B + (s + 1) * LANES], cos64, sin64, D_HEAD // 2)
            kb_ref[0, rows, sl] = kb
            kbb_ref[0, rows, sl] = kb.astype(jnp.bfloat16)
        z = proj(C_VB, 2 * D_B)
        vb_ref[0, rows, :] = z[:, :D_B]
        vbb_ref[0, rows, :] = z[:, :D_B].astype(jnp.bfloat16)
        gb_ref[0, rows, :] = _silu(z[:, D_B:]).astype(jnp.bfloat16)
        qm_ref[0, rows, :] = (proj(C_QM, D_M) * scale).astype(jnp.bfloat16)
        gm_ref[0, rows, :] = _silu(proj(C_GM, D_M)).astype(jnp.bfloat16)
        z = proj(C_QI, H_IDX * D_IDX)
        for s in range(H_IDX * D_IDX // LANES):
            sl = slice(s * LANES, (s + 1) * LANES)
            qi_ref[0, rows, sl] = _rope_slab(z[:, sl], cos32, sin32, D_IDX // 2).astype(jnp.bfloat16)
        z = proj(C_KIWI, 2 * LANES)
        kiwi = _rope_slab(z[:, :LANES], cos_kw, sin_kw, D_IDX // 2)
        kiwi_ref[0, rows, :] = kiwi
        ki_ref[0, rows, :] = kiwi[:, :D_IDX]
        ki4_ref[0, rows, :] = _rope_slab(z[:, LANES:], cos32, sin32, D_IDX // 2).astype(jnp.bfloat16)

    @pl.when(i >= n_tiles - keep_tiles)
    def _():
        for rows, ka, va in kept:
            ak_ref[0, rows, :] = ka
            av_ref[0, rows, :] = va


def _proj_call(x, g, w, rope, keep_rows):
    B, S, D = x.shape
    tm = min(PROJ_TM, S)
    n_tiles = S // tm
    keep_tiles = keep_rows // tm
    assert n_tiles * tm == S and keep_tiles * tm == keep_rows

    def tile(n, dtype):
        return (jax.ShapeDtypeStruct((B, S, n), dtype),
                pl.BlockSpec((1, tm, n), lambda b, i: (b, i, 0)))

    f32, bf16 = jnp.float32, jnp.bfloat16
    outs = [tile(D_A, bf16), tile(D_A, bf16), tile(D_A, bf16), tile(D_A, bf16),
            tile(D_B, bf16), tile(D_B, f32), tile(D_B, f32), tile(D_B, bf16),
            tile(D_M, bf16), tile(D_M, bf16), tile(H_IDX * D_IDX, bf16),
            tile(LANES, f32), tile(LANES, bf16), tile(D_IDX, f32), tile(D_B, bf16), tile(D_B, bf16)]
    keep_spec = pl.BlockSpec(
        (1, tm, D_A), lambda b, i: (b, jnp.maximum(i - (n_tiles - keep_tiles), 0), 0))
    outs += [(jax.ShapeDtypeStruct((B, keep_rows, D_A), f32), keep_spec)] * 2
    return pl.pallas_call(
        functools.partial(_proj_kernel, n_tiles=n_tiles, keep_tiles=keep_tiles),
        grid=(B, n_tiles),
        in_specs=[pl.BlockSpec((1, tm, D), lambda b, i: (b, i, 0)),
                  pl.BlockSpec((1, D), lambda b, i: (0, 0)),
                  pl.BlockSpec((D, W_COLS), lambda b, i: (0, 0)),
                  pl.BlockSpec((6, tm, LANES), lambda b, i: (0, i, 0))],
        out_specs=[o[1] for o in outs],
        out_shape=[o[0] for o in outs],
        compiler_params=_cparams(2),
        name="proj",
    )(x, g, w, rope)


def _memkv_kernel(m_ref, g_ref, w_ref, mk_ref, mv_ref):
    xn = _rms_scale(m_ref[...], g_ref[...]).astype(jnp.bfloat16)
    mk_ref[...] = jnp.dot(xn, w_ref[:, :D_M], preferred_element_type=jnp.float32)
    mv_ref[...] = jnp.dot(xn, w_ref[:, D_M:], preferred_element_type=jnp.float32)


def _memkv_call(mem2d, g, w):
    R, D = mem2d.shape
    tm = min(PROJ_TM, R)
    assert R % tm == 0
    row = lambda n: pl.BlockSpec((tm, n), lambda i: (i, 0))
    return pl.pallas_call(
        _memkv_kernel,
        grid=(R // tm,),
        in_specs=[row(D), pl.BlockSpec((1, D), lambda i: (0, 0)),
                  pl.BlockSpec((D, 2 * D_M), lambda i: (0, 0))],
        out_specs=[row(D_M), row(D_M)],
        out_shape=[jax.ShapeDtypeStruct((R, D_M), jnp.float32)] * 2,
        compiler_params=_cparams(1),
        name="memkv",
    )(mem2d, g, w)


def _softmax_rows_pv(s, v):
    m = jnp.max(s, axis=1, keepdims=True)
    p = jnp.exp2(s - m)
    l = jnp.sum(p, axis=1, keepdims=True)
    o = jnp.dot(p.astype(jnp.bfloat16), v, preferred_element_type=jnp.float32)
    return o * (1.0 / l)


def _keep_lanes(slab, lo, hi):
    x = slab.astype(jnp.float32)
    lane = lax.broadcasted_iota(jnp.int32, x.shape, 1)
    return jnp.where(lane >= lo, jnp.where(lane < hi, x, 0.0), 0.0).astype(jnp.bfloat16)


def _pair_attend(q_slab, k_slab, v_slab, bias_fn):
    lane = lax.broadcasted_iota(jnp.int32, q_slab.shape, 1)
    out = None
    for hh in range(2):
        s = lax.dot_general(_keep_lanes(q_slab, hh * D_HEAD, (hh + 1) * D_HEAD), k_slab, _NT,
                            preferred_element_type=jnp.float32)
        o = _softmax_rows_pv(bias_fn(s, hh), v_slab)
        out = o if hh == 0 else jnp.where(lane < D_HEAD, out, o)
    return out


def _mem_attend(qm_ref, gm_ref, mk_ref, mv_ref, om_ref):
    for p in range(D_M // LANES):
        sl = slice(p * LANES, (p + 1) * LANES)
        o = _pair_attend(qm_ref[0, :, sl], mk_ref[0, :, sl].astype(jnp.bfloat16),
                         mv_ref[0, :, sl].astype(jnp.bfloat16), lambda s, hh: s)
        om_ref[0, :, sl] = (o * gm_ref[0, :, sl].astype(jnp.float32)).astype(jnp.bfloat16)


def _build_band_bias(vec_ref, bias_ref, n_q, n_valid, chunked):
    n_keys = bias_ref.shape[2]
    i = lax.broadcasted_iota(jnp.int32, (n_q, n_keys), 0)
    r = lax.broadcasted_iota(jnp.int32, (n_q, n_keys), 1)
    lo = ((i >> 6) << 6) if chunked else jnp.zeros_like(i)
    for h in range(bias_ref.shape[0]):
        base = jnp.broadcast_to(vec_ref[h:h + 1, :], (n_q, BIAS_VEC))
        rolled = pltpu.roll(base, 0, 1, stride=1, stride_axis=0)
        b = rolled[:, :n_keys] * LOG2E
        bias_ref[h] = jnp.where(r >= lo, jnp.where(r < lo + n_valid, b, _NEG_INF), _NEG_INF)


def _build_band_bias_t(rvec_ref, bias_ref):
    n_keys, n_q = bias_ref.shape[1], bias_ref.shape[2]
    r = lax.broadcasted_iota(jnp.int32, (n_keys, n_q), 0)
    i = lax.broadcasted_iota(jnp.int32, (n_keys, n_q), 1)
    lo = (i >> 6) << 6
    for h in range(bias_ref.shape[0]):
        base = jnp.broadcast_to(rvec_ref[h:h + 1, :], (n_keys, BIAS_VEC))
        rolled = pltpu.roll(base, 0, 1, stride=1, stride_axis=0)
        b = rolled[:, :n_q] * LOG2E
        bias_ref[h] = jnp.where(r >= lo, jnp.where(r < lo + A_WINDOW + CHUNK, b, _NEG_INF), _NEG_INF)


def _band_prompt_kernel(qa_ref, ka_ref, va_ref, ga_ref, qm_ref, gm_ref, mk_ref, mv_ref, rvec_ref,
                        oa_ref, om_ref, kpad, vt_blk, mk_bf, mv_t, bias_ref):
    j = pl.program_id(1)
    n_front = A_WINDOW // BAND_TQ
    n_win = BAND_WIN // BAND_TQ

    @pl.when(jnp.logical_and(pl.program_id(0) == 0, j == 0))
    def _():
        _build_band_bias_t(rvec_ref, bias_ref)

    @pl.when(j == 0)
    def _():
        kpad[:A_WINDOW, :] = jnp.zeros((A_WINDOW, D_A), jnp.bfloat16)
        kpad[A_WINDOW:, :] = ka_ref[0]
        for t in range(vt_blk.shape[0]):
            if t < n_front:
                vt_blk[t] = jnp.zeros((D_A, BAND_TQ), jnp.bfloat16)
            else:
                rows = slice((t - n_front) * BAND_TQ, (t - n_front + 1) * BAND_TQ)
                vt_blk[t] = va_ref[0, rows, :].astype(jnp.float32).T.astype(jnp.bfloat16)
        mk_bf[...] = mk_ref[0].astype(jnp.bfloat16)
        mv_t[...] = mv_ref[0].T.astype(jnp.bfloat16)

    start = pl.multiple_of(j * BAND_TQ, BAND_TQ)

    def store_a(sl, o):
        oa_ref[0, :, sl] = o

    def store_m(sl, o):
        om_ref[0, :, sl] = o

    def run(mask_front):
        if mask_front:
            pos = lax.broadcasted_iota(jnp.int32, (BAND_WIN, BAND_TQ), 0) + j * BAND_TQ
            front = jnp.where(pos >= A_WINDOW, 0.0, _NEG_INF)
        _kq_attend(lambda h: bias_ref[h] + front if mask_front else bias_ref[h],
                   qa_ref[0], ga_ref[0],
                   lambda sl: kpad[pl.ds(start, BAND_WIN), sl],
                   lambda sl: jnp.concatenate([vt_blk[j + u, sl, :] for u in range(n_win)], axis=1),
                   store_a)
        _kq_attend(lambda h: None, qm_ref[0], gm_ref[0], lambda sl: mk_bf[:, sl], lambda sl: mv_t[sl, :],
                   store_m)

    pl.when(j < n_front)(lambda: run(True))
    pl.when(j >= n_front)(lambda: run(False))


def _band_prompt_call(qa, ka, va, ga, qm, gm, mk, mv, bias_rvec):
    B, S, _ = qa.shape
    nq = S // BAND_TQ
    n_mem = mk.shape[1]
    assert nq * BAND_TQ == S and A_WINDOW % BAND_TQ == 0
    qblk = lambda n: pl.BlockSpec((1, BAND_TQ, n), lambda b, j: (b, j, 0))
    full = lambda r, n: pl.BlockSpec((1, r, n), lambda b, j: (b, 0, 0))
    return pl.pallas_call(
        _band_prompt_kernel,
        grid=(B, nq),
        in_specs=[qblk(D_A), full(S, D_A), full(S, D_A), qblk(D_A), qblk(D_M), qblk(D_M),
                  full(n_mem, D_M), full(n_mem, D_M),
                  pl.BlockSpec((H_A, BIAS_VEC), lambda b, j: (0, 0))],
        out_specs=[qblk(D_A), qblk(D_M)],
        out_shape=[jax.ShapeDtypeStruct((B, S, D_A), jnp.bfloat16),
                   jax.ShapeDtypeStruct((B, S, D_M), jnp.bfloat16)],
        scratch_shapes=[pltpu.VMEM((S + A_WINDOW, D_A), jnp.bfloat16),
                        pltpu.VMEM((nq + A_WINDOW // BAND_TQ, D_A, BAND_TQ), jnp.bfloat16),
                        pltpu.VMEM((n_mem, D_M), jnp.bfloat16),
                        pltpu.VMEM((D_M, n_mem), jnp.bfloat16),
                        pltpu.VMEM((H_A, BAND_WIN, BAND_TQ), jnp.float32)],
        compiler_params=_cparams(2),
        name="band_prompt",
    )(qa, ka, va, ga, qm, gm, mk, mv, bias_rvec)


def _band_sample_kernel(qa_ref, kc_ref, vc_ref, kn_ref, vn_ref, ga_ref, qm_ref, gm_ref,
                        mk_ref, mv_ref, vec_ref, oa_ref, om_ref, kcat, vcat, bias_ref):
    P = kc_ref.shape[1]
    T = kn_ref.shape[1]
    pad = kcat.shape[0] - P - T

    @pl.when(pl.program_id(0) == 0)
    def _():
        _build_band_bias(vec_ref, bias_ref, T, P + T, False)

    kcat[:P, :] = kc_ref[0].astype(jnp.bfloat16)
    vcat[:P, :] = vc_ref[0].astype(jnp.bfloat16)
    kcat[P:P + T, :] = kn_ref[0]
    vcat[P:P + T, :] = vn_ref[0]
    zeros = jnp.zeros((pad, D_A), jnp.bfloat16)
    kcat[P + T:, :] = zeros
    vcat[P + T:, :] = zeros
    for p in range(D_A // LANES):
        sl = slice(p * LANES, (p + 1) * LANES)
        o = _pair_attend(qa_ref[0, :, sl], kcat[:, sl], vcat[:, sl],
                         lambda s, hh, p=p: s + bias_ref[2 * p + hh])
        oa_ref[0, :, sl] = (o * ga_ref[0, :, sl].astype(jnp.float32)).astype(jnp.bfloat16)
    _mem_attend(qm_ref, gm_ref, mk_ref, mv_ref, om_ref)


def _band_sample_call(qa, kc, vc, kn, vn, ga, qm, gm, mk, mv, bias_vec):
    B, T, _ = qa.shape
    P = kc.shape[1]
    n_keys = -(-(P + T) // LANES) * LANES
    n_mem = mk.shape[1]
    blk = lambda r, n: pl.BlockSpec((1, r, n), lambda b: (b, 0, 0))
    return pl.pallas_call(
        _band_sample_kernel,
        grid=(B,),
        in_specs=[blk(T, D_A), blk(P, D_A), blk(P, D_A), blk(T, D_A), blk(T, D_A), blk(T, D_A),
                  blk(T, D_M), blk(T, D_M), blk(n_mem, D_M), blk(n_mem, D_M),
                  pl.BlockSpec((H_A, BIAS_VEC), lambda b: (0, 0))],
        out_specs=[blk(T, D_A), blk(T, D_M)],
        out_shape=[jax.ShapeDtypeStruct((B, T, D_A), jnp.bfloat16),
                   jax.ShapeDtypeStruct((B, T, D_M), jnp.bfloat16)],
        scratch_shapes=[pltpu.VMEM((n_keys, D_A), jnp.bfloat16)] * 2
        + [pltpu.VMEM((H_A, T, n_keys), jnp.float32)],
        compiler_params=_cparams(1),
        name="band_sample",
    )(qa, kc, vc, kn, vn, ga, qm, gm, mk, mv, bias_vec)


def _key_to_f32(k):
    bits = k ^ ((k >> 31) & jnp.int32(0x7FFFFFFF))
    return lax.bitcast_convert_type(bits, jnp.float32)


def _count(pred_f32):
    ones = jnp.ones((8, pred_f32.shape[0]), jnp.bfloat16)
    c = jnp.dot(ones, pred_f32.astype(jnp.bfloat16), preferred_element_type=jnp.float32)
    return c[0:1, :]


def _dsa_select(sc_ref, probs, tq, topk, out_ref=None):
    one, zero = jnp.float32(1.0), jnp.float32(0.0)
    dst = sc_ref if out_ref is None else out_ref

    def put(off, tk, mask):
        dst[off:off + tk, :] = mask.astype(dst.dtype)

    live = [(off, tk) for off, tk in probs if tk > topk]
    n_live = len(live)

    def sc_of(n):
        off, tk = live[n]
        return sc_ref[off:off + tk, :]

    if live:
        def thr_step(i, ts):
            bit = lax.shift_left(jnp.int32(1), 31 - i)
            out = []
            for n in range(n_live):
                cand = ts[n] + bit
                thr = _key_to_f32(cand)
                off, tk = live[n]
                n_vec = (tk * 7 // 32) // 64 * 64
                n_mxu = tk - n_vec
                c = _count(jnp.where(sc_ref[off:off + n_mxu, :] >= thr, one, zero))
                if n_vec:
                    c = c + jnp.sum(jnp.where(sc_ref[off + n_mxu:off + tk, :] >= thr, one, zero),
                                    axis=0, keepdims=True)
                out.append(jnp.where(c >= topk, cand, ts[n]))
            return tuple(out)

        start = tuple(jnp.full((1, tq), _INT_MIN, jnp.int32) for _ in range(n_live))
        t_keys = lax.fori_loop(0, 32, thr_step, start)
        thrs = [jnp.where(t == _INT_MIN, _NEG_INF, _key_to_f32(t)) for t in t_keys]

        for n in range(n_live):
            off, tk = live[n]
            c_ge = _count(jnp.where(sc_of(n) >= thrs[n], one, zero))

            def tie_search(n=n, off=off, tk=tk):
                need = topk - _count(jnp.where(sc_of(n) > thrs[n], one, zero))
                tile = 2 * LANES

                def step(first, size, state):
                    before, jmax = state
                    r = lax.broadcasted_iota(jnp.int32, (size, size), 0)
                    c = lax.broadcasted_iota(jnp.int32, (size, size), 1)
                    lower = jnp.where(c <= r, one, zero).astype(jnp.bfloat16)
                    tied = jnp.where(sc_ref[pl.ds(off + first, size), :] == thrs[n], one, zero)
                    rank = before + jnp.dot(lower, tied.astype(jnp.bfloat16),
                                            preferred_element_type=jnp.float32)
                    kidx = (lax.broadcasted_iota(jnp.int32, (size, tq), 0) + first).astype(jnp.float32)
                    taken = jnp.where(tied > 0, jnp.where(rank <= need, kidx, -one), -one)
                    jmax = jnp.maximum(jmax, jnp.max(taken, axis=0, keepdims=True))
                    return rank[size - 1:size, :], jmax

                state = (jnp.zeros((1, tq), jnp.float32), jnp.full((1, tq), -1.0, jnp.float32))
                n_full, rest = divmod(tk, tile)
                state = lax.fori_loop(
                    0, n_full, lambda t, st: step(pl.multiple_of(t * tile, tile), tile, st), state)
                if rest:
                    state = step(n_full * tile, rest, state)
                return state[1].astype(jnp.int32)

            def keep_with_ties(n=n, off=off, tk=tk, tie_search=tie_search):
                jmax = tie_search()
                sc = sc_of(n)
                kidx = lax.broadcasted_iota(jnp.int32, (tk, tq), 0)
                fin = jnp.where(jnp.abs(sc) < jnp.float32(jnp.inf), zero, _NEG_INF)
                put(off, tk, jnp.where(
                    sc > thrs[n], fin,
                    jnp.where(sc == thrs[n], jnp.where(kidx <= jmax, fin, _NEG_INF), _NEG_INF)))

            def keep_plain(n=n, off=off, tk=tk):
                sc = sc_of(n)
                fin = jnp.where(jnp.abs(sc) < jnp.float32(jnp.inf), zero, _NEG_INF)
                put(off, tk, jnp.where(sc >= thrs[n], fin, _NEG_INF))

            has_excess = jnp.max(jnp.where(c_ge > topk, one, zero)) > 0
            lax.cond(has_excess, keep_with_ties, keep_plain)

    for off, tk in probs:
        if tk <= topk:
            sc = sc_ref[off:off + tk, :]
            put(off, tk, jnp.where(jnp.abs(sc) < jnp.float32(jnp.inf), zero, _NEG_INF))


def _dsa_scores(limit, qi, kiwi, ki4):
    tk, tq = ki4.shape[0], qi.shape[0]
    kiwi_t = kiwi.T
    acc = jnp.zeros((tk, tq), jnp.float32)
    heads_per_slab = LANES // D_IDX

    def head_q(h):
        slab = qi[:, (h // heads_per_slab) * LANES:(h // heads_per_slab + 1) * LANES]
        lo = (h % heads_per_slab) * D_IDX
        return _keep_lanes(slab, lo, lo + D_IDX)

    for h in range(0, H_IDX, 2):
        d = lax.dot_general(ki4, jnp.concatenate([head_q(h), head_q(h + 1)], axis=0), _NT,
                            preferred_element_type=jnp.float32)
        for e in range(2):
            w = kiwi_t[D_IDX + h + e:D_IDX + h + e + 1, :] * ((D_IDX ** -0.5) * (H_IDX ** -0.5))
            acc = acc + w * jnp.maximum(d[:, e * tq:(e + 1) * tq], 0.0)
    if limit is not None:
        kidx = lax.broadcasted_iota(jnp.int32, (tk, tq), 0)
        acc = jnp.where(kidx < limit, acc, _NEG_INF)
    return acc


def _kq_attend(bias_of, q, g, k_slab_of, vt_slab_of, store):
    tq = q.shape[0]
    for p in range(q.shape[1] // LANES):
        sl = slice(p * LANES, (p + 1) * LANES)
        q_slab = q[:, sl]
        q_pair = jnp.concatenate([_keep_lanes(q_slab, 0, D_HEAD), _keep_lanes(q_slab, D_HEAD, LANES)],
                                 axis=0)
        s_pair = lax.dot_general(k_slab_of(sl), q_pair, _NT, preferred_element_type=jnp.float32)
        probs, inv_l = [], []
        for hh in range(2):
            s = s_pair[:, hh * tq:(hh + 1) * tq]
            bias = bias_of(2 * p + hh)
            if bias is not None:
                s = s + bias
            m = jnp.max(s, axis=0, keepdims=True)
            pr = jnp.exp2(s - m)
            inv_l.append(1.0 / jnp.sum(pr, axis=0, keepdims=True))
            probs.append(pr.astype(jnp.bfloat16))
        o_pair = jnp.dot(vt_slab_of(sl), jnp.concatenate(probs, axis=1),
                         preferred_element_type=jnp.float32)
        ot = jnp.concatenate([o_pair[:D_HEAD, :tq] * inv_l[0], o_pair[D_HEAD:, tq:] * inv_l[1]], axis=0)
        store(sl, (ot.T * g[:, sl].astype(jnp.float32)).astype(jnp.bfloat16))


def _dsa_blocks(S, tq):
    return [(tq * j * (j + 1) // 2, (j + 1) * tq) for j in range(S // tq)]


def _own_region():
    return pl.when(pl.program_id(0) >= 0)


def _dsa_select_kernel(qi_ref, kiwi_ref, ki4_ref, keep_ref, sc_ref, *, topk, tq):
    own_region = _own_region()
    probs = _dsa_blocks(ki4_ref.shape[1], tq)
    for j, (off, tk) in enumerate(probs):
        rows = slice(j * tq, (j + 1) * tq)

        @own_region
        def _(j=j, tk=tk, off=off, rows=rows):
            qpos = j * tq + lax.broadcasted_iota(jnp.int32, (1, tq), 1)
            limit = ((qpos >> 6) + 1) << 6
            sc_ref[off:off + tk, :] = _dsa_scores(limit, qi_ref[0, rows, :], kiwi_ref[0, rows, :],
                                                  ki4_ref[0, :tk, :])

    _dsa_select(sc_ref, probs, tq, topk, out_ref=keep_ref.at[0])


def _dsa_attend_kernel(q_ref, g_ref, k_ref, v_ref, keep_ref, o_ref, vt, *, tq):
    own_region = _own_region()
    S = k_ref.shape[1]
    for t in range(S // tq):
        vt[t] = v_ref[0, t * tq:(t + 1) * tq, :].astype(jnp.float32).T.astype(jnp.bfloat16)
    for j, (off, tk) in enumerate(_dsa_blocks(S, tq)):
        rows = slice(j * tq, (j + 1) * tq)

        @own_region
        def _(j=j, tk=tk, off=off, rows=rows):
            def store(sl, o):
                o_ref[0, rows, sl] = o

            bias = keep_ref[0, off:off + tk, :].astype(jnp.float32)
            _kq_attend(lambda h: bias, q_ref[0, rows, :], g_ref[0, rows, :],
                       lambda sl: k_ref[0, :tk, sl],
                       lambda sl: jnp.concatenate([vt[t, sl, :] for t in range(j + 1)], axis=1), store)


def _dsa_prompt_call(qb, qi, kiwi, gb, kb, vb, ki4):
    B, S, _ = qb.shape
    tq = DSA_TQ
    nq = S // tq
    assert CHUNK == 64 and nq * tq == S
    topk = min(TOPK_MAX, S // 4)
    n_rows = tq * nq * (nq + 1) // 2
    full = lambda n: pl.BlockSpec((1, S, n), lambda b: (b, 0, 0))
    keep_spec = pl.BlockSpec((1, n_rows, tq), lambda b: (b, 0, 0))
    keep = pl.pallas_call(
        functools.partial(_dsa_select_kernel, topk=topk, tq=tq),
        grid=(B,),
        in_specs=[full(H_IDX * D_IDX), full(LANES), full(LANES)],
        out_specs=keep_spec,
        out_shape=jax.ShapeDtypeStruct((B, n_rows, tq), jnp.bfloat16),
        scratch_shapes=[pltpu.VMEM((n_rows, tq), jnp.float32)],
        compiler_params=_cparams(1),
        name="dsa_select",
    )(qi, kiwi, ki4)
    return pl.pallas_call(
        functools.partial(_dsa_attend_kernel, tq=tq),
        grid=(B,),
        in_specs=[full(D_B), full(D_B), full(D_B), full(D_B), keep_spec],
        out_specs=full(D_B),
        out_shape=jax.ShapeDtypeStruct((B, S, D_B), jnp.bfloat16),
        scratch_shapes=[pltpu.VMEM((nq, D_B, tq), jnp.bfloat16)],
        compiler_params=_cparams(1),
        name="dsa_attend",
    )(qb, gb, kb, vb, keep)


def _dsa_sample_kernel(q_ref, qi_ref, kiwi_ref, g_ref, kc_ref, vc_ref, kic_ref,
                       kn_ref, vn_ref, kin_ref, o_ref, kbf, vt, ki4s, sc_ref, *, topk):
    P = kc_ref.shape[1]
    T = kn_ref.shape[1]
    tq = q_ref.shape[1]
    kbf[:P, :] = kc_ref[0].astype(jnp.bfloat16)
    kbf[P:, :] = kn_ref[0].astype(jnp.bfloat16)
    vt[:, :P] = vc_ref[0].T.astype(jnp.bfloat16)
    vn_tile = jnp.concatenate([vn_ref[0], jnp.zeros((LANES - T, D_B), jnp.float32)], axis=0)
    vt[:, P:] = vn_tile.T[:, :T].astype(jnp.bfloat16)
    ki4s[:P, :] = kic_ref[0]
    ki4s[P:, :] = kin_ref[0]
    sc_ref[...] = _dsa_scores(None, qi_ref[0], kiwi_ref[0], ki4s[...])
    _dsa_select(sc_ref, [(0, P + T)], tq, topk)

    def store(sl, o):
        o_ref[0, :, sl] = o

    _kq_attend(lambda h: sc_ref[...], q_ref[0], g_ref[0], lambda sl: kbf[:, sl], lambda sl: vt[sl, :],
               store)


def _dsa_sample_call(qb, qi, kiwi, gb, kc, vc, kic, kn, vn, kin):
    B, tq, _ = qb.shape
    P, T = kc.shape[1], kn.shape[1]
    tk = P + T
    topk = min(TOPK_MAX, tk // 4)
    blk = lambda r, n: pl.BlockSpec((1, r, n), lambda b: (b, 0, 0))
    return pl.pallas_call(
        functools.partial(_dsa_sample_kernel, topk=topk),
        grid=(B,),
        in_specs=[blk(tq, D_B), blk(tq, H_IDX * D_IDX), blk(tq, LANES), blk(tq, D_B),
                  blk(P, D_B), blk(P, D_B), blk(P, LANES),
                  blk(T, D_B), blk(T, D_B), blk(T, LANES)],
        out_specs=blk(tq, D_B),
        out_shape=jax.ShapeDtypeStruct((B, tq, D_B), jnp.bfloat16),
        scratch_shapes=[pltpu.VMEM((tk, D_B), jnp.bfloat16),
                        pltpu.VMEM((D_B, tk), jnp.bfloat16),
                        pltpu.VMEM((tk, LANES), jnp.bfloat16),
                        pltpu.VMEM((tk, tq), jnp.float32)],
        compiler_params=_cparams(1),
        name="dsa_sample",
    )(qb, qi, kiwi, gb, kc, vc, kic, kn, vn, kin)


def _merge_kernel(x_ref, oa_ref, ob_ref, om_ref, w_ref, g_ref, y_ref, *, last):
    acc = x_ref[...]
    acc = acc + jnp.dot(oa_ref[...], w_ref[:D_A, :], preferred_element_type=jnp.float32)
    acc = acc + jnp.dot(ob_ref[...], w_ref[D_A:D_A + D_B, :], preferred_element_type=jnp.float32)
    acc = acc + jnp.dot(om_ref[...], w_ref[D_A + D_B:, :], preferred_element_type=jnp.float32)
    y_ref[...] = _rms_scale(acc, g_ref[...]) if last else acc


def _merge_call(x2d, oa, ob, om, w, g, last):
    R, D = x2d.shape
    tm = min(MERGE_TM, R)
    assert R % tm == 0
    row = lambda n: pl.BlockSpec((tm, n), lambda i: (i, 0))
    return pl.pallas_call(
        functools.partial(_merge_kernel, last=last),
        grid=(R // tm,),
        in_specs=[row(D), row(D_A), row(D_B), row(D_M),
                  pl.BlockSpec((D_A + D_B + D_M, D), lambda i: (0, 0)),
                  pl.BlockSpec((1, D), lambda i: (0, 0))],
        out_specs=row(D),
        out_shape=jax.ShapeDtypeStruct((R, D), jnp.float32),
        compiler_params=_cparams(1),
        name="merge",
    )(x2d, oa, ob, om, w, g)


def _pack_w_in(w):
    D = w.shape[0]
    ki = w[:, C_KIWI:C_KIWI + D_IDX]
    wi = w[:, C_KIWI + D_IDX:C_KIWI + D_IDX + H_IDX]
    pad = jnp.zeros((D, LANES - D_IDX - H_IDX), w.dtype)
    packed = jnp.concatenate([w[:, :C_KIWI], ki, wi, pad, jnp.tile(ki, (1, LANES // D_IDX))], axis=1)
    return packed.astype(jnp.bfloat16)


def _rope_tables(pos):
    posf = pos.astype(jnp.float32)

    def tables(d):
        half = d // 2
        inv_freq = ROPE_THETA ** (-jnp.arange(half, dtype=jnp.float32) * 2.0 / d)
        ang = posf[:, None] * inv_freq[None, :]
        cos, sin = jnp.cos(ang), jnp.sin(ang)
        return jnp.concatenate([cos, cos], axis=1), jnp.concatenate([-sin, sin], axis=1)

    cos64, sin64 = tables(D_HEAD)
    cos32, sin32 = tables(D_IDX)
    n = pos.shape[0]
    rest = LANES - D_IDX
    cos_kw = jnp.concatenate([cos32, jnp.ones((n, rest), jnp.float32)], axis=1)
    sin_kw = jnp.concatenate([sin32, jnp.zeros((n, rest), jnp.float32)], axis=1)
    rep = lambda t, d: jnp.tile(t, (1, LANES // d))
    return jnp.stack([rep(cos64, D_HEAD), rep(sin64, D_HEAD), rep(cos32, D_IDX), rep(sin32, D_IDX),
                      cos_kw, sin_kw])


def _band_bias_vec(table):
    n_far = A_WINDOW - REL_CLIP + 1
    lo_idx = A_WINDOW + REL_CLIP - (BAND_WIN - 1)
    assert lo_idx >= 0 and BIAS_VEC >= BAND_WIN + BAND_TQ - 1
    far = table[:, 2 * REL_CLIP:2 * REL_CLIP + 1].astype(jnp.float32)
    mid = table[:, lo_idx:2 * REL_CLIP][:, ::-1].astype(jnp.float32)
    vec = jnp.concatenate([jnp.tile(far, (1, n_far)), mid, jnp.tile(far, (1, BIAS_VEC - BAND_WIN))], axis=1)
    rvec = jnp.concatenate([vec[:, :1], vec[:, 1:][:, ::-1]], axis=1)
    return vec, rvec


def kernel(x_prompt, x_sample, mem_prompt, cache_a_k, cache_a_v, cache_b_k, cache_b_v, cache_b_kidx,
           cache_mem_k, cache_mem_v, norm_mix_g, w_in, rel_bias_a, norm_mem_g, w_mem_kv, w_out,
           norm_final_g):
    B, S, D = x_prompt.shape
    Bs, T, _ = x_sample.shape
    depth = w_in.shape[0]
    P = cache_b_k.shape[2]
    Pa = cache_a_k.shape[2]
    n_mem = mem_prompt.shape[1]
    keep = min(A_WINDOW, S)
    assert Pa == A_WINDOW and T <= CHUNK and LANES % T == 0

    rope_p = _rope_tables(jnp.arange(S))
    rope_s = jnp.tile(_rope_tables(P + jnp.arange(T)), (1, Bs, 1))
    g_final = norm_final_g.reshape(1, D)

    xp, xs = x_prompt, x_sample
    outs_p = [[] for _ in range(7)]
    outs_s = [[] for _ in range(5)]
    for l in range(depth):
        last = l == depth - 1
        w = _pack_w_in(w_in[l])
        g_mix = norm_mix_g[l].reshape(1, D)
        w_o = w_out[l].astype(jnp.bfloat16)
        bias_vec, bias_rvec = _band_bias_vec(rel_bias_a[l])

        (qa, ka, va, ga, qb, kb, vb, gb, qm, gm, qi, kiwi, ki4, ki, kbb, vbb, ak, av) = _proj_call(
            xp, g_mix, w, rope_p, keep)
        mk, mv = _memkv_call(mem_prompt.reshape(B * n_mem, D), norm_mem_g[l].reshape(1, D),
                             w_mem_kv[l].astype(jnp.bfloat16))
        mk = mk.reshape(B, n_mem, D_M)
        mv = mv.reshape(B, n_mem, D_M)
        oa, om = _band_prompt_call(qa, ka, va, ga, qm, gm, mk, mv, bias_rvec)
        ob = _dsa_prompt_call(qb, qi, kiwi, gb, kbb, vbb, ki4)
        xp = _merge_call(xp.reshape(B * S, D), oa.reshape(B * S, D_A), ob.reshape(B * S, D_B),
                         om.reshape(B * S, D_M), w_o, g_final, last).reshape(B, S, D)
        for lst, t in zip(outs_p, (ak.reshape(B, keep, H_A, D_HEAD), av.reshape(B, keep, H_A, D_HEAD),
                                   kb.reshape(B, S, H_B, D_HEAD), vb.reshape(B, S, H_B, D_HEAD), ki,
                                   mk.reshape(B, n_mem, H_M, D_HEAD), mv.reshape(B, n_mem, H_M, D_HEAD))):
            lst.append(t)

        (qa, ka, va, ga, qb, kb, vb, gb, qm, gm, qi, kiwi, ki4, ki, kbb, vbb, ak, av) = _proj_call(
            xs.reshape(1, Bs * T, D), g_mix, w, rope_s, Bs * T)
        per_b = lambda t: t.reshape(Bs, T, t.shape[-1])
        oa, om = _band_sample_call(
            per_b(qa), cache_a_k[l].reshape(Bs, Pa, D_A), cache_a_v[l].reshape(Bs, Pa, D_A),
            per_b(ka), per_b(va), per_b(ga), per_b(qm), per_b(gm),
            cache_mem_k[l].reshape(Bs, n_mem, D_M), cache_mem_v[l].reshape(Bs, n_mem, D_M), bias_vec)
        rep = lambda t: jnp.tile(per_b(t), (1, LANES // T, 1))
        kic = jnp.tile(cache_b_kidx[l], (1, 1, LANES // D_IDX)).astype(jnp.bfloat16)
        ob = _dsa_sample_call(rep(qb), rep(qi), rep(kiwi), rep(gb),
                              cache_b_k[l].reshape(Bs, P, D_B), cache_b_v[l].reshape(Bs, P, D_B), kic,
                              per_b(kb), per_b(vb), per_b(ki4))[:, :T]
        xs = _merge_call(xs.reshape(Bs * T, D), oa.reshape(Bs * T, D_A), ob.reshape(Bs * T, D_B),
                         om.reshape(Bs * T, D_M), w_o, g_final, last).reshape(Bs, T, D)
        for lst, t in zip(outs_s, (ak.reshape(Bs, T, H_A, D_HEAD), av.reshape(Bs, T, H_A, D_HEAD),
                                   kb.reshape(Bs, T, H_B, D_HEAD), vb.reshape(Bs, T, H_B, D_HEAD),
                                   ki.reshape(Bs, T, D_IDX))):
            lst.append(t)

    st = lambda ts: jnp.stack(ts, axis=0)
    return (xp, xs) + tuple(st(t) for t in outs_p) + tuple(st(t) for t in outs_s)
```
